```python
import jax, jax.numpy as jnp
from jax import lax
import numpy as np

D_MODEL = 2048
BATCH = 2
SEQ = 4096
DEPTH = 4

GRID_W = 64
CTX_LEN = 256
HEAD_DIM = 64
BLOCK = 128
WINDOW = 128
ROPE_THETA = 10000.0
A_HEADS = D_MODEL // 256
A_KV_HEADS = A_HEADS // 4
C_HEADS = D_MODEL // 256
C_KV_HEADS = C_HEADS // 4
B_HEADS = D_MODEL // 128
B_WIDTH = B_HEADS * HEAD_DIM
LORA_DECAY = 96
LORA_ICLR = 96
LORA_GATE = 256
D_FF = 4 * D_MODEL
A_Q_W = A_HEADS * HEAD_DIM
A_KV_W = A_KV_HEADS * HEAD_DIM
C_Q_W = C_HEADS * HEAD_DIM
C_KV_W = C_KV_HEADS * HEAD_DIM
A_IN = A_Q_W + 2 * A_KV_W
B_IN = 3 * B_WIDTH + 2 * LORA_DECAY + 2 * LORA_ICLR + LORA_GATE
C_IN = C_Q_W + 2 * C_KV_W
N_IN = A_IN + B_IN + C_IN
MIX_W = A_Q_W + B_WIDTH + C_Q_W
B_SPLITS = (B_WIDTH, 2 * B_WIDTH, 3 * B_WIDTH, 3 * B_WIDTH + LORA_DECAY, 3 * B_WIDTH + 2 * LORA_DECAY, 3 * B_WIDTH + 2 * LORA_DECAY + LORA_ICLR, 3 * B_WIDTH + 2 * LORA_DECAY + 2 * LORA_ICLR)
NORM_EPS = 1e-6
GN_EPS = 64e-5
NEG_INF = -1e30

kernel_name = "hybrid_parallel_heads_dit_block"


def rms_norm(x, g):
    xf = x.astype(jnp.float32)
    y = xf * lax.rsqrt(jnp.mean(xf * xf, -1, keepdims=True) + NORM_EPS)
    return (y * g.astype(jnp.float32)).astype(x.dtype)


def axial_rope_tables(length):
    rows = length // GRID_W
    row = jnp.broadcast_to(jnp.arange(rows)[:, None], (rows, GRID_W)).reshape(-1)
    col = jnp.broadcast_to(jnp.arange(GRID_W)[None, :], (rows, GRID_W)).reshape(-1)
    n_freq = HEAD_DIM // 4
    inv = ROPE_THETA ** (-jnp.arange(n_freq, dtype=jnp.float32) / n_freq)
    ang = jnp.concatenate([row[:, None].astype(jnp.float32) * inv, col[:, None].astype(jnp.float32) * inv], -1)
    return jnp.cos(ang), jnp.sin(ang)


def apply_rope(x, cos, sin):
    half = HEAD_DIM // 2
    xf = x.astype(jnp.float32)
    x1, x2 = xf[..., :half], xf[..., half:]
    c, s = cos[:, None, :], sin[:, None, :]
    return jnp.concatenate([x1 * c - x2 * s, x1 * s + x2 * c], -1).astype(x.dtype)


def joint_softmax(logits, sink=None):
    sizes = [s.shape[-1] for s in logits]
    parts = list(logits)
    if sink is not None:
        parts.append(jnp.broadcast_to(sink.astype(jnp.float32), logits[0].shape[:-1] + (1,)))
    p = jax.nn.softmax(jnp.concatenate(parts, -1), axis=-1)
    out, start = [], 0
    for n in sizes:
        out.append(p[..., start:start + n])
        start += n
    return out


def attn_heads(p, n_heads, n_kv, q_g, k_g):
    bsz, t = p.shape[:2]
    q, k, v = jnp.split(p, (n_heads * HEAD_DIM, (n_heads + n_kv) * HEAD_DIM), -1)
    q = rms_norm(q.reshape(bsz, t, n_heads, HEAD_DIM), q_g)
    k = rms_norm(k.reshape(bsz, t, n_kv, HEAD_DIM), k_g)
    return q, k, v.reshape(bsz, t, n_kv, HEAD_DIM)


def window_attention(q, k, v, kc, vc, sink):
    bsz, t, h, dh = q.shape
    hk = k.shape[2]
    nb = t // BLOCK
    scale = dh ** -0.5
    qb = q.reshape(bsz, nb, BLOCK, hk, h // hk, dh)
    pad = ((0, 0), (BLOCK, BLOCK), (0, 0), (0, 0))
    kp, vp = jnp.pad(k, pad), jnp.pad(v, pad)
    idx = jnp.arange(nb)[:, None] * BLOCK + jnp.arange(3 * BLOCK)[None, :]
    kb, vb = kp[:, idx], vp[:, idx]
    qpos = jnp.arange(nb)[:, None] * BLOCK + jnp.arange(BLOCK)[None, :]
    kpos = idx - BLOCK
    valid = (jnp.abs(qpos[:, :, None] - kpos[:, None, :]) <= WINDOW) & (kpos[:, None, :] >= 0) & (kpos[:, None, :] < t)
    s_lat = jnp.einsum('bnqhgd,bnkhd->bnhgqk', qb, kb).astype(jnp.float32) * scale
    s_lat = jnp.where(valid[None, :, None, None], s_lat, NEG_INF)
    s_ctx = jnp.einsum('bnqhgd,bchd->bnhgqc', qb, kc).astype(jnp.float32) * scale
    p_lat, p_ctx = joint_softmax([s_lat, s_ctx], sink)
    o = (jnp.einsum('bnhgqk,bnkhd->bnqhgd', p_lat.astype(v.dtype), vb)
         + jnp.einsum('bnhgqc,bchd->bnqhgd', p_ctx.astype(vc.dtype), vc))
    return o.reshape(bsz, t, h * dh)


def global_attention(q, k, v, kc, vc):
    bsz, t, h, dh = q.shape
    hk = k.shape[2]
    nb = t // BLOCK
    scale = dh ** -0.5
    qb = jnp.moveaxis(q.reshape(bsz, nb, BLOCK, hk, h // hk, dh), 1, 0)

    def one_block(qblk):
        s_lat = jnp.einsum('bqhgd,bkhd->bhgqk', qblk, k).astype(jnp.float32) * scale
        s_ctx = jnp.einsum('bqhgd,bchd->bhgqc', qblk, kc).astype(jnp.float32) * scale
        p_lat, p_ctx = joint_softmax([s_lat, s_ctx])
        return (jnp.einsum('bhgqk,bkhd->bqhgd', p_lat.astype(v.dtype), v)
                + jnp.einsum('bhgqc,bchd->bqhgd', p_ctx.astype(vc.dtype), vc))

    o = lax.map(one_block, qb)
    return jnp.moveaxis(o, 0, 1).reshape(bsz, t, h * dh)


def context_attention(q, k, v, sink):
    bsz, cn, h, dh = q.shape
    hk = k.shape[2]
    qg = q.reshape(bsz, cn, hk, h // hk, dh)
    s = jnp.einsum('bqhgd,bkhd->bhgqk', qg, k).astype(jnp.float32) * dh ** -0.5
    (p,) = joint_softmax([s], sink)
    o = jnp.einsum('bhgqk,bkhd->bqhgd', p.astype(v.dtype), v)
    return o.reshape(bsz, cn, h * dh)


def centred_shift(p, mu_prev, mu_next):
    prev = jnp.pad(p, ((0, 0), (1, 0), (0, 0)))[:, :-1]
    nxt = jnp.pad(p, ((0, 0), (0, 1), (0, 0)))[:, 1:]
    return p + mu_prev * (prev - p) + mu_next * (nxt - p)


def rwkv_streams(p, w0, w_up, a0, a_up, g_up, k_k, k_a):
    p = p.astype(jnp.float32)
    r, k, v, wd_f, wd_b, ad_f, ad_b, gd = jnp.split(p, B_SPLITS, -1)
    bsz, t = p.shape[:2]
    heads = lambda z: z.reshape(bsz, t, B_HEADS, HEAD_DIM)
    kk = heads(k * k_k)
    kk = kk / jnp.maximum(jnp.sqrt(jnp.sum(kk * kk, -1, keepdims=True)), 1e-12)
    decay, keys, iclr = [], [], []
    for d, (wd, ad) in enumerate(((wd_f, ad_f), (wd_b, ad_b))):
        w_log = -jax.nn.softplus(-(w0[d] + jnp.tanh(wd) @ w_up[d])) - 0.5
        a = jax.nn.sigmoid(a0[d] + ad @ a_up[d])
        decay.append(heads(jnp.exp(-jnp.exp(w_log))))
        keys.append(heads(k * (1.0 + (a - 1.0) * k_a)))
        iclr.append(heads(a))
    g = jax.nn.sigmoid(gd) @ g_up
    return dict(r=heads(r), v=heads(v), kk=kk, decay=decay, k=keys, a=iclr, g=g)


def wkv7_scan(state0, s, d, reverse):
    tm = lambda z: jnp.swapaxes(z, 0, 1)
    xs = (tm(s['r']), tm(s['decay'][d]), tm(s['k'][d]), tm(s['v']), tm(s['kk']), tm(s['a'][d]))

    def step(S, inp):
        r_t, w_t, k_t, v_t, kk_t, a_t = inp
        sa = jnp.einsum('bhvk,bhk->bhv', S, kk_t)
        S = S * w_t[:, :, None, :] - sa[..., None] * (kk_t * a_t)[:, :, None, :] + v_t[..., None] * k_t[:, :, None, :]
        return S, jnp.einsum('bhvk,bhk->bhv', S, r_t)

    s_fin, y = lax.scan(step, state0, xs, reverse=reverse)
    return s_fin, jnp.swapaxes(y, 0, 1)


def rwkv_readout(ys, s, r_k, gn_g, gn_b):
    y = ys[0] + ys[1]
    bsz, t = y.shape[:2]
    mu = jnp.mean(y, -1, keepdims=True)
    var = jnp.mean(jnp.square(y - mu), -1, keepdims=True)
    yn = ((y - mu) * lax.rsqrt(var + GN_EPS)).reshape(bsz, t, B_WIDTH) * gn_g + gn_b
    rk = r_k.reshape(B_HEADS, HEAD_DIM)
    bonus = jnp.sum(s['r'] * (s['k'][0] + s['k'][1]) * rk, -1, keepdims=True) * s['v']
    return (yn + bonus.reshape(bsz, t, B_WIDTH)) * s['g']


def rwkv_mixer(p_c, p_l, mu, w0, w_up, a0, a_up, g_up, k_k, k_a, r_k, gn_g, gn_b, need_ctx_out):
    s_c = rwkv_streams(centred_shift(p_c, mu[0], mu[1]), w0, w_up, a0, a_up, g_up, k_k, k_a)
    s_l = rwkv_streams(centred_shift(p_l, mu[0], mu[1]), w0, w_up, a0, a_up, g_up, k_k, k_a)
    state0 = jnp.zeros((p_l.shape[0], B_HEADS, HEAD_DIM, HEAD_DIM), jnp.float32)
    ys_c, ys_l = [], []
    for d in range(2):
        st_c, y_c = wkv7_scan(state0, s_c, d, d == 1)
        _, y_l = wkv7_scan(st_c, s_l, d, d == 1)
        ys_c.append(y_c)
        ys_l.append(y_l)
    out_l = rwkv_readout(ys_l, s_l, r_k, gn_g, gn_b).astype(p_l.dtype)
    out_c = rwkv_readout(ys_c, s_c, r_k, gn_g, gn_b).astype(p_c.dtype) if need_ctx_out else None
    return out_c, out_l


def sqrelu_mlp(u, w1, w2):
    return jnp.square(jax.nn.relu(u @ w1)) @ w2


def setup_inputs(seed: int = 0) -> dict:
    key = jax.random.key(seed)
    ks = jax.random.split(key, 32)
    f32 = jnp.float32
    nrm = lambda k, shape, s: jax.random.normal(k, shape, f32) * s
    return {
        'x': nrm(ks[0], (BATCH, SEQ, D_MODEL), 1.0),
        'c': nrm(ks[1], (BATCH, D_MODEL), 1.0),
        'ctx': nrm(ks[2], (BATCH, CTX_LEN, D_MODEL), 1.0),
        'c_ctx': nrm(ks[3], (D_MODEL,), 1.0),
        'ada_w': nrm(ks[4], (DEPTH, D_MODEL, 6 * D_MODEL), 0.5 * D_MODEL ** -0.5),
        'ada_b': nrm(ks[5], (DEPTH, 6 * D_MODEL), 0.02),
        'norm1_g': 1.0 + nrm(ks[6], (DEPTH, D_MODEL), 0.02),
        'norm2_g': 1.0 + nrm(ks[7], (DEPTH, D_MODEL), 0.02),
        'w_in': nrm(ks[8], (DEPTH, D_MODEL, N_IN), D_MODEL ** -0.5),
        'a_q_norm': 1.0 + nrm(ks[9], (DEPTH, HEAD_DIM), 0.02),
        'a_k_norm': 1.0 + nrm(ks[10], (DEPTH, HEAD_DIM), 0.02),
        'a_sink': nrm(ks[11], (DEPTH, A_HEADS), 0.5),
        'c_q_norm': 1.0 + nrm(ks[12], (DEPTH, HEAD_DIM), 0.02),
        'c_k_norm': 1.0 + nrm(ks[13], (DEPTH, HEAD_DIM), 0.02),
        'shift_mu': jax.random.uniform(ks[14], (DEPTH, 2, B_IN), f32, 0.0, 0.5),
        'decay_w0': jax.random.uniform(ks[15], (DEPTH, 2, B_WIDTH), f32, -4.0, 1.0),
        'decay_up': nrm(ks[16], (DEPTH, 2, LORA_DECAY, B_WIDTH), 0.5 * LORA_DECAY ** -0.5),
        'iclr_a0': nrm(ks[17], (DEPTH, 2, B_WIDTH), 0.5),
        'iclr_up': nrm(ks[18], (DEPTH, 2, LORA_ICLR, B_WIDTH), 0.5 * LORA_ICLR ** -0.5),
        'gate_up': nrm(ks[19], (DEPTH, LORA_GATE, B_WIDTH), LORA_GATE ** -0.5),
        'k_k': 0.85 + nrm(ks[20], (DEPTH, B_WIDTH), 0.02),
        'k_a': 1.0 + nrm(ks[21], (DEPTH, B_WIDTH), 0.02),
        'r_k': nrm(ks[22], (DEPTH, B_WIDTH), 0.1),
        'gn_g': 1.0 + nrm(ks[23], (DEPTH, B_WIDTH), 0.02),
        'gn_b': nrm(ks[24], (DEPTH, B_WIDTH), 0.02),
        'w_out': nrm(ks[25], (DEPTH, MIX_W, D_MODEL), MIX_W ** -0.5),
        'mlp_w1': nrm(ks[26], (DEPTH, D_MODEL, D_FF), D_MODEL ** -0.5),
        'mlp_w2': nrm(ks[27], (DEPTH, D_FF, D_MODEL), D_FF ** -0.5),
    }


def reference(x, c, ctx, c_ctx, ada_w, ada_b, norm1_g, norm2_g, w_in, a_q_norm, a_k_norm, a_sink, c_q_norm, c_k_norm, shift_mu, decay_w0, decay_up, iclr_a0, iclr_up, gate_up, k_k, k_a, r_k, gn_g, gn_b, w_out, mlp_w1, mlp_w2):
    cos, sin = axial_rope_tables(x.shape[1])
    h_lat, h_ctx = x, ctx
    for l in range(DEPTH):
        last = l == DEPTH - 1
        mod_lat = jax.nn.silu(c) @ ada_w[l] + ada_b[l]
        mod_ctx = jax.nn.silu(c_ctx) @ ada_w[l] + ada_b[l]
        sh1_l, sc1_l, g1_l, sh2_l, sc2_l, g2_l = jnp.split(mod_lat[:, None, :], 6, -1)
        sh1_c, sc1_c, g1_c, sh2_c, sc2_c, g2_c = jnp.split(mod_ctx, 6, -1)

        u_lat = rms_norm(h_lat, norm1_g[l]) * (1.0 + sc1_l) + sh1_l
        u_ctx = rms_norm(h_ctx, norm1_g[l]) * (1.0 + sc1_c) + sh1_c
        pA_l, pB_l, pC_l = jnp.split(u_lat @ w_in[l], (A_IN, A_IN + B_IN), -1)
        pA_c, pB_c, pC_c = jnp.split(u_ctx @ w_in[l], (A_IN, A_IN + B_IN), -1)

        qA_l, kA_l, vA_l = attn_heads(pA_l, A_HEADS, A_KV_HEADS, a_q_norm[l], a_k_norm[l])
        qA_l, kA_l = apply_rope(qA_l, cos, sin), apply_rope(kA_l, cos, sin)
        qA_c, kA_c, vA_c = attn_heads(pA_c, A_HEADS, A_KV_HEADS, a_q_norm[l], a_k_norm[l])
        sinkA = a_sink[l].reshape(A_KV_HEADS, A_HEADS // A_KV_HEADS, 1, 1)
        oA_l = window_attention(qA_l, kA_l, vA_l, kA_c, vA_c, sinkA)

        oB_c, oB_l = rwkv_mixer(pB_c, pB_l, shift_mu[l], decay_w0[l], decay_up[l], iclr_a0[l], iclr_up[l], gate_up[l], k_k[l], k_a[l], r_k[l], gn_g[l], gn_b[l], not last)

        qC_l, kC_l, vC_l = attn_heads(pC_l, C_HEADS, C_KV_HEADS, c_q_norm[l], c_k_norm[l])
        qC_l, kC_l = apply_rope(qC_l, cos, sin), apply_rope(kC_l, cos, sin)
        qC_c, kC_c, vC_c = attn_heads(pC_c, C_HEADS, C_KV_HEADS, c_q_norm[l], c_k_norm[l])
        oC_l = global_attention(qC_l, kC_l, vC_l, kC_c, vC_c)

        h_lat = h_lat + g1_l * (jnp.concatenate([oA_l, oB_l, oC_l], -1) @ w_out[l])
        u2_l = rms_norm(h_lat, norm2_g[l]) * (1.0 + sc2_l) + sh2_l
        h_lat = h_lat + g2_l * sqrelu_mlp(u2_l, mlp_w1[l], mlp_w2[l])

        if not last:
            oA_c = context_attention(qA_c, kA_c, vA_c, sinkA)
            oC_c = context_attention(qC_c, kC_c, vC_c, None)
            h_ctx = h_ctx + g1_c * (jnp.concatenate([oA_c, oB_c, oC_c], -1) @ w_out[l])
            u2_c = rms_norm(h_ctx, norm2_g[l]) * (1.0 + sc2_c) + sh2_c
            h_ctx = h_ctx + g2_c * sqrelu_mlp(u2_c, mlp_w1[l], mlp_w2[l])
    return h_lat
```

```python
import functools

import jax
import jax.numpy as jnp
from jax import lax
from jax.experimental import pallas as pl
from jax.experimental.pallas import tpu as pltpu

F32 = jnp.float32
BF16 = jnp.bfloat16

HEAD_DIM = 64
GRID_W = 64
WINDOW = 128
ROPE_THETA = 10000.0
NORM_EPS = 1e-6
GN_EPS = 64e-5
NEG_INF = -1e30
LANES = 128
LORA_PAD = 128
CHUNK = 64
VMEM_LIMIT = 52 * 1024 * 1024


def _dot(a, b, trans_a=False, trans_b=False):
    dn = (((0 if trans_a else 1,), (1 if trans_b else 0,)), ((), ()))
    return lax.dot_general(a, b, dn, preferred_element_type=F32)


def _split(x):
    hi = x.astype(BF16)
    lo = (x - hi.astype(F32)).astype(BF16)
    return hi, lo


def _dot3(a, b, trans_a=False, trans_b=False):
    ah, al = _split(a)
    bh, bl = _split(b)
    kw = dict(trans_a=trans_a, trans_b=trans_b)
    return _dot(ah, bh, **kw) + (_dot(al, bh, **kw) + _dot(ah, bl, **kw))


def _dot1(a, b, trans_a=False, trans_b=False):
    return _dot(a.astype(BF16), b.astype(BF16), trans_a=trans_a, trans_b=trans_b)


def _group_sum(x, e_ref):
    gw = e_ref.shape[0]
    e = e_ref[...]
    outs = []
    for g in range(x.shape[1] // gw):
        hi, lo = _split(x[:, g * gw:(g + 1) * gw])
        outs.append(_dot(hi, e) + _dot(lo, e))
    return outs[0] if len(outs) == 1 else jnp.concatenate(outs, axis=1)


def _cparams(sem):
    return pltpu.CompilerParams(dimension_semantics=sem, vmem_limit_bytes=VMEM_LIMIT)


def _mods_kernel(c_ref, w_ref, b_ref, o_ref):
    c = c_ref[...]
    s = c * jax.nn.sigmoid(c)
    o_ref[0] = _dot3(s, w_ref[0]) + b_ref[0]


def _mods(cvec, ada_w, ada_b):
    depth, d, n = ada_w.shape
    tn = 512
    return pl.pallas_call(
        _mods_kernel,
        grid=(depth, n // tn),
        in_specs=[
            pl.BlockSpec((8, d), lambda l, j: (0, 0)),
            pl.BlockSpec((1, d, tn), lambda l, j: (l, 0, j)),
            pl.BlockSpec((1, 1, tn), lambda l, j: (l, 0, j)),
        ],
        out_specs=pl.BlockSpec((1, 8, tn), lambda l, j: (l, 0, j)),
        out_shape=jax.ShapeDtypeStruct((depth, 8, n), F32),
        compiler_params=_cparams(("parallel", "parallel")),
        name="adaln_mods",
    )(cvec, ada_w, ada_b.reshape(depth, 1, n))


def _modulated_norm(x, g, sc, sh):
    ms = jnp.mean(x * x, axis=-1, keepdims=True)
    return (x * lax.rsqrt(ms + NORM_EPS) * g) * (1.0 + sc) + sh


def _inproj_kernel(h_ref, g_ref, sc_ref, sh_ref, w_ref, o_ref, u_scr):
    @pl.when(pl.program_id(1) == 0)
    def _():
        u_scr[...] = _modulated_norm(h_ref[...], g_ref[...], sc_ref[...], sh_ref[...]).astype(BF16)

    o_ref[...] = _dot(u_scr[...], w_ref[...])


def _inproj(h, gain, mods4, w, cfg):
    rows, d = h.shape
    n = w.shape[1]
    tm, tn = cfg["tm"], 768
    modrow = cfg["modrow"]
    return pl.pallas_call(
        _inproj_kernel,
        grid=(rows // tm, n // tn),
        in_specs=[
            pl.BlockSpec((tm, d), lambda i, j: (i, 0)),
            pl.BlockSpec((1, d), lambda i, j: (0, 0)),
            pl.BlockSpec((None, None, 1, d), lambda i, j: (modrow(i), 1, 0, 0)),
            pl.BlockSpec((None, None, 1, d), lambda i, j: (modrow(i), 0, 0, 0)),
            pl.BlockSpec((d, tn), lambda i, j: (0, j)),
        ],
        out_specs=pl.BlockSpec((tm, tn), lambda i, j: (i, j)),
        out_shape=jax.ShapeDtypeStruct((rows, n), F32),
        scratch_shapes=[pltpu.VMEM((tm, d), BF16)],
        compiler_params=_cparams(("parallel", "arbitrary")),
        name="in_proj",
    )(h, gain, mods4, mods4, w)


def _qkprep_kernel(p_ref, gain_ref, cos_ref, sin_ref, e_ref, o_ref, *, qk_w):
    x = p_ref[...]
    width = x.shape[1]
    ss = _group_sum(x * x, e_ref)
    y = x * lax.rsqrt(ss * (1.0 / HEAD_DIM) + NORM_EPS) * gain_ref[0]
    cos = cos_ref[...]
    sin = sin_ref[...]
    lane = lax.broadcasted_iota(jnp.int32, (x.shape[0], LANES), 1)
    first_half = (lane & (HEAD_DIM // 2)) == 0
    for g in range(width // LANES):
        sl = slice(g * LANES, (g + 1) * LANES)
        if g * LANES < qk_w:
            yg = y[:, sl]
            partner = jnp.where(first_half, pltpu.roll(yg, LANES - HEAD_DIM // 2, 1),
                                pltpu.roll(yg, HEAD_DIM // 2, 1))
            o_ref[0, :, sl] = (yg * cos + partner * sin).astype(BF16)
        else:
            o_ref[0, :, sl] = x[:, sl].astype(BF16)


def _qkprep(p, gains, cos_t, sin_t, e_mat, cfg):
    rows = p.shape[0]
    tm, aw = cfg["tm"], cfg["attn_w"]
    first_blk = cfg["attn_col0"] // aw
    rope_blk = cfg["rope_blk"]
    return pl.pallas_call(
        functools.partial(_qkprep_kernel, qk_w=cfg["qk_w"]),
        grid=(rows // tm, 2),
        in_specs=[
            pl.BlockSpec((tm, aw), lambda i, s: (i, first_blk + s)),
            pl.BlockSpec((1, 1, aw), lambda i, s: (s, 0, 0)),
            pl.BlockSpec((tm, LANES), lambda i, s: (rope_blk(i), 0)),
            pl.BlockSpec((tm, LANES), lambda i, s: (rope_blk(i), 0)),
            pl.BlockSpec(e_mat.shape, lambda i, s: (0, 0)),
        ],
        out_specs=pl.BlockSpec((1, tm, aw), lambda i, s: (s, i, 0)),
        out_shape=jax.ShapeDtypeStruct((2, rows, aw), BF16),
        compiler_params=_cparams(("parallel", "parallel")),
        name="qk_prep",
    )(p, gains, cos_t, sin_t, e_mat)


def _attn_kernel(sink_ref, q_ref, kc_ref, vc_ref, kl_ref, vl_ref, o_ref, *,
                 window, n_kv, group, tq, tk, seq):
    i = pl.program_id(1)
    n_lat_tiles = seq // tq
    is_lat = i < n_lat_tiles
    rows = group * tq
    if window:
        span = tq + 2 * WINDOW
        start = jnp.clip(i * tq - WINDOW, 0, seq - span)
        start = pl.multiple_of(start, WINDOW)
        qpos = i * tq + (lax.broadcasted_iota(jnp.int32, (rows, span), 0) & (tq - 1))
        kpos = start + lax.broadcasted_iota(jnp.int32, (rows, span), 1)
        valid = jnp.abs(qpos - kpos) <= WINDOW
        n_iter = jnp.where(is_lat, 1, 0)
    else:
        span = tk
        n_iter = jnp.where(is_lat, seq // tk, 0)

    for j in range(n_kv):
        ksl = slice(j * HEAD_DIM, (j + 1) * HEAD_DIM)
        q = jnp.concatenate(
            [q_ref[0, :, (j * group + g) * HEAD_DIM:(j * group + g + 1) * HEAD_DIM] for g in range(group)],
            axis=0)
        if sink_ref is not None:
            m0 = jnp.concatenate(
                [jnp.full((tq, 1), sink_ref[j * group + g], F32) for g in range(group)], axis=0)
            l0 = jnp.ones((rows, 1), F32)
        else:
            m0 = jnp.full((rows, 1), NEG_INF, F32)
            l0 = jnp.zeros((rows, 1), F32)

        s = _dot(q, kc_ref[0, :, ksl], trans_b=True)
        m = jnp.maximum(m0, jnp.max(s, axis=-1, keepdims=True))
        p = jnp.exp(s - m)
        l = l0 * jnp.exp(m0 - m) + jnp.sum(p, axis=-1, keepdims=True)
        acc = _dot(p.astype(BF16), vc_ref[0, :, ksl])

        def body(kb, carry, q=q, ksl=ksl):
            m, l, acc = carry
            if window:
                off = start
            else:
                off = pl.multiple_of(kb * tk, tk)
            s = _dot(q, kl_ref[0, pl.ds(off, span), ksl], trans_b=True)
            if window:
                s = jnp.where(valid, s, NEG_INF)
            m_new = jnp.maximum(m, jnp.max(s, axis=-1, keepdims=True))
            alpha = jnp.exp(m - m_new)
            p = jnp.exp(s - m_new)
            l_new = l * alpha + jnp.sum(p, axis=-1, keepdims=True)
            acc_new = acc * alpha + _dot(p.astype(BF16), vl_ref[0, pl.ds(off, span), ksl])
            return m_new, l_new, acc_new

        m, l, acc = lax.fori_loop(0, n_iter, body, (m, l, acc))
        o = acc / l
        for g in range(group):
            c0 = (j * group + g) * HEAD_DIM
            o_ref[:, c0:c0 + HEAD_DIM] = o[g * tq:(g + 1) * tq].astype(BF16)


def _attention(qkv, sec, sink, cfg, window):
    rows = qkv.shape[1]
    b, seq, ctx, tq = cfg["batch"], cfg["seq"], cfg["ctx"], cfg["tq"]
    n_heads, n_kv = cfg["attn_heads"], cfg["attn_kv"]
    q_w = n_heads * HEAD_DIM
    kv_w = n_kv * HEAD_DIM
    assert kv_w == LANES and q_w % kv_w == 0 and ctx == tq
    n_lat_tiles = seq // tq
    k_blk = q_w // kv_w
    ctx_blk0 = b * seq // ctx

    def q_map(bi, i, *_):
        return (sec, jnp.where(i < n_lat_tiles, bi * n_lat_tiles + i, ctx_blk0 + bi), 0)

    def o_map(bi, i, *_):
        return (jnp.where(i < n_lat_tiles, bi * n_lat_tiles + i, ctx_blk0 + bi), 0)

    kernel = functools.partial(_attn_kernel, window=window, n_kv=n_kv, group=n_heads // n_kv,
                               tq=tq, tk=cfg["tk"], seq=seq)
    in_specs = [
        pl.BlockSpec((1, tq, q_w), q_map),
        pl.BlockSpec((1, ctx, kv_w), lambda bi, i, *_: (sec, ctx_blk0 + bi, k_blk)),
        pl.BlockSpec((1, ctx, kv_w), lambda bi, i, *_: (sec, ctx_blk0 + bi, k_blk + 1)),
        pl.BlockSpec((1, seq, kv_w), lambda bi, i, *_: (sec, bi, k_blk)),
        pl.BlockSpec((1, seq, kv_w), lambda bi, i, *_: (sec, bi, k_blk + 1)),
    ]
    args = [qkv, qkv, qkv, qkv, qkv]
    if sink is not None:
        in_specs = [pl.BlockSpec(memory_space=pltpu.SMEM)] + in_specs
        args = [sink] + args
    else:
        kernel = functools.partial(kernel, None)
    return pl.pallas_call(
        kernel,
        grid=(b, n_lat_tiles + 1),
        in_specs=in_specs,
        out_specs=pl.BlockSpec((tq, q_w), o_map),
        out_shape=jax.ShapeDtypeStruct((rows, q_w), BF16),
        compiler_params=_cparams(("parallel", "arbitrary")),
        name="window_attn" if window else "global_attn",
    )(*args)


def _softplus(z):
    return jnp.maximum(z, 0.0) + jnp.log(1.0 + jnp.exp(-jnp.abs(z)))


def _rwkv_prep_kernel(cur_ref, prev_ref, next_ref, mu_ref, w0_ref, wup_ref, a0_ref, aup_ref, gup_ref,
                      kk_ref, ka_ref, rk_ref, e_ref,
                      sh_ref, sf_ref, sb_ref, bonus_ref, gate_ref, *, bw, seg_lat, seg_ctx, n_lat_tiles):
    i = pl.program_id(0)
    p = cur_ref[...]
    tr = p.shape[0]
    lat = i < n_lat_tiles
    seg = jnp.where(lat, seg_lat, seg_ctx)
    pos = jnp.where(lat, i, i - n_lat_tiles) % seg
    has_prev = (pos != 0).astype(F32)
    has_next = (pos != seg - 1).astype(F32)
    row = lax.broadcasted_iota(jnp.int32, (tr, 1), 0)
    prev_row = prev_ref[7:8, :] * has_prev
    next_row = next_ref[0:1, :] * has_next
    prv = jnp.where(row == 0, prev_row, pltpu.roll(p, 1, 0))
    nxt = jnp.where(row == tr - 1, next_row, pltpu.roll(p, tr - 1, 0))
    xs = p + mu_ref[0:1, :] * (prv - p) + mu_ref[1:2, :] * (nxt - p)

    r = xs[:, 0:bw]
    k = xs[:, bw:2 * bw]
    v = xs[:, 2 * bw:3 * bw]
    lora = xs[:, 3 * bw:]
    n_hp = bw // LANES

    kk = k * kk_ref[...]
    kk = kk / jnp.maximum(jnp.sqrt(_group_sum(kk * kk, e_ref)), 1e-12)
    ksum = jnp.zeros_like(k)
    for d, s_ref in enumerate((sf_ref, sb_ref)):
        wd = lora[:, d * LORA_PAD:(d + 1) * LORA_PAD]
        ad = lora[:, (2 + d) * LORA_PAD:(3 + d) * LORA_PAD]
        w_log = -_softplus(-(w0_ref[d:d + 1, :] + _dot3(jnp.tanh(wd), wup_ref[d]))) - 0.5
        log_decay = -jnp.exp(w_log)
        a = jax.nn.sigmoid(a0_ref[d:d + 1, :] + _dot3(ad, aup_ref[d]))
        key = k * (1.0 + (a - 1.0) * ka_ref[...])
        ksum = ksum + key
        kka = kk * a
        for hp in range(n_hp):
            sl = slice(hp * LANES, (hp + 1) * LANES)
            s_ref[0, hp] = log_decay[:, sl]
            s_ref[1, hp] = key[:, sl]
            s_ref[2, hp] = kka[:, sl]
    for hp in range(n_hp):
        sl = slice(hp * LANES, (hp + 1) * LANES)
        sh_ref[0, hp] = r[:, sl]
        sh_ref[1, hp] = v[:, sl]
        sh_ref[2, hp] = kk[:, sl]
    bonus_ref[...] = _group_sum(r * ksum * rk_ref[...], e_ref) * v
    gate_ref[...] = _dot3(jax.nn.sigmoid(lora[:, 4 * LORA_PAD:]), gup_ref[...])


def _rwkv_prep(p, lw, e_mat, cfg):
    rows = p.shape[0]
    bw, tr = cfg["b_width"], cfg["tr"]
    cw = 3 * bw + cfg["lora_w"]
    n_hp = bw // LANES
    n_lat_tiles = cfg["batch"] * cfg["seq"] // tr
    hb = tr // 8
    last8 = rows // 8 - 1
    full = lambda a: pl.BlockSpec(a.shape, lambda i: (0,) * a.ndim)
    consts = [lw["mu"], lw["w0"], lw["w_up"], lw["a0"], lw["a_up"], lw["g_up"], lw["k_k"], lw["k_a"], lw["r_k"],
              e_mat]
    stream_spec = pl.BlockSpec((3, n_hp, tr, LANES), lambda i: (0, 0, i, 0))
    stream_shape = jax.ShapeDtypeStruct((3, n_hp, rows, LANES), F32)
    kernel = functools.partial(_rwkv_prep_kernel, bw=bw, seg_lat=cfg["seq"] // tr, seg_ctx=cfg["ctx"] // tr,
                               n_lat_tiles=n_lat_tiles)
    return pl.pallas_call(
        kernel,
        grid=(rows // tr,),
        in_specs=[
            pl.BlockSpec((tr, cw), lambda i: (i, 0)),
            pl.BlockSpec((8, cw), lambda i: (jnp.maximum(i * hb - 1, 0), 0)),
            pl.BlockSpec((8, cw), lambda i: (jnp.minimum((i + 1) * hb, last8), 0)),
        ] + [full(a) for a in consts],
        out_specs=[stream_spec, stream_spec, stream_spec,
                   pl.BlockSpec((tr, bw), lambda i: (i, 0)),
                   pl.BlockSpec((tr, bw), lambda i: (i, 0))],
        out_shape=[stream_shape, stream_shape, stream_shape,
                   jax.ShapeDtypeStruct((rows, bw), F32),
                   jax.ShapeDtypeStruct((rows, bw), F32)],
        compiler_params=_cparams(("parallel",)),
        name="rwkv_prep",
    )(p, p, p, *consts)


def _scan_chunk(r, v, kk, lw, k, b, z_ref, z_idx, y_ref, y_idx, upper):
    c = r.shape[0]
    hd = HEAD_DIM
    rid = lax.broadcasted_iota(jnp.int32, (c, c), 0)
    cid = lax.broadcasted_iota(jnp.int32, (c, c), 1)
    if upper:
        incl, strict = cid >= rid, cid > rid
    else:
        incl, strict = cid <= rid, cid < rid
    eye = cid == rid
    tri = jnp.where(incl, 1.0, 0.0).astype(BF16)

    h1 = lw.astype(BF16)
    r1 = lw - h1.astype(F32)
    h2 = r1.astype(BF16)
    h3 = (r1 - h2.astype(F32)).astype(BF16)
    cum = _dot(tri, h1) + (_dot(tri, h2) + _dot(tri, h3))
    tot = cum[0:1] if upper else cum[c - 1:c]
    e_ex = jnp.exp(cum - lw)
    e_inv = jnp.exp(-cum)
    e_cum = jnp.exp(cum)
    e_rel = jnp.exp(tot - cum)
    e_tot = jnp.exp(tot)
    qk = kk * e_ex
    rt = r * e_cum
    bt = b * e_inv
    kt = k * e_inv
    bh = b * e_rel
    kh = k * e_rel

    ys = []
    for h in range(LANES // hd):
        sl = slice(h * hd, (h + 1) * hd)
        qk_h, rt_h, v_h = qk[:, sl], rt[:, sl], v[:, sl]
        p1 = _dot1(jnp.concatenate([qk_h, rt_h], axis=0), jnp.concatenate([bt[:, sl], kt[:, sl]], axis=0),
                   trans_b=True)
        a_ab = jnp.where(strict, p1[:c, :c], 0.0)
        a_ak = jnp.where(strict, p1[:c, c:], 0.0)
        b_rb = jnp.where(incl, p1[c:, :c], 0.0)
        b_rk = jnp.where(incl, p1[c:, c:], 0.0)
        x = -a_ab
        t = jnp.where(eye, 1.0, 0.0) + x
        sq = 1
        while 2 * sq < c:
            x = _dot3(x, x)
            t = t + _dot3(t, x)
            sq *= 2
        av = _dot1(a_ak, v_h)
        tu = _dot3(t, jnp.concatenate([qk_h, av], axis=1))
        bu = _dot1(b_rb, tu)
        rh = rt_h - bu[:, :hd]
        yh = _dot1(b_rk, v_h) - bu[:, hd:]
        z = z_ref[z_idx + h]
        ys.append(yh + _dot3(rh, z))
        btu = _dot1(bh[:, sl], tu, trans_a=True)
        g = jnp.where(eye, jnp.broadcast_to(e_tot[:, sl], (hd, hd)), 0.0) - btu[:, :hd]
        hc = _dot1(kh[:, sl], v_h, trans_a=True) - btu[:, hd:]
        z_ref[z_idx + h] = _dot3(g, z) + hc
    y_ref[y_idx] = jnp.concatenate(ys, axis=1)


def _rwkv_scan_kernel(shf_ref, sf_ref, shb_ref, sb_ref, yf_ref, yb_ref, z_ref, *, n_hp):
    @pl.when(pl.program_id(1) == 0)
    def _():
        z_ref[...] = jnp.zeros_like(z_ref)

    heads_per = LANES // HEAD_DIM

    def body(hp, carry):
        _scan_chunk(shf_ref[0, hp], shf_ref[1, hp], shf_ref[2, hp], sf_ref[0, hp], sf_ref[1, hp], sf_ref[2, hp],
                    z_ref, hp * heads_per, yf_ref, hp, upper=False)
        _scan_chunk(shb_ref[0, hp], shb_ref[1, hp], shb_ref[2, hp], sb_ref[0, hp], sb_ref[1, hp], sb_ref[2, hp],
                    z_ref, (n_hp + hp) * heads_per, yb_ref, hp, upper=True)
        return carry

    lax.fori_loop(0, n_hp, body, 0)


def _rwkv_scan(s_sh, s_f, s_b, cfg):
    _, n_hp, rows, _ = s_sh.shape
    b, seq, ctx = cfg["batch"], cfg["seq"], cfg["ctx"]
    c = CHUNK
    ncc, ncl = ctx // c, seq // c
    ctx0 = b * seq // c

    def fwd_blk(bi, j):
        return jnp.where(j < ncc, ctx0 + bi * ncc + j, bi * ncl + (j - ncc))

    def bwd_blk(bi, j):
        return jnp.where(j < ncc, ctx0 + bi * ncc + (ncc - 1 - j), bi * ncl + (ncl - 1 - (j - ncc)))

    in_f = pl.BlockSpec((3, n_hp, c, LANES), lambda bi, j: (0, 0, fwd_blk(bi, j), 0))
    in_b = pl.BlockSpec((3, n_hp, c, LANES), lambda bi, j: (0, 0, bwd_blk(bi, j), 0))
    out_f = pl.BlockSpec((n_hp, c, LANES), lambda bi, j: (0, fwd_blk(bi, j), 0))
    out_b = pl.BlockSpec((n_hp, c, LANES), lambda bi, j: (0, bwd_blk(bi, j), 0))
    y_shape = jax.ShapeDtypeStruct((n_hp, rows, LANES), F32)
    return pl.pallas_call(
        functools.partial(_rwkv_scan_kernel, n_hp=n_hp),
        grid=(b, ncc + ncl),
        in_specs=[in_f, in_f, in_b, in_b],
        out_specs=[out_f, out_b],
        out_shape=[y_shape, y_shape],
        scratch_shapes=[pltpu.VMEM((2 * n_hp * (LANES // HEAD_DIM), HEAD_DIM, HEAD_DIM), F32)],
        compiler_params=_cparams(("parallel", "arbitrary")),
        name="rwkv_scan",
    )(s_sh, s_f, s_sh, s_b)


def _readout_kernel(yf_ref, yb_ref, bonus_ref, gate_ref, gg_ref, gb_ref, e_ref, o_ref):
    n_hp = yf_ref.shape[0]
    y = jnp.concatenate([yf_ref[hp] + yb_ref[hp] for hp in range(n_hp)], axis=1)
    mu = _group_sum(y, e_ref) * (1.0 / HEAD_DIM)
    yc = y - mu
    var = _group_sum(yc * yc, e_ref) * (1.0 / HEAD_DIM)
    yn = yc * lax.rsqrt(var + GN_EPS) * gg_ref[...] + gb_ref[...]
    o_ref[...] = ((yn + bonus_ref[...]) * gate_ref[...]).astype(BF16)


def _readout(y_f, y_b, bonus, gate, gn_g, gn_b, e_mat, cfg):
    n_hp, rows, _ = y_f.shape
    bw, tm = cfg["b_width"], cfg["tm"]
    y_spec = pl.BlockSpec((n_hp, tm, LANES), lambda i: (0, i, 0))
    row_spec = pl.BlockSpec((tm, bw), lambda i: (i, 0))
    vec_spec = pl.BlockSpec((1, bw), lambda i: (0, 0))
    return pl.pallas_call(
        _readout_kernel,
        grid=(rows // tm,),
        in_specs=[y_spec, y_spec, row_spec, row_spec, vec_spec, vec_spec,
                  pl.BlockSpec(e_mat.shape, lambda i: (0, 0))],
        out_specs=row_spec,
        out_shape=jax.ShapeDtypeStruct((rows, bw), BF16),
        compiler_params=_cparams(("parallel",)),
        name="rwkv_readout",
    )(y_f, y_b, bonus, gate, gn_g, gn_b, e_mat)


def _outproj_kernel(oa_ref, ob_ref, oc_ref, wa_ref, wb_ref, wc_ref, h_ref, g_ref, o_ref):
    mix = _dot(oa_ref[...], wa_ref[...]) + _dot(ob_ref[...], wb_ref[...]) + _dot(oc_ref[...], wc_ref[...])
    o_ref[...] = h_ref[...] + g_ref[...] * mix


def _outproj(o_a, o_b, o_c, w_a, w_b, w_c, h, mods4, cfg):
    rows, d = h.shape
    tm, tn = cfg["tm"], 1024
    modrow = cfg["modrow"]
    act = lambda a: pl.BlockSpec((tm, a.shape[1]), lambda i, j: (i, 0))
    wgt = lambda w: pl.BlockSpec((w.shape[0], tn), lambda i, j: (0, j))
    return pl.pallas_call(
        _outproj_kernel,
        grid=(rows // tm, d // tn),
        in_specs=[act(o_a), act(o_b), act(o_c), wgt(w_a), wgt(w_b), wgt(w_c),
                  pl.BlockSpec((tm, tn), lambda i, j: (i, j)),
                  pl.BlockSpec((None, None, 1, tn), lambda i, j: (modrow(i), 2, 0, j))],
        out_specs=pl.BlockSpec((tm, tn), lambda i, j: (i, j)),
        out_shape=jax.ShapeDtypeStruct((rows, d), F32),
        compiler_params=_cparams(("parallel", "arbitrary")),
        name="out_proj",
    )(o_a, o_b, o_c, w_a, w_b, w_c, h, mods4)


def _mlp_kernel(h_ref, g_ref, sc_ref, sh_ref, gate_ref, w1_ref, w2_ref, o_ref, u_scr, acc_scr):
    f = pl.program_id(1)

    @pl.when(f == 0)
    def _():
        u_scr[...] = _modulated_norm(h_ref[...], g_ref[...], sc_ref[...], sh_ref[...]).astype(BF16)

    a = jnp.maximum(_dot(u_scr[...], w1_ref[...]), 0.0)
    part = _dot((a * a).astype(BF16), w2_ref[...])

    @pl.when(f == 0)
    def _():
        acc_scr[...] = part

    @pl.when(f > 0)
    def _():
        acc_scr[...] += part

    @pl.when(f == pl.num_programs(1) - 1)
    def _():
        o_ref[...] = h_ref[...] + gate_ref[...] * acc_scr[...]


def _mlp(h, gain, mods4, w1, w2, cfg):
    rows, d = h.shape
    dff = w1.shape[1]
    tm, tf = cfg["tm"], 1024
    modrow = cfg["modrow"]
    mod = lambda k: pl.BlockSpec((None, None, 1, d), lambda i, f: (modrow(i), k, 0, 0))
    return pl.pallas_call(
        _mlp_kernel,
        grid=(rows // tm, dff // tf),
        in_specs=[
            pl.BlockSpec((tm, d), lambda i, f: (i, 0)),
            pl.BlockSpec((1, d), lambda i, f: (0, 0)),
            mod(4), mod(3), mod(5),
            pl.BlockSpec((d, tf), lambda i, f: (0, f)),
            pl.BlockSpec((tf, d), lambda i, f: (f, 0)),
        ],
        out_specs=pl.BlockSpec((tm, d), lambda i, f: (i, 0)),
        out_shape=jax.ShapeDtypeStruct((rows, d), F32),
        scratch_shapes=[pltpu.VMEM((tm, d), BF16), pltpu.VMEM((tm, d), F32)],
        compiler_params=_cparams(("parallel", "arbitrary")),
        name="mlp",
    )(h, gain, mods4, mods4, mods4, w1, w2)


def _rope_tables(seq, tm):
    rows = seq // GRID_W
    row = jnp.broadcast_to(jnp.arange(rows)[:, None], (rows, GRID_W)).reshape(-1)
    col = jnp.broadcast_to(jnp.arange(GRID_W)[None, :], (rows, GRID_W)).reshape(-1)
    n_freq = HEAD_DIM // 4
    inv = ROPE_THETA ** (-jnp.arange(n_freq, dtype=F32) / n_freq)
    ang = jnp.concatenate([row[:, None].astype(F32) * inv, col[:, None].astype(F32) * inv], -1)
    cos, sin = jnp.cos(ang), jnp.sin(ang)
    reps = LANES // HEAD_DIM
    cos_t = jnp.tile(jnp.concatenate([cos, cos], -1), (1, reps))
    sin_t = jnp.tile(jnp.concatenate([-sin, sin], -1), (1, reps))
    cos_t = jnp.concatenate([cos_t, jnp.ones((tm, LANES), F32)], 0)
    sin_t = jnp.concatenate([sin_t, jnp.zeros((tm, LANES), F32)], 0)
    return cos_t, sin_t


def _block_ones(width):
    g = jnp.arange(width) // HEAD_DIM
    return (g[:, None] == g[None, :]).astype(BF16)


def kernel(x, c, ctx, c_ctx, ada_w, ada_b, norm1_g, norm2_g, w_in, a_q_norm, a_k_norm, a_sink, c_q_norm,
           c_k_norm, shift_mu, decay_w0, decay_up, iclr_a0, iclr_up, gate_up, k_k, k_a, r_k, gn_g, gn_b,
           w_out, mlp_w1, mlp_w2):
    batch, seq, d = x.shape
    n_ctx = ctx.shape[1]
    depth = ada_w.shape[0]
    bw = k_k.shape[1]
    lora_d, lora_i, lora_g = decay_up.shape[2], iclr_up.shape[2], gate_up.shape[1]
    a_heads = a_sink.shape[1]
    a_kv = a_heads // 4
    q_w, kv_w = a_heads * HEAD_DIM, a_kv * HEAD_DIM
    attn_w = q_w + 2 * kv_w
    lora_w = 4 * LORA_PAD + lora_g
    b_in = 3 * bw + 2 * lora_d + 2 * lora_i + lora_g
    assert lora_d <= LORA_PAD and lora_i <= LORA_PAD
    assert w_in.shape[2] == 2 * attn_w + b_in

    tm = batch * n_ctx
    n_lat_tiles = batch * seq // tm
    tiles_per_batch = seq // tm
    cfg = dict(
        batch=batch, seq=seq, ctx=n_ctx, tm=tm, tr=n_ctx, tq=n_ctx, tk=512,
        b_width=bw, lora_w=lora_w, attn_w=attn_w, attn_col0=3 * bw + lora_w, qk_w=q_w + kv_w,
        attn_heads=a_heads, attn_kv=a_kv,
        modrow=lambda i: jnp.where(i < n_lat_tiles, i // tiles_per_batch, batch),
        rope_blk=lambda i: jnp.where(i < n_lat_tiles, i % tiles_per_batch, tiles_per_batch),
    )
    assert seq % tm == 0 and seq % GRID_W == 0 and n_ctx % CHUNK == 0 and (3 * bw) % lora_w == 0
    assert cfg["attn_col0"] % attn_w == 0 and batch + 1 <= 8

    def relayout_cols(m):
        a_part, b_part, c_part = m[..., :attn_w], m[..., attn_w:attn_w + b_in], m[..., attn_w + b_in:]
        pad = lambda z, n: jnp.pad(z, [(0, 0)] * (z.ndim - 1) + [(0, n - z.shape[-1])])
        o = 3 * bw
        pieces = [b_part[..., :o]]
        for width in (lora_d, lora_d, lora_i, lora_i):
            pieces.append(pad(b_part[..., o:o + width], LORA_PAD))
            o += width
        pieces.append(b_part[..., o:])
        return jnp.concatenate(pieces + [a_part, c_part], -1)

    w_in_r = relayout_cols(w_in).astype(BF16)
    mu_r = relayout_cols(jnp.pad(shift_mu, ((0, 0), (0, 0), (attn_w, attn_w))))[..., :3 * bw + lora_w]
    pad_rows = lambda z: jnp.pad(z, ((0, 0), (0, 0), (0, LORA_PAD - z.shape[2]), (0, 0)))
    w_up_r, a_up_r = pad_rows(decay_up), pad_rows(iclr_up)
    w_out_b = w_out.astype(BF16)
    w1_b, w2_b = mlp_w1.astype(BF16), mlp_w2.astype(BF16)

    scale = HEAD_DIM ** -0.5
    tile = lambda g, n: jnp.tile(g, (1, n))

    def gains(qg, kg):
        return jnp.concatenate([tile(qg, a_heads) * scale, tile(kg, a_kv), jnp.ones((depth, kv_w), F32)], -1)

    qk_gains = jnp.stack([gains(a_q_norm, a_k_norm), gains(c_q_norm, c_k_norm)], 1)[:, :, None, :]

    cos_t, sin_t = _rope_tables(seq, tm)
    e_attn = _block_ones(attn_w)
    e_b = _block_ones(2 * LANES)

    cvec = jnp.zeros((8, d), F32).at[:batch].set(c).at[batch].set(c_ctx)
    mods = _mods(cvec, ada_w, ada_b).reshape(depth, 8, 6, 1, d)

    h = jnp.concatenate([x.reshape(batch * seq, d), ctx.reshape(batch * n_ctx, d)], 0)
    for l in range(depth):
        m4 = mods[l]
        p = _inproj(h, norm1_g[l][None], m4, w_in_r[l], cfg)
        qkv = _qkprep(p, qk_gains[l], cos_t, sin_t, e_attn, cfg)
        o_a = _attention(qkv, 0, a_sink[l], cfg, window=True)
        o_c = _attention(qkv, 1, None, cfg, window=False)
        lw = dict(mu=mu_r[l], w0=decay_w0[l], w_up=w_up_r[l], a0=iclr_a0[l], a_up=a_up_r[l], g_up=gate_up[l],
                  k_k=k_k[l][None], k_a=k_a[l][None], r_k=r_k[l][None])
        s_sh, s_f, s_b, bonus, gate = _rwkv_prep(p, lw, e_b, cfg)
        y_f, y_b = _rwkv_scan(s_sh, s_f, s_b, cfg)
        o_b = _readout(y_f, y_b, bonus, gate, gn_g[l][None], gn_b[l][None], e_b, cfg)
        h = _outproj(o_a, o_b, o_c, w_out_b[l, :q_w], w_out_b[l, q_w:q_w + bw], w_out_b[l, q_w + bw:],
                     h, m4, cfg)
        h = _mlp(h, norm2_g[l][None], m4, w1_b[l], w2_b[l], cfg)
    return h[:batch * seq].reshape(batch, seq, d)
```

```python
import functools

import jax
import jax.numpy as jnp
from jax import lax
from jax.experimental import pallas as pl
from jax.experimental.pallas import tpu as pltpu

F32 = jnp.float32
BF16 = jnp.bfloat16

HEAD_DIM = 64
GRID_W = 64
WINDOW = 128
ROPE_THETA = 10000.0
NORM_EPS = 1e-6
GN_EPS = 64e-5
NEG_INF = -1e30
LANES = 128
LORA_PAD = 128
CHUNK = 64
VMEM_LIMIT = 52 * 1024 * 1024


def _dot(a, b, trans_a=False, trans_b=False):
    dn = (((0 if trans_a else 1,), (1 if trans_b else 0,)), ((), ()))
    return lax.dot_general(a, b, dn, preferred_element_type=F32)


def _split(x):
    hi = x.astype(BF16)
    lo = (x - hi.astype(F32)).astype(BF16)
    return hi, lo


def _dot3(a, b, trans_a=False, trans_b=False):
    ah, al = _split(a)
    bh, bl = _split(b)
    kw = dict(trans_a=trans_a, trans_b=trans_b)
    return _dot(ah, bh, **kw) + (_dot(al, bh, **kw) + _dot(ah, bl, **kw))


def _dot1(a, b, trans_a=False, trans_b=False):
    return _dot(a.astype(BF16), b.astype(BF16), trans_a=trans_a, trans_b=trans_b)


def _group_sum(x, e_ref):
    gw = e_ref.shape[0]
    e = e_ref[...]
    outs = []
    for g in range(x.shape[1] // gw):
        hi, lo = _split(x[:, g * gw:(g + 1) * gw])
        outs.append(_dot(hi, e) + _dot(lo, e))
    return outs[0] if len(outs) == 1 else jnp.concatenate(outs, axis=1)


def _cparams(sem):
    return pltpu.CompilerParams(dimension_semantics=sem, vmem_limit_bytes=VMEM_LIMIT)


def _mods_kernel(c_ref, w_ref, b_ref, o_ref):
    c = c_ref[...]
    s = c * jax.nn.sigmoid(c)
    o_ref[0] = _dot3(s, w_ref[0]) + b_ref[0]


def _mods(cvec, ada_w, ada_b):
    depth, d, n = ada_w.shape
    tn = 512
    return pl.pallas_call(
        _mods_kernel,
        grid=(depth, n // tn),
        in_specs=[
            pl.BlockSpec((8, d), lambda l, j: (0, 0)),
            pl.BlockSpec((1, d, tn), lambda l, j: (l, 0, j)),
            pl.BlockSpec((1, 1, tn), lambda l, j: (l, 0, j)),
        ],
        out_specs=pl.BlockSpec((1, 8, tn), lambda l, j: (l, 0, j)),
        out_shape=jax.ShapeDtypeStruct((depth, 8, n), F32),
        compiler_params=_cparams(("parallel", "parallel")),
        name="adaln_mods",
    )(cvec, ada_w, ada_b.reshape(depth, 1, n))


def _modulated_norm(x, g, sc, sh):
    ms = jnp.mean(x * x, axis=-1, keepdims=True)
    return (x * lax.rsqrt(ms + NORM_EPS) * g) * (1.0 + sc) + sh


def _inproj_kernel(h_ref, g_ref, sc_ref, sh_ref, w_ref, o_ref, u_scr):
    @pl.when(pl.program_id(1) == 0)
    def _():
        u_scr[...] = _modulated_norm(h_ref[...], g_ref[...], sc_ref[...], sh_ref[...]).astype(BF16)

    o_ref[...] = _dot(u_scr[...], w_ref[...])


def _inproj(h, gain, mods, w, l, cfg):
    rows, d = h.shape
    n = w.shape[2]
    tm, tn = cfg["tm"], 768
    modrow = cfg["modrow"]
    return pl.pallas_call(
        _inproj_kernel,
        grid=(rows // tm, n // tn),
        in_specs=[
            pl.BlockSpec((tm, d), lambda i, j: (i, 0)),
            pl.BlockSpec((None, 1, d), lambda i, j: (l, 0, 0)),
            pl.BlockSpec((None, None, None, 1, d), lambda i, j: (l, modrow(i), 1, 0, 0)),
            pl.BlockSpec((None, None, None, 1, d), lambda i, j: (l, modrow(i), 0, 0, 0)),
            pl.BlockSpec((None, d, tn), lambda i, j: (l, 0, j)),
        ],
        out_specs=pl.BlockSpec((tm, tn), lambda i, j: (i, j)),
        out_shape=jax.ShapeDtypeStruct((rows, n), F32),
        scratch_shapes=[pltpu.VMEM((tm, d), BF16)],
        compiler_params=_cparams(("parallel", "arbitrary")),
        name="in_proj",
    )(h, gain, mods, mods, w)


def _qkprep_kernel(p_ref, gain_ref, cos_ref, sin_ref, e_ref, o_ref, *, qk_w):
    x = p_ref[...]
    width = x.shape[1]
    ss = _group_sum(x * x, e_ref)
    y = x * lax.rsqrt(ss * (1.0 / HEAD_DIM) + NORM_EPS) * gain_ref[0]
    cos = cos_ref[...]
    sin = sin_ref[...]
    lane = lax.broadcasted_iota(jnp.int32, (x.shape[0], LANES), 1)
    first_half = (lane & (HEAD_DIM // 2)) == 0
    for g in range(width // LANES):
        sl = slice(g * LANES, (g + 1) * LANES)
        if g * LANES < qk_w:
            yg = y[:, sl]
            partner = jnp.where(first_half, pltpu.roll(yg, LANES - HEAD_DIM // 2, 1),
                                pltpu.roll(yg, HEAD_DIM // 2, 1))
            o_ref[0, :, sl] = (yg * cos + partner * sin).astype(BF16)
        else:
            o_ref[0, :, sl] = x[:, sl].astype(BF16)


def _qkprep(p, gains, cos_t, sin_t, e_mat, cfg):
    rows = p.shape[0]
    tm, aw = cfg["tm"], cfg["attn_w"]
    first_blk = cfg["attn_col0"] // aw
    rope_blk = cfg["rope_blk"]
    return pl.pallas_call(
        functools.partial(_qkprep_kernel, qk_w=cfg["qk_w"]),
        grid=(rows // tm, 2),
        in_specs=[
            pl.BlockSpec((tm, aw), lambda i, s: (i, first_blk + s)),
            pl.BlockSpec((1, 1, aw), lambda i, s: (s, 0, 0)),
            pl.BlockSpec((tm, LANES), lambda i, s: (rope_blk(i), 0)),
            pl.BlockSpec((tm, LANES), lambda i, s: (rope_blk(i), 0)),
            pl.BlockSpec(e_mat.shape, lambda i, s: (0, 0)),
        ],
        out_specs=pl.BlockSpec((1, tm, aw), lambda i, s: (s, i, 0)),
        out_shape=jax.ShapeDtypeStruct((2, rows, aw), BF16),
        compiler_params=_cparams(("parallel", "parallel")),
        name="qk_prep",
    )(p, gains, cos_t, sin_t, e_mat)


def _attn_kernel(sink_ref, q_ref, kc_ref, vc_ref, kl_ref, vl_ref, o_ref, *,
                 window, n_kv, group, tq, tk, seq):
    i = pl.program_id(1)
    n_lat_tiles = seq // tq
    is_lat = i < n_lat_tiles
    rows = group * tq
    if window:
        span = tq + 2 * WINDOW
        start = jnp.clip(i * tq - WINDOW, 0, seq - span)
        start = pl.multiple_of(start, WINDOW)
        qpos = i * tq + (lax.broadcasted_iota(jnp.int32, (rows, span), 0) & (tq - 1))
        kpos = start + lax.broadcasted_iota(jnp.int32, (rows, span), 1)
        valid = jnp.abs(qpos - kpos) <= WINDOW
        n_iter = jnp.where(is_lat, 1, 0)
    else:
        span = tk
        n_iter = jnp.where(is_lat, seq // tk, 0)

    for j in range(n_kv):
        ksl = slice(j * HEAD_DIM, (j + 1) * HEAD_DIM)
        q = jnp.concatenate(
            [q_ref[0, :, (j * group + g) * HEAD_DIM:(j * group + g + 1) * HEAD_DIM] for g in range(group)],
            axis=0)
        if sink_ref is not None:
            m0 = jnp.concatenate(
                [jnp.full((tq, 1), sink_ref[j * group + g], F32) for g in range(group)], axis=0)
            l0 = jnp.ones((rows, 1), F32)
        else:
            m0 = jnp.full((rows, 1), NEG_INF, F32)
            l0 = jnp.zeros((rows, 1), F32)

        s = _dot(q, kc_ref[0, :, ksl], trans_b=True)
        m = jnp.maximum(m0, jnp.max(s, axis=-1, keepdims=True))
        p = jnp.exp(s - m)
        l = l0 * jnp.exp(m0 - m) + jnp.sum(p, axis=-1, keepdims=True)
        acc = _dot(p.astype(BF16), vc_ref[0, :, ksl])

        def body(kb, carry, q=q, ksl=ksl):
            m, l, acc = carry
            if window:
                off = start
            else:
                off = pl.multiple_of(kb * tk, tk)
            s = _dot(q, kl_ref[0, pl.ds(off, span), ksl], trans_b=True)
            if window:
                s = jnp.where(valid, s, NEG_INF)
            m_new = jnp.maximum(m, jnp.max(s, axis=-1, keepdims=True))
            alpha = jnp.exp(m - m_new)
            p = jnp.exp(s - m_new)
            l_new = l * alpha + jnp.sum(p, axis=-1, keepdims=True)
            acc_new = acc * alpha + _dot(p.astype(BF16), vl_ref[0, pl.ds(off, span), ksl])
            return m_new, l_new, acc_new

        m, l, acc = lax.fori_loop(0, n_iter, body, (m, l, acc))
        o = acc / l
        for g in range(group):
            c0 = (j * group + g) * HEAD_DIM
            o_ref[:, c0:c0 + HEAD_DIM] = o[g * tq:(g + 1) * tq].astype(BF16)


def _attention(qkv, sec, sink, cfg, window):
    rows = qkv.shape[1]
    b, seq, ctx, tq = cfg["batch"], cfg["seq"], cfg["ctx"], cfg["tq"]
    n_heads, n_kv = cfg["attn_heads"], cfg["attn_kv"]
    q_w = n_heads * HEAD_DIM
    kv_w = n_kv * HEAD_DIM
    assert kv_w == LANES and q_w % kv_w == 0 and ctx == tq
    n_lat_tiles = seq // tq
    k_blk = q_w // kv_w
    ctx_blk0 = b * seq // ctx

    def q_map(bi, i, *_):
        return (sec, jnp.where(i < n_lat_tiles, bi * n_lat_tiles + i, ctx_blk0 + bi), 0)

    def o_map(bi, i, *_):
        return (jnp.where(i < n_lat_tiles, bi * n_lat_tiles + i, ctx_blk0 + bi), 0)

    kernel = functools.partial(_attn_kernel, window=window, n_kv=n_kv, group=n_heads // n_kv,
                               tq=tq, tk=cfg["tk"], seq=seq)
    in_specs = [
        pl.BlockSpec((1, tq, q_w), q_map),
        pl.BlockSpec((1, ctx, kv_w), lambda bi, i, *_: (sec, ctx_blk0 + bi, k_blk)),
        pl.BlockSpec((1, ctx, kv_w), lambda bi, i, *_: (sec, ctx_blk0 + bi, k_blk + 1)),
        pl.BlockSpec((1, seq, kv_w), lambda bi, i, *_: (sec, bi, k_blk)),
        pl.BlockSpec((1, seq, kv_w), lambda bi, i, *_: (sec, bi, k_blk + 1)),
    ]
    args = [qkv, qkv, qkv, qkv, qkv]
    if sink is not None:
        in_specs = [pl.BlockSpec(memory_space=pltpu.SMEM)] + in_specs
        args = [sink] + args
    else:
        kernel = functools.partial(kernel, None)
    return pl.pallas_call(
        kernel,
        grid=(b, n_lat_tiles + 1),
        in_specs=in_specs,
        out_specs=pl.BlockSpec((tq, q_w), o_map),
        out_shape=jax.ShapeDtypeStruct((rows, q_w), BF16),
        compiler_params=_cparams(("parallel", "arbitrary")),
        name="window_attn" if window else "global_attn",
    )(*args)


def _softplus(z):
    return jnp.maximum(z, 0.0) + jnp.log(1.0 + jnp.exp(-jnp.abs(z)))


def _tri3(tri, x):
    h1 = x.astype(BF16)
    r1 = x - h1.astype(F32)
    h2 = r1.astype(BF16)
    h3 = (r1 - h2.astype(F32)).astype(BF16)
    return _dot(tri, h1) + (_dot(tri, h2) + _dot(tri, h3))


def _rwkv_prep_kernel(cur_ref, prev_ref, next_ref, mu_ref, w0_ref, wup_ref, a0_ref, aup_ref, gup_ref,
                      kk_ref, ka_ref, rk_ref, e_ref, tril_ref, triu_ref, ones_ref,
                      sv_ref, sf_ref, sb_ref, etot_ref, bonus_ref, gate_ref, *,
                      bw, seg_lat, seg_ctx, n_lat_tiles):
    i = pl.program_id(0)
    p = cur_ref[...]
    tr = p.shape[0]
    lat = i < n_lat_tiles
    seg = jnp.where(lat, seg_lat, seg_ctx)
    pos = jnp.where(lat, i, i - n_lat_tiles) % seg
    has_prev = (pos != 0).astype(F32)
    has_next = (pos != seg - 1).astype(F32)
    row = lax.broadcasted_iota(jnp.int32, (tr, 1), 0)
    prev_row = prev_ref[7:8, :] * has_prev
    next_row = next_ref[0:1, :] * has_next
    prv = jnp.where(row == 0, prev_row, pltpu.roll(p, 1, 0))
    nxt = jnp.where(row == tr - 1, next_row, pltpu.roll(p, tr - 1, 0))
    xs = p + mu_ref[0:1, :] * (prv - p) + mu_ref[1:2, :] * (nxt - p)

    r = xs[:, 0:bw]
    k = xs[:, bw:2 * bw]
    v = xs[:, 2 * bw:3 * bw]
    lora = xs[:, 3 * bw:]
    n_hp = bw // LANES

    kk = k * kk_ref[...]
    kk = kk / jnp.maximum(jnp.sqrt(_group_sum(kk * kk, e_ref)), 1e-12)
    ksum = jnp.zeros_like(k)
    ones_bd = ones_ref[...]
    for d, (s_ref, tri_ref) in enumerate(((sf_ref, tril_ref), (sb_ref, triu_ref))):
        wd = lora[:, d * LORA_PAD:(d + 1) * LORA_PAD]
        ad = lora[:, (2 + d) * LORA_PAD:(3 + d) * LORA_PAD]
        w_log = -_softplus(-(w0_ref[d:d + 1, :] + _dot3(jnp.tanh(wd), wup_ref[d]))) - 0.5
        log_decay = -jnp.exp(w_log)
        a = jax.nn.sigmoid(a0_ref[d:d + 1, :] + _dot3(ad, aup_ref[d]))
        key = k * (1.0 + (a - 1.0) * ka_ref[...])
        ksum = ksum + key
        kka = kk * a
        cum = _tri3(tri_ref[...], log_decay)
        tot = _tri3(ones_bd, log_decay)
        e_inv = jnp.exp(-cum)
        e_rel = jnp.exp(tot - cum)
        streams = (kk * jnp.exp(cum - log_decay), r * jnp.exp(cum), kka * e_inv, key * e_inv,
                   kka * e_rel, key * e_rel)
        for n, st in enumerate(streams):
            st = st.astype(BF16)
            for hp in range(n_hp):
                s_ref[n, hp] = st[:, hp * LANES:(hp + 1) * LANES]
        for cidx in range(tr // CHUNK):
            e_tot = jnp.exp(tot[cidx * CHUNK:cidx * CHUNK + 1])
            etot_ref[d, cidx] = jnp.concatenate(
                [e_tot[:, hp * LANES:(hp + 1) * LANES] for hp in range(n_hp)], axis=0)
    vb = v.astype(BF16)
    for hp in range(n_hp):
        sv_ref[hp] = vb[:, hp * LANES:(hp + 1) * LANES]
    bonus_ref[...] = _group_sum(r * ksum * rk_ref[...], e_ref) * v
    gate_ref[...] = _dot3(jax.nn.sigmoid(lora[:, 4 * LORA_PAD:]), gup_ref[...])


def _chunk_block_diag(tr, kind):
    t = jnp.arange(tr)
    same = (t[:, None] // CHUNK) == (t[None, :] // CHUNK)
    if kind == "lower":
        same = same & (t[None, :] <= t[:, None])
    elif kind == "upper":
        same = same & (t[None, :] >= t[:, None])
    return same.astype(BF16)


N_STREAMS = 6


def _rwkv_prep(p, lw, e_mat, cfg):
    rows = p.shape[0]
    bw, tr = cfg["b_width"], cfg["tr"]
    cw = 3 * bw + cfg["lora_w"]
    n_hp = bw // LANES
    n_lat_tiles = cfg["batch"] * cfg["seq"] // tr
    hb = tr // 8
    last8 = rows // 8 - 1
    full = lambda a: pl.BlockSpec(a.shape, lambda i: (0,) * a.ndim)
    consts = [lw["mu"], lw["w0"], lw["w_up"], lw["a0"], lw["a_up"], lw["g_up"], lw["k_k"], lw["k_a"], lw["r_k"],
              e_mat, _chunk_block_diag(tr, "lower"), _chunk_block_diag(tr, "upper"), _chunk_block_diag(tr, "ones")]
    stream_spec = pl.BlockSpec((N_STREAMS, n_hp, tr, LANES), lambda i: (0, 0, i, 0))
    stream_shape = jax.ShapeDtypeStruct((N_STREAMS, n_hp, rows, LANES), BF16)
    cpt = tr // CHUNK
    kernel = functools.partial(_rwkv_prep_kernel, bw=bw, seg_lat=cfg["seq"] // tr, seg_ctx=cfg["ctx"] // tr,
                               n_lat_tiles=n_lat_tiles)
    return pl.pallas_call(
        kernel,
        grid=(rows // tr,),
        in_specs=[
            pl.BlockSpec((tr, cw), lambda i: (i, 0)),
            pl.BlockSpec((8, cw), lambda i: (jnp.maximum(i * hb - 1, 0), 0)),
            pl.BlockSpec((8, cw), lambda i: (jnp.minimum((i + 1) * hb, last8), 0)),
        ] + [full(a) for a in consts],
        out_specs=[pl.BlockSpec((n_hp, tr, LANES), lambda i: (0, i, 0)),
                   stream_spec, stream_spec,
                   pl.BlockSpec((2, cpt, n_hp, LANES), lambda i: (0, i, 0, 0)),
                   pl.BlockSpec((tr, bw), lambda i: (i, 0)),
                   pl.BlockSpec((tr, bw), lambda i: (i, 0))],
        out_shape=[jax.ShapeDtypeStruct((n_hp, rows, LANES), BF16),
                   stream_shape, stream_shape,
                   jax.ShapeDtypeStruct((2, rows // CHUNK, n_hp, LANES), F32),
                   jax.ShapeDtypeStruct((rows, bw), F32),
                   jax.ShapeDtypeStruct((rows, bw), F32)],
        compiler_params=_cparams(("parallel",)),
        name="rwkv_prep",
    )(p, p, p, *consts)


def _lhs3(a):
    hi, lo = _split(a)
    return jnp.concatenate([hi, lo, hi], axis=1)


def _rhs3(b):
    hi, lo = _split(b)
    return jnp.concatenate([hi, hi, lo], axis=0)


def _scan_direction(v_ref, st_ref, e_ref, y_ref, s_ref, s_base, n_hp, upper):
    c, hd = CHUNK, HEAD_DIM
    per = LANES // hd
    heads = [(hp, h) for hp in range(n_hp) for h in range(per)]
    n = len(heads)
    rid = lax.broadcasted_iota(jnp.int32, (c, c), 0)
    cid = lax.broadcasted_iota(jnp.int32, (c, c), 1)
    incl, strict = (cid >= rid, cid > rid) if upper else (cid <= rid, cid < rid)
    eye_f = jnp.where(cid == rid, 1.0, 0.0)

    def stream(k):
        return [st_ref[k, hp, :, h * hd:(h + 1) * hd] for hp, h in heads]

    qk, rt, bt, kt, bh, kh = (stream(k) for k in range(N_STREAMS))
    vv = [v_ref[hp, :, h * hd:(h + 1) * hd] for hp, h in heads]
    p1 = [_dot(jnp.concatenate([qk[i], rt[i]], axis=0), jnp.concatenate([bt[i], kt[i]], axis=0), trans_b=True)
          for i in range(n)]
    x_pow = [jnp.where(strict, -p[:c, :c], 0.0) for p in p1]
    t = [eye_f + xi for xi in x_pow]
    x_pow = [_dot(_lhs3(xi), _rhs3(xi)) for xi in x_pow]
    sq = 2
    while sq < c:
        w = [_rhs3(xi) for xi in x_pow]
        if 2 * sq < c:
            prod = [_dot(_lhs3(jnp.concatenate([t[i], x_pow[i]], axis=0)), w[i]) for i in range(n)]
            t = [t[i] + prod[i][:c] for i in range(n)]
            x_pow = [pr[c:] for pr in prod]
        else:
            t = [t[i] + _dot(_lhs3(t[i]), w[i]) for i in range(n)]
        sq *= 2
    avy = [_dot(jnp.concatenate([jnp.where(strict, p[:c, c:], 0.0), jnp.where(incl, p[c:, c:], 0.0)],
                                axis=0).astype(BF16), vv[i]) for i, p in enumerate(p1)]
    tu = []
    for i in range(n):
        rhs = jnp.concatenate([qk[i], avy[i][:c].astype(BF16)], axis=1)
        tu.append(_dot(jnp.concatenate(_split(t[i]), axis=1), jnp.concatenate([rhs, rhs], axis=0)).astype(BF16))
    bu = [_dot(jnp.where(incl, p[c:, :c], 0.0).astype(BF16), tu[i]) for i, p in enumerate(p1)]
    rh = [(rt[i].astype(F32) - bu[i][:, :hd]).astype(BF16) for i in range(n)]
    s_old = [s_ref[s_base + i] for i in range(n)]
    s_bf = [s.astype(BF16) for s in s_old]
    y = [avy[i][c:] - bu[i][:, hd:] + _dot(rh[i], s_bf[i], trans_b=True) for i in range(n)]
    for hp in range(n_hp):
        y_ref[hp] = jnp.concatenate(y[hp * per:(hp + 1) * per], axis=1)
    mt = [_dot(tu[i], bh[i], trans_a=True) for i in range(n)]
    hct = [_dot(vv[i], kh[i], trans_a=True) - mt[i][hd:] for i in range(n)]
    for i, (hp, h) in enumerate(heads):
        e_row = e_ref[hp:hp + 1, h * hd:(h + 1) * hd]
        s_ref[s_base + i] = s_old[i] * e_row + hct[i] - _dot(s_bf[i], mt[i][:hd].astype(BF16))


def _rwkv_scan_kernel(vf_ref, sf_ref, ef_ref, vb_ref, sb_ref, eb_ref, yf_ref, yb_ref, s_ref, *, n_hp):
    @pl.when(pl.program_id(1) == 0)
    def _():
        s_ref[...] = jnp.zeros_like(s_ref)

    n_heads = n_hp * (LANES // HEAD_DIM)
    _scan_direction(vf_ref, sf_ref, ef_ref, yf_ref, s_ref, 0, n_hp, upper=False)
    _scan_direction(vb_ref, sb_ref, eb_ref, yb_ref, s_ref, n_heads, n_hp, upper=True)


def _rwkv_scan(s_v, s_f, s_b, e_tot, cfg):
    n_hp, rows, _ = s_v.shape
    b, seq, ctx = cfg["batch"], cfg["seq"], cfg["ctx"]
    c = CHUNK
    ncc, ncl = ctx // c, seq // c
    ctx0 = b * seq // c

    def fwd_blk(bi, j):
        return jnp.where(j < ncc, ctx0 + bi * ncc + j, bi * ncl + (j - ncc))

    def bwd_blk(bi, j):
        return jnp.where(j < ncc, ctx0 + bi * ncc + (ncc - 1 - j), bi * ncl + (ncl - 1 - (j - ncc)))

    def specs(blk, d):
        return [pl.BlockSpec((n_hp, c, LANES), lambda bi, j: (0, blk(bi, j), 0)),
                pl.BlockSpec((N_STREAMS, n_hp, c, LANES), lambda bi, j: (0, 0, blk(bi, j), 0)),
                pl.BlockSpec((None, None, n_hp, LANES), lambda bi, j: (d, blk(bi, j), 0, 0))]

    out_f = pl.BlockSpec((n_hp, c, LANES), lambda bi, j: (0, fwd_blk(bi, j), 0))
    out_b = pl.BlockSpec((n_hp, c, LANES), lambda bi, j: (0, bwd_blk(bi, j), 0))
    y_shape = jax.ShapeDtypeStruct((n_hp, rows, LANES), F32)
    return pl.pallas_call(
        functools.partial(_rwkv_scan_kernel, n_hp=n_hp),
        grid=(b, ncc + ncl),
        in_specs=specs(fwd_blk, 0) + specs(bwd_blk, 1),
        out_specs=[out_f, out_b],
        out_shape=[y_shape, y_shape],
        scratch_shapes=[pltpu.VMEM((2 * n_hp * (LANES // HEAD_DIM), HEAD_DIM, HEAD_DIM), F32)],
        compiler_params=_cparams(("parallel", "arbitrary")),
        name="rwkv_scan",
    )(s_v, s_f, e_tot, s_v, s_b, e_tot)


def _readout_kernel(yf_ref, yb_ref, bonus_ref, gate_ref, gg_ref, gb_ref, e_ref, o_ref):
    n_hp = yf_ref.shape[0]
    y = jnp.concatenate([yf_ref[hp] + yb_ref[hp] for hp in range(n_hp)], axis=1)
    mu = _group_sum(y, e_ref) * (1.0 / HEAD_DIM)
    yc = y - mu
    var = _group_sum(yc * yc, e_ref) * (1.0 / HEAD_DIM)
    yn = yc * lax.rsqrt(var + GN_EPS) * gg_ref[...] + gb_ref[...]
    o_ref[...] = ((yn + bonus_ref[...]) * gate_ref[...]).astype(BF16)


def _readout(y_f, y_b, bonus, gate, gn_g, gn_b, e_mat, cfg):
    n_hp, rows, _ = y_f.shape
    bw, tm = cfg["b_width"], cfg["tm"]
    y_spec = pl.BlockSpec((n_hp, tm, LANES), lambda i: (0, i, 0))
    row_spec = pl.BlockSpec((tm, bw), lambda i: (i, 0))
    vec_spec = pl.BlockSpec((1, bw), lambda i: (0, 0))
    return pl.pallas_call(
        _readout_kernel,
        grid=(rows // tm,),
        in_specs=[y_spec, y_spec, row_spec, row_spec, vec_spec, vec_spec,
                  pl.BlockSpec(e_mat.shape, lambda i: (0, 0))],
        out_specs=row_spec,
        out_shape=jax.ShapeDtypeStruct((rows, bw), BF16),
        compiler_params=_cparams(("parallel",)),
        name="rwkv_readout",
    )(y_f, y_b, bonus, gate, gn_g, gn_b, e_mat)


def _outproj_kernel(oa_ref, ob_ref, oc_ref, w_ref, h_ref, g_ref, o_ref):
    mix = jnp.concatenate([oa_ref[...], ob_ref[...], oc_ref[...]], axis=1)
    o_ref[...] = h_ref[...] + g_ref[...] * _dot(mix, w_ref[...])


def _outproj(o_a, o_b, o_c, w, h, mods, l, cfg):
    rows, d = h.shape
    tm, tn = cfg["tm"], 1024
    modrow = cfg["modrow"]
    act = lambda a: pl.BlockSpec((tm, a.shape[1]), lambda i, j: (i, 0))
    return pl.pallas_call(
        _outproj_kernel,
        grid=(rows // tm, d // tn),
        in_specs=[act(o_a), act(o_b), act(o_c),
                  pl.BlockSpec((None, d, tn), lambda i, j: (l, 0, j)),
                  pl.BlockSpec((tm, tn), lambda i, j: (i, j)),
                  pl.BlockSpec((None, None, None, 1, tn), lambda i, j: (l, modrow(i), 2, 0, j))],
        out_specs=pl.BlockSpec((tm, tn), lambda i, j: (i, j)),
        out_shape=jax.ShapeDtypeStruct((rows, d), F32),
        compiler_params=_cparams(("parallel", "arbitrary")),
        name="out_proj",
    )(o_a, o_b, o_c, w, h, mods)


def _mlp_kernel(h_ref, g_ref, sc_ref, sh_ref, gate_ref, w1_ref, w2_ref, o_ref, u_scr, acc_scr):
    f = pl.program_id(1)

    @pl.when(f == 0)
    def _():
        u_scr[...] = _modulated_norm(h_ref[...], g_ref[...], sc_ref[...], sh_ref[...]).astype(BF16)

    a = jnp.maximum(_dot(u_scr[...], w1_ref[...]), 0.0)
    part = _dot((a * a).astype(BF16), w2_ref[...])

    @pl.when(f == 0)
    def _():
        acc_scr[...] = part

    @pl.when(f > 0)
    def _():
        acc_scr[...] += part

    @pl.when(f == pl.num_programs(1) - 1)
    def _():
        o_ref[...] = h_ref[...] + gate_ref[...] * acc_scr[...]


def _mlp(h, gain, mods, w1, w2, l, cfg):
    rows, d = h.shape
    dff = w1.shape[2]
    tm, tf = cfg["tm"], 1024
    modrow = cfg["modrow"]
    mod = lambda k: pl.BlockSpec((None, None, None, 1, d), lambda i, f: (l, modrow(i), k, 0, 0))
    return pl.pallas_call(
        _mlp_kernel,
        grid=(rows // tm, dff // tf),
        in_specs=[
            pl.BlockSpec((tm, d), lambda i, f: (i, 0)),
            pl.BlockSpec((None, 1, d), lambda i, f: (l, 0, 0)),
            mod(4), mod(3), mod(5),
            pl.BlockSpec((None, d, tf), lambda i, f: (l, 0, f)),
            pl.BlockSpec((None, tf, d), lambda i, f: (l, f, 0)),
        ],
        out_specs=pl.BlockSpec((tm, d), lambda i, f: (i, 0)),
        out_shape=jax.ShapeDtypeStruct((rows, d), F32),
        scratch_shapes=[pltpu.VMEM((tm, d), BF16), pltpu.VMEM((tm, d), F32)],
        compiler_params=_cparams(("parallel", "arbitrary")),
        name="mlp",
    )(h, gain, mods, mods, mods, w1, w2)


def _rope_tables(seq, tm):
    rows = seq // GRID_W
    row = jnp.broadcast_to(jnp.arange(rows)[:, None], (rows, GRID_W)).reshape(-1)
    col = jnp.broadcast_to(jnp.arange(GRID_W)[None, :], (rows, GRID_W)).reshape(-1)
    n_freq = HEAD_DIM // 4
    inv = ROPE_THETA ** (-jnp.arange(n_freq, dtype=F32) / n_freq)
    ang = jnp.concatenate([row[:, None].astype(F32) * inv, col[:, None].astype(F32) * inv], -1)
    cos, sin = jnp.cos(ang), jnp.sin(ang)
    reps = LANES // HEAD_DIM
    cos_t = jnp.tile(jnp.concatenate([cos, cos], -1), (1, reps))
    sin_t = jnp.tile(jnp.concatenate([-sin, sin], -1), (1, reps))
    cos_t = jnp.concatenate([cos_t, jnp.ones((tm, LANES), F32)], 0)
    sin_t = jnp.concatenate([sin_t, jnp.zeros((tm, LANES), F32)], 0)
    return cos_t, sin_t


def _block_ones(width):
    g = jnp.arange(width) // HEAD_DIM
    return (g[:, None] == g[None, :]).astype(BF16)


def kernel(x, c, ctx, c_ctx, ada_w, ada_b, norm1_g, norm2_g, w_in, a_q_norm, a_k_norm, a_sink, c_q_norm,
           c_k_norm, shift_mu, decay_w0, decay_up, iclr_a0, iclr_up, gate_up, k_k, k_a, r_k, gn_g, gn_b,
           w_out, mlp_w1, mlp_w2):
    batch, seq, d = x.shape
    n_ctx = ctx.shape[1]
    depth = ada_w.shape[0]
    bw = k_k.shape[1]
    lora_d, lora_i, lora_g = decay_up.shape[2], iclr_up.shape[2], gate_up.shape[1]
    a_heads = a_sink.shape[1]
    a_kv = a_heads // 4
    q_w, kv_w = a_heads * HEAD_DIM, a_kv * HEAD_DIM
    attn_w = q_w + 2 * kv_w
    lora_w = 4 * LORA_PAD + lora_g
    b_in = 3 * bw + 2 * lora_d + 2 * lora_i + lora_g
    assert lora_d <= LORA_PAD and lora_i <= LORA_PAD
    assert w_in.shape[2] == 2 * attn_w + b_in

    tm = batch * n_ctx
    n_lat_tiles = batch * seq // tm
    tiles_per_batch = seq // tm
    cfg = dict(
        batch=batch, seq=seq, ctx=n_ctx, tm=tm, tr=n_ctx, tq=n_ctx, tk=512,
        b_width=bw, lora_w=lora_w, attn_w=attn_w, attn_col0=3 * bw + lora_w, qk_w=q_w + kv_w,
        attn_heads=a_heads, attn_kv=a_kv,
        modrow=lambda i: jnp.where(i < n_lat_tiles, i // tiles_per_batch, batch),
        rope_blk=lambda i: jnp.where(i < n_lat_tiles, i % tiles_per_batch, tiles_per_batch),
    )
    assert seq % tm == 0 and seq % GRID_W == 0 and n_ctx % CHUNK == 0 and (3 * bw) % lora_w == 0
    assert cfg["attn_col0"] % attn_w == 0 and batch + 1 <= 8

    def relayout_cols(m):
        a_part, b_part, c_part = m[..., :attn_w], m[..., attn_w:attn_w + b_in], m[..., attn_w + b_in:]
        pad = lambda z, n: jnp.pad(z, [(0, 0)] * (z.ndim - 1) + [(0, n - z.shape[-1])])
        o = 3 * bw
        pieces = [b_part[..., :o]]
        for width in (lora_d, lora_d, lora_i, lora_i):
            pieces.append(pad(b_part[..., o:o + width], LORA_PAD))
            o += width
        pieces.append(b_part[..., o:])
        return jnp.concatenate(pieces + [a_part, c_part], -1)

    w_in_r = relayout_cols(w_in).astype(BF16)
    mu_r = relayout_cols(jnp.pad(shift_mu, ((0, 0), (0, 0), (attn_w, attn_w))))[..., :3 * bw + lora_w]
    pad_rows = lambda z: jnp.pad(z, ((0, 0), (0, 0), (0, LORA_PAD - z.shape[2]), (0, 0)))
    w_up_r, a_up_r = pad_rows(decay_up), pad_rows(iclr_up)
    w_out_b = w_out.astype(BF16)
    w1_b, w2_b = mlp_w1.astype(BF16), mlp_w2.astype(BF16)

    scale = HEAD_DIM ** -0.5
    tile = lambda g, n: jnp.tile(g, (1, n))

    def gains(qg, kg):
        return jnp.concatenate([tile(qg, a_heads) * scale, tile(kg, a_kv), jnp.ones((depth, kv_w), F32)], -1)

    qk_gains = jnp.stack([gains(a_q_norm, a_k_norm), gains(c_q_norm, c_k_norm)], 1)[:, :, None, :]

    cos_t, sin_t = _rope_tables(seq, tm)
    e_attn = _block_ones(attn_w)
    e_b = _block_ones(2 * LANES)

    cvec = jnp.zeros((8, d), F32).at[:batch].set(c).at[batch].set(c_ctx)
    mods = _mods(cvec, ada_w, ada_b).reshape(depth, 8, 6, 1, d)

    h = jnp.concatenate([x.reshape(batch * seq, d), ctx.reshape(batch * n_ctx, d)], 0)
    n1g, n2g = norm1_g[:, None, :], norm2_g[:, None, :]
    for l in range(depth):
        p = _inproj(h, n1g, mods, w_in_r, l, cfg)
        qkv = _qkprep(p, qk_gains[l], cos_t, sin_t, e_attn, cfg)
        o_a = _attention(qkv, 0, a_sink[l], cfg, window=True)
        o_c = _attention(qkv, 1, None, cfg, window=False)
        lw = dict(mu=mu_r[l], w0=decay_w0[l], w_up=w_up_r[l], a0=iclr_a0[l], a_up=a_up_r[l], g_up=gate_up[l],
                  k_k=k_k[l][None], k_a=k_a[l][None], r_k=r_k[l][None])
        s_v, s_f, s_b, e_tot, bonus, gate = _rwkv_prep(p, lw, e_b, cfg)
        y_f, y_b = _rwkv_scan(s_v, s_f, s_b, e_tot, cfg)
        o_b = _readout(y_f, y_b, bonus, gate, gn_g[l][None], gn_b[l][None], e_b, cfg)
        h = _outproj(o_a, o_b, o_c, w_out_b, h, mods, l, cfg)
        h = _mlp(h, n2g, mods, w1_b, w2_b, l, cfg)
    return h[:batch * seq].reshape(batch, seq, d)
```

```python
import functools

import jax
import jax.numpy as jnp
from jax import lax
from jax.experimental import pallas as pl
from jax.experimental.pallas import tpu as pltpu

F32 = jnp.float32
BF16 = jnp.bfloat16

HEAD_DIM = 64
GRID_W = 64
WINDOW = 128
ROPE_THETA = 10000.0
NORM_EPS = 1e-6
GN_EPS = 64e-5
NEG_INF = -1e30
LOG2E = 1.4426950408889634
LANES = 128
LORA_PAD = 128
CHUNK = 64
VMEM_LIMIT = 52 * 1024 * 1024


def _dot(a, b, trans_a=False, trans_b=False):
    dn = (((0 if trans_a else 1,), (1 if trans_b else 0,)), ((), ()))
    return lax.dot_general(a, b, dn, preferred_element_type=F32)


def _split(x):
    hi = x.astype(BF16)
    lo = (x - hi.astype(F32)).astype(BF16)
    return hi, lo


def _dot3(a, b, trans_a=False, trans_b=False):
    ah, al = _split(a)
    bh, bl = _split(b)
    kw = dict(trans_a=trans_a, trans_b=trans_b)
    return _dot(ah, bh, **kw) + (_dot(al, bh, **kw) + _dot(ah, bl, **kw))


def _dot1(a, b, trans_a=False, trans_b=False):
    return _dot(a.astype(BF16), b.astype(BF16), trans_a=trans_a, trans_b=trans_b)


def _group_sum(x, e_ref):
    gw = e_ref.shape[0]
    e = e_ref[...]
    outs = []
    for g in range(x.shape[1] // gw):
        hi, lo = _split(x[:, g * gw:(g + 1) * gw])
        outs.append(_dot(hi, e) + _dot(lo, e))
    return outs[0] if len(outs) == 1 else jnp.concatenate(outs, axis=1)


def _cparams(sem):
    return pltpu.CompilerParams(dimension_semantics=sem, vmem_limit_bytes=VMEM_LIMIT)


def _mods_kernel(c_ref, w_ref, b_ref, o_ref):
    c = c_ref[...]
    s = c * jax.nn.sigmoid(c)
    o_ref[0] = _dot3(s, w_ref[0]) + b_ref[0]


def _mods(cvec, ada_w, ada_b):
    depth, d, n = ada_w.shape
    tn = 512
    return pl.pallas_call(
        _mods_kernel,
        grid=(depth, n // tn),
        in_specs=[
            pl.BlockSpec((8, d), lambda l, j: (0, 0)),
            pl.BlockSpec((1, d, tn), lambda l, j: (l, 0, j)),
            pl.BlockSpec((1, 1, tn), lambda l, j: (l, 0, j)),
        ],
        out_specs=pl.BlockSpec((1, 8, tn), lambda l, j: (l, 0, j)),
        out_shape=jax.ShapeDtypeStruct((depth, 8, n), F32),
        compiler_params=_cparams(("parallel", "parallel")),
        name="adaln_mods",
    )(cvec, ada_w, ada_b.reshape(depth, 1, n))


def _modulated_norm(x, g, sc, sh):
    ms = jnp.mean(x * x, axis=-1, keepdims=True)
    return (x * lax.rsqrt(ms + NORM_EPS) * g) * (1.0 + sc) + sh


def _inproj_kernel(h_ref, g_ref, sc_ref, sh_ref, w_ref, o_ref, u_scr):
    @pl.when(pl.program_id(1) == 0)
    def _():
        u_scr[...] = _modulated_norm(h_ref[...], g_ref[...], sc_ref[...], sh_ref[...]).astype(BF16)

    o_ref[...] = _dot(u_scr[...], w_ref[...])


def _inproj(h, gain, mods, w, l, cfg):
    rows, d = h.shape
    n = w.shape[2]
    tm, tn = cfg["tm"], 768
    modrow = cfg["modrow"]
    return pl.pallas_call(
        _inproj_kernel,
        grid=(rows // tm, n // tn),
        in_specs=[
            pl.BlockSpec((tm, d), lambda i, j: (i, 0)),
            pl.BlockSpec((None, 1, d), lambda i, j: (l, 0, 0)),
            pl.BlockSpec((None, None, None, 1, d), lambda i, j: (l, modrow(i), 1, 0, 0)),
            pl.BlockSpec((None, None, None, 1, d), lambda i, j: (l, modrow(i), 0, 0, 0)),
            pl.BlockSpec((None, d, tn), lambda i, j: (l, 0, j)),
        ],
        out_specs=pl.BlockSpec((tm, tn), lambda i, j: (i, j)),
        out_shape=jax.ShapeDtypeStruct((rows, n), F32),
        scratch_shapes=[pltpu.VMEM((tm, d), BF16)],
        compiler_params=_cparams(("parallel", "arbitrary")),
        name="in_proj",
    )(h, gain, mods, mods, w)


def _qkprep_kernel(p_ref, gain_ref, cos_ref, sin_ref, e_ref, o_ref, *, qk_w):
    x = p_ref[...]
    width = x.shape[1]
    ss = _group_sum(x * x, e_ref)
    y = x * lax.rsqrt(ss * (1.0 / HEAD_DIM) + NORM_EPS) * gain_ref[0]
    cos = cos_ref[...]
    sin = sin_ref[...]
    lane = lax.broadcasted_iota(jnp.int32, (x.shape[0], LANES), 1)
    first_half = (lane & (HEAD_DIM // 2)) == 0
    for g in range(width // LANES):
        sl = slice(g * LANES, (g + 1) * LANES)
        if g * LANES < qk_w:
            yg = y[:, sl]
            partner = jnp.where(first_half, pltpu.roll(yg, LANES - HEAD_DIM // 2, 1),
                                pltpu.roll(yg, HEAD_DIM // 2, 1))
            o_ref[0, :, sl] = (yg * cos + partner * sin).astype(BF16)
        else:
            o_ref[0, :, sl] = x[:, sl].astype(BF16)


def _qkprep(p, gains, cos_t, sin_t, e_mat, cfg):
    rows = p.shape[0]
    tm, aw = cfg["tm"], cfg["attn_w"]
    first_blk = cfg["attn_col0"] // aw
    rope_blk = cfg["rope_blk"]
    return pl.pallas_call(
        functools.partial(_qkprep_kernel, qk_w=cfg["qk_w"]),
        grid=(rows // tm, 2),
        in_specs=[
            pl.BlockSpec((tm, aw), lambda i, s: (i, first_blk + s)),
            pl.BlockSpec((1, 1, aw), lambda i, s: (s, 0, 0)),
            pl.BlockSpec((tm, LANES), lambda i, s: (rope_blk(i), 0)),
            pl.BlockSpec((tm, LANES), lambda i, s: (rope_blk(i), 0)),
            pl.BlockSpec(e_mat.shape, lambda i, s: (0, 0)),
        ],
        out_specs=pl.BlockSpec((1, tm, aw), lambda i, s: (s, i, 0)),
        out_shape=jax.ShapeDtypeStruct((2, rows, aw), BF16),
        compiler_params=_cparams(("parallel", "parallel")),
        name="qk_prep",
    )(p, gains, cos_t, sin_t, e_mat)


def _attn_kernel(sink_ref, q_ref, kc_ref, vc_ref, kl_ref, vl_ref, o_ref, *,
                 window, n_heads, group, tq, tk, seq):
    i = pl.program_id(1)
    is_lat = i < seq // tq
    hd = HEAD_DIM
    n_kv = n_heads // group
    rows = group * tq
    den_lane = [((j + 1) % n_kv) * hd for j in range(n_kv)]
    if window:
        span = tq + 2 * WINDOW
        start = pl.multiple_of(jnp.clip(i * tq - WINDOW, 0, seq - span), WINDOW)
        delta = (lax.broadcasted_iota(jnp.int32, (tq, span), 0) - lax.broadcasted_iota(jnp.int32, (tq, span), 1)
                 + (i * tq - start))
        bias = jnp.where(jnp.abs(delta) <= WINDOW, 0.0, NEG_INF)
        n_iter = jnp.where(is_lat, 1, 0)
    else:
        span = tk
        n_iter = jnp.where(is_lat, seq // tk, 0)

    def with_ones(vblk, j):
        lane = lax.broadcasted_iota(jnp.int32, vblk.shape, 1)
        ones_col = jnp.where(lane == den_lane[j], 1.0, 0.0).astype(BF16)
        return jnp.where((lane >= j * hd) & (lane < (j + 1) * hd), vblk, ones_col)

    zeros = jnp.zeros((tq, hd), BF16)
    acc_lane = lax.broadcasted_iota(jnp.int32, (rows, LANES), 1)
    for j in range(n_kv):
        qs = []
        for g in range(group):
            h = j * group + g
            parts = [zeros] * n_kv
            parts[j] = q_ref[0, :, h * hd:(h + 1) * hd]
            qs.append(jnp.concatenate(parts, axis=1))
        q = jnp.concatenate(qs, axis=0)
        if sink_ref is not None:
            m0 = jnp.concatenate(
                [jnp.full((tq, 1), sink_ref[j * group + g] * LOG2E, F32) for g in range(group)], axis=0)
            acc0 = jnp.where(acc_lane == den_lane[j], 1.0, 0.0)
        else:
            m0 = jnp.full((rows, 1), NEG_INF, F32)
            acc0 = jnp.zeros((rows, LANES), F32)

        s = _dot(q, kc_ref[0], trans_b=True)
        m = jnp.maximum(m0, jnp.max(s, axis=-1, keepdims=True))
        p = jnp.exp2(s - m).astype(BF16)
        acc = jnp.exp2(m0 - m) * acc0 + _dot(p, with_ones(vc_ref[0], j))

        def body(kb, carry, q=q, j=j):
            m, acc = carry
            off = start if window else pl.multiple_of(kb * tk, tk)
            s = _dot(q, kl_ref[0, pl.ds(off, span), :], trans_b=True)
            if window:
                s = (s.reshape(group, tq, span) + bias[None]).reshape(rows, span)
            m_new = jnp.maximum(m, jnp.max(s, axis=-1, keepdims=True))
            p = jnp.exp2(s - m_new).astype(BF16)
            acc_new = jnp.exp2(m - m_new) * acc + _dot(p, with_ones(vl_ref[0, pl.ds(off, span), :], j))
            return m_new, acc_new

        m, acc = lax.fori_loop(0, n_iter, body, (m, acc))
        o = acc[:, j * hd:(j + 1) * hd] / acc[:, den_lane[j]:den_lane[j] + 1]
        for g in range(group):
            c0 = (j * group + g) * hd
            o_ref[:, c0:c0 + hd] = o[g * tq:(g + 1) * tq].astype(BF16)


def _attention(qkv, sec, sink, cfg, window):
    rows = qkv.shape[1]
    b, seq, ctx, tq = cfg["batch"], cfg["seq"], cfg["ctx"], cfg["tq"]
    n_heads, n_kv = cfg["attn_heads"], cfg["attn_kv"]
    q_w = n_heads * HEAD_DIM
    kv_w = n_kv * HEAD_DIM
    assert kv_w == LANES and q_w % kv_w == 0 and ctx == tq
    n_lat_tiles = seq // tq
    k_blk = q_w // kv_w
    ctx_blk0 = b * seq // ctx

    def q_map(bi, i, *_):
        return (sec, jnp.where(i < n_lat_tiles, bi * n_lat_tiles + i, ctx_blk0 + bi), 0)

    def o_map(bi, i, *_):
        return (jnp.where(i < n_lat_tiles, bi * n_lat_tiles + i, ctx_blk0 + bi), 0)

    tk = cfg["tk"]
    assert seq % tk == 0 and tq % WINDOW == 0 and tq + 2 * WINDOW <= seq
    kernel = functools.partial(_attn_kernel, window=window, n_heads=n_heads, group=n_heads // n_kv,
                               tq=tq, tk=tk, seq=seq)
    in_specs = [
        pl.BlockSpec((1, tq, q_w), q_map),
        pl.BlockSpec((1, ctx, kv_w), lambda bi, i, *_: (sec, ctx_blk0 + bi, k_blk)),
        pl.BlockSpec((1, ctx, kv_w), lambda bi, i, *_: (sec, ctx_blk0 + bi, k_blk + 1)),
        pl.BlockSpec((1, seq, kv_w), lambda bi, i, *_: (sec, bi, k_blk)),
        pl.BlockSpec((1, seq, kv_w), lambda bi, i, *_: (sec, bi, k_blk + 1)),
    ]
    args = [qkv, qkv, qkv, qkv, qkv]
    if sink is not None:
        in_specs = [pl.BlockSpec(memory_space=pltpu.SMEM)] + in_specs
        args = [sink] + args
    else:
        kernel = functools.partial(kernel, None)
    return pl.pallas_call(
        kernel,
        grid=(b, n_lat_tiles + 1),
        in_specs=in_specs,
        out_specs=pl.BlockSpec((tq, q_w), o_map),
        out_shape=jax.ShapeDtypeStruct((rows, q_w), BF16),
        compiler_params=_cparams(("parallel", "arbitrary")),
        name="window_attn" if window else "global_attn",
    )(*args)


def _softplus(z):
    return jnp.maximum(z, 0.0) + jnp.log(1.0 + jnp.exp(-jnp.abs(z)))


def _tri3(tri, x):
    h1 = x.astype(BF16)
    r1 = x - h1.astype(F32)
    h2 = r1.astype(BF16)
    h3 = (r1 - h2.astype(F32)).astype(BF16)
    return _dot(tri, h1) + (_dot(tri, h2) + _dot(tri, h3))


def _rwkv_prep_kernel(cur_ref, prev_ref, next_ref, mu_ref, w0_ref, wup_ref, a0_ref, aup_ref, gup_ref,
                      kk_ref, ka_ref, rk_ref, e_ref, tril_ref, triu_ref, ones_ref,
                      sv_ref, sf_ref, sb_ref, etot_ref, bonus_ref, gate_ref, *,
                      bw, seg_lat, seg_ctx, n_lat_tiles):
    i = pl.program_id(0)
    p = cur_ref[...]
    tr = p.shape[0]
    lat = i < n_lat_tiles
    seg = jnp.where(lat, seg_lat, seg_ctx)
    pos = jnp.where(lat, i, i - n_lat_tiles) % seg
    has_prev = (pos != 0).astype(F32)
    has_next = (pos != seg - 1).astype(F32)
    row = lax.broadcasted_iota(jnp.int32, (tr, 1), 0)
    prev_row = prev_ref[7:8, :] * has_prev
    next_row = next_ref[0:1, :] * has_next
    prv = jnp.where(row == 0, prev_row, pltpu.roll(p, 1, 0))
    nxt = jnp.where(row == tr - 1, next_row, pltpu.roll(p, tr - 1, 0))
    xs = p + mu_ref[0:1, :] * (prv - p) + mu_ref[1:2, :] * (nxt - p)

    r = xs[:, 0:bw]
    k = xs[:, bw:2 * bw]
    v = xs[:, 2 * bw:3 * bw]
    lora = xs[:, 3 * bw:]
    n_hp = bw // LANES

    kk = k * kk_ref[...]
    kk = kk / jnp.maximum(jnp.sqrt(_group_sum(kk * kk, e_ref)), 1e-12)
    ksum = jnp.zeros_like(k)
    ones_bd = ones_ref[...]
    for d, (s_ref, tri_ref) in enumerate(((sf_ref, tril_ref), (sb_ref, triu_ref))):
        wd = lora[:, d * LORA_PAD:(d + 1) * LORA_PAD]
        ad = lora[:, (2 + d) * LORA_PAD:(3 + d) * LORA_PAD]
        w_log = -_softplus(-(w0_ref[d:d + 1, :] + _dot3(jnp.tanh(wd), wup_ref[d]))) - 0.5
        log_decay = -jnp.exp(w_log)
        a = jax.nn.sigmoid(a0_ref[d:d + 1, :] + _dot3(ad, aup_ref[d]))
        key = k * (1.0 + (a - 1.0) * ka_ref[...])
        ksum = ksum + key
        kka = kk * a
        cum = _tri3(tri_ref[...], log_decay)
        tot = _tri3(ones_bd, log_decay)
        e_inv = jnp.exp(-cum)
        e_rel = jnp.exp(tot - cum)
        streams = (kk * jnp.exp(cum - log_decay), r * jnp.exp(cum), kka * e_inv, key * e_inv,
                   kka * e_rel, key * e_rel)
        for n, st in enumerate(streams):
            st = st.astype(BF16)
            for hp in range(n_hp):
                s_ref[n, hp] = st[:, hp * LANES:(hp + 1) * LANES]
        for cidx in range(tr // CHUNK):
            e_tot = jnp.exp(tot[cidx * CHUNK:cidx * CHUNK + 1])
            etot_ref[d, cidx] = jnp.concatenate(
                [e_tot[:, hp * LANES:(hp + 1) * LANES] for hp in range(n_hp)], axis=0)
    vb = v.astype(BF16)
    for hp in range(n_hp):
        sv_ref[hp] = vb[:, hp * LANES:(hp + 1) * LANES]
    bonus_ref[...] = _group_sum(r * ksum * rk_ref[...], e_ref) * v
    gate_ref[...] = _dot3(jax.nn.sigmoid(lora[:, 4 * LORA_PAD:]), gup_ref[...])


def _chunk_block_diag(tr, kind):
    t = jnp.arange(tr)
    same = (t[:, None] // CHUNK) == (t[None, :] // CHUNK)
    if kind == "lower":
        same = same & (t[None, :] <= t[:, None])
    elif kind == "upper":
        same = same & (t[None, :] >= t[:, None])
    return same.astype(BF16)


N_STREAMS = 6


def _rwkv_prep(p, lw, e_mat, cfg):
    rows = p.shape[0]
    bw, tr = cfg["b_width"], cfg["tr"]
    cw = 3 * bw + cfg["lora_w"]
    n_hp = bw // LANES
    n_lat_tiles = cfg["batch"] * cfg["seq"] // tr
    hb = tr // 8
    last8 = rows // 8 - 1
    full = lambda a: pl.BlockSpec(a.shape, lambda i: (0,) * a.ndim)
    consts = [lw["mu"], lw["w0"], lw["w_up"], lw["a0"], lw["a_up"], lw["g_up"], lw["k_k"], lw["k_a"], lw["r_k"],
              e_mat, _chunk_block_diag(tr, "lower"), _chunk_block_diag(tr, "upper"), _chunk_block_diag(tr, "ones")]
    stream_spec = pl.BlockSpec((N_STREAMS, n_hp, tr, LANES), lambda i: (0, 0, i, 0))
    stream_shape = jax.ShapeDtypeStruct((N_STREAMS, n_hp, rows, LANES), BF16)
    cpt = tr // CHUNK
    kernel = functools.partial(_rwkv_prep_kernel, bw=bw, seg_lat=cfg["seq"] // tr, seg_ctx=cfg["ctx"] // tr,
                               n_lat_tiles=n_lat_tiles)
    return pl.pallas_call(
        kernel,
        grid=(rows // tr,),
        in_specs=[
            pl.BlockSpec((tr, cw), lambda i: (i, 0)),
            pl.BlockSpec((8, cw), lambda i: (jnp.maximum(i * hb - 1, 0), 0)),
            pl.BlockSpec((8, cw), lambda i: (jnp.minimum((i + 1) * hb, last8), 0)),
        ] + [full(a) for a in consts],
        out_specs=[pl.BlockSpec((n_hp, tr, LANES), lambda i: (0, i, 0)),
                   stream_spec, stream_spec,
                   pl.BlockSpec((2, cpt, n_hp, LANES), lambda i: (0, i, 0, 0)),
                   pl.BlockSpec((tr, bw), lambda i: (i, 0)),
                   pl.BlockSpec((tr, bw), lambda i: (i, 0))],
        out_shape=[jax.ShapeDtypeStruct((n_hp, rows, LANES), BF16),
                   stream_shape, stream_shape,
                   jax.ShapeDtypeStruct((2, rows // CHUNK, n_hp, LANES), F32),
                   jax.ShapeDtypeStruct((rows, bw), F32),
                   jax.ShapeDtypeStruct((rows, bw), F32)],
        compiler_params=_cparams(("parallel",)),
        name="rwkv_prep",
    )(p, p, p, *consts)


def _lhs3(a):
    hi, lo = _split(a)
    return jnp.concatenate([hi, lo, hi], axis=1)


def _rhs3(b):
    hi, lo = _split(b)
    return jnp.concatenate([hi, hi, lo], axis=0)


def _scan_direction(v_ref, st_ref, e_ref, y_ref, s_ref, s_base, n_hp, upper):
    c, hd = CHUNK, HEAD_DIM
    per = LANES // hd
    heads = [(hp, h) for hp in range(n_hp) for h in range(per)]
    n = len(heads)
    rid = lax.broadcasted_iota(jnp.int32, (c, c), 0)
    cid = lax.broadcasted_iota(jnp.int32, (c, c), 1)
    incl, strict = (cid >= rid, cid > rid) if upper else (cid <= rid, cid < rid)
    eye_f = jnp.where(cid == rid, 1.0, 0.0)

    def stream(k):
        return [st_ref[k, hp, :, h * hd:(h + 1) * hd] for hp, h in heads]

    qk, rt, bt, kt, bh, kh = (stream(k) for k in range(N_STREAMS))
    vv = [v_ref[hp, :, h * hd:(h + 1) * hd] for hp, h in heads]
    p1 = [_dot(jnp.concatenate([qk[i], rt[i]], axis=0), jnp.concatenate([bt[i], kt[i]], axis=0), trans_b=True)
          for i in range(n)]
    x_pow = [jnp.where(strict, -p[:c, :c], 0.0) for p in p1]
    t = [eye_f + xi for xi in x_pow]
    x_pow = [_dot(_lhs3(xi), _rhs3(xi)) for xi in x_pow]
    sq = 2
    while sq < c:
        w = [_rhs3(xi) for xi in x_pow]
        if 2 * sq < c:
            prod = [_dot(_lhs3(jnp.concatenate([t[i], x_pow[i]], axis=0)), w[i]) for i in range(n)]
            t = [t[i] + prod[i][:c] for i in range(n)]
            x_pow = [pr[c:] for pr in prod]
        else:
            t = [t[i] + _dot(_lhs3(t[i]), w[i]) for i in range(n)]
        sq *= 2
    avy = [_dot(jnp.concatenate([jnp.where(strict, p[:c, c:], 0.0), jnp.where(incl, p[c:, c:], 0.0)],
                                axis=0).astype(BF16), vv[i]) for i, p in enumerate(p1)]
    tu = []
    for i in range(n):
        rhs = jnp.concatenate([qk[i], avy[i][:c].astype(BF16)], axis=1)
        tu.append(_dot(jnp.concatenate(_split(t[i]), axis=1), jnp.concatenate([rhs, rhs], axis=0)).astype(BF16))
    bu = [_dot(jnp.where(incl, p[c:, :c], 0.0).astype(BF16), tu[i]) for i, p in enumerate(p1)]
    rh = [(rt[i].astype(F32) - bu[i][:, :hd]).astype(BF16) for i in range(n)]
    s_old = [s_ref[s_base + i] for i in range(n)]
    s_bf = [s.astype(BF16) for s in s_old]
    y = [avy[i][c:] - bu[i][:, hd:] + _dot(rh[i], s_bf[i], trans_b=True) for i in range(n)]
    for hp in range(n_hp):
        y_ref[hp] = jnp.concatenate(y[hp * per:(hp + 1) * per], axis=1)
    mt = [_dot(tu[i], bh[i], trans_a=True) for i in range(n)]
    hct = [_dot(vv[i], kh[i], trans_a=True) - mt[i][hd:] for i in range(n)]
    for i, (hp, h) in enumerate(heads):
        e_row = e_ref[hp:hp + 1, h * hd:(h + 1) * hd]
        s_ref[s_base + i] = s_old[i] * e_row + hct[i] - _dot(s_bf[i], mt[i][:hd].astype(BF16))


def _rwkv_scan_kernel(vf_ref, sf_ref, ef_ref, vb_ref, sb_ref, eb_ref, yf_ref, yb_ref, s_ref, *, n_hp):
    @pl.when(pl.program_id(1) == 0)
    def _():
        s_ref[...] = jnp.zeros_like(s_ref)

    n_heads = n_hp * (LANES // HEAD_DIM)
    _scan_direction(vf_ref, sf_ref, ef_ref, yf_ref, s_ref, 0, n_hp, upper=False)
    _scan_direction(vb_ref, sb_ref, eb_ref, yb_ref, s_ref, n_heads, n_hp, upper=True)


def _rwkv_scan(s_v, s_f, s_b, e_tot, cfg):
    n_hp, rows, _ = s_v.shape
    b, seq, ctx = cfg["batch"], cfg["seq"], cfg["ctx"]
    c = CHUNK
    ncc, ncl = ctx // c, seq // c
    ctx0 = b * seq // c

    def fwd_blk(bi, j):
        return jnp.where(j < ncc, ctx0 + bi * ncc + j, bi * ncl + (j - ncc))

    def bwd_blk(bi, j):
        return jnp.where(j < ncc, ctx0 + bi * ncc + (ncc - 1 - j), bi * ncl + (ncl - 1 - (j - ncc)))

    def specs(blk, d):
        return [pl.BlockSpec((n_hp, c, LANES), lambda bi, j: (0, blk(bi, j), 0)),
                pl.BlockSpec((N_STREAMS, n_hp, c, LANES), lambda bi, j: (0, 0, blk(bi, j), 0)),
                pl.BlockSpec((None, None, n_hp, LANES), lambda bi, j: (d, blk(bi, j), 0, 0))]

    out_f = pl.BlockSpec((n_hp, c, LANES), lambda bi, j: (0, fwd_blk(bi, j), 0))
    out_b = pl.BlockSpec((n_hp, c, LANES), lambda bi, j: (0, bwd_blk(bi, j), 0))
    y_shape = jax.ShapeDtypeStruct((n_hp, rows, LANES), F32)
    return pl.pallas_call(
        functools.partial(_rwkv_scan_kernel, n_hp=n_hp),
        grid=(b, ncc + ncl),
        in_specs=specs(fwd_blk, 0) + specs(bwd_blk, 1),
        out_specs=[out_f, out_b],
        out_shape=[y_shape, y_shape],
        scratch_shapes=[pltpu.VMEM((2 * n_hp * (LANES // HEAD_DIM), HEAD_DIM, HEAD_DIM), F32)],
        compiler_params=_cparams(("parallel", "arbitrary")),
        name="rwkv_scan",
    )(s_v, s_f, e_tot, s_v, s_b, e_tot)


def _readout_kernel(yf_ref, yb_ref, bonus_ref, gate_ref, gg_ref, gb_ref, e_ref, o_ref):
    n_hp = yf_ref.shape[0]
    y = jnp.concatenate([yf_ref[hp] + yb_ref[hp] for hp in range(n_hp)], axis=1)
    mu = _group_sum(y, e_ref) * (1.0 / HEAD_DIM)
    yc = y - mu
    var = _group_sum(yc * yc, e_ref) * (1.0 / HEAD_DIM)
    yn = yc * lax.rsqrt(var + GN_EPS) * gg_ref[...] + gb_ref[...]
    o_ref[...] = ((yn + bonus_ref[...]) * gate_ref[...]).astype(BF16)


def _readout(y_f, y_b, bonus, gate, gn_g, gn_b, e_mat, cfg):
    n_hp, rows, _ = y_f.shape
    bw, tm = cfg["b_width"], cfg["tm"]
    y_spec = pl.BlockSpec((n_hp, tm, LANES), lambda i: (0, i, 0))
    row_spec = pl.BlockSpec((tm, bw), lambda i: (i, 0))
    vec_spec = pl.BlockSpec((1, bw), lambda i: (0, 0))
    return pl.pallas_call(
        _readout_kernel,
        grid=(rows // tm,),
        in_specs=[y_spec, y_spec, row_spec, row_spec, vec_spec, vec_spec,
                  pl.BlockSpec(e_mat.shape, lambda i: (0, 0))],
        out_specs=row_spec,
        out_shape=jax.ShapeDtypeStruct((rows, bw), BF16),
        compiler_params=_cparams(("parallel",)),
        name="rwkv_readout",
    )(y_f, y_b, bonus, gate, gn_g, gn_b, e_mat)


def _outproj_kernel(oa_ref, ob_ref, oc_ref, w_ref, h_ref, g_ref, o_ref):
    mix = jnp.concatenate([oa_ref[...], ob_ref[...], oc_ref[...]], axis=1)
    o_ref[...] = h_ref[...] + g_ref[...] * _dot(mix, w_ref[...])


def _outproj(o_a, o_b, o_c, w, h, mods, l, cfg):
    rows, d = h.shape
    tm, tn = cfg["tm"], 1024
    modrow = cfg["modrow"]
    act = lambda a: pl.BlockSpec((tm, a.shape[1]), lambda i, j: (i, 0))
    return pl.pallas_call(
        _outproj_kernel,
        grid=(rows // tm, d // tn),
        in_specs=[act(o_a), act(o_b), act(o_c),
                  pl.BlockSpec((None, d, tn), lambda i, j: (l, 0, j)),
                  pl.BlockSpec((tm, tn), lambda i, j: (i, j)),
                  pl.BlockSpec((None, None, None, 1, tn), lambda i, j: (l, modrow(i), 2, 0, j))],
        out_specs=pl.BlockSpec((tm, tn), lambda i, j: (i, j)),
        out_shape=jax.ShapeDtypeStruct((rows, d), F32),
        compiler_params=_cparams(("parallel", "arbitrary")),
        name="out_proj",
    )(o_a, o_b, o_c, w, h, mods)


def _mlp_kernel(h_ref, g_ref, sc_ref, sh_ref, gate_ref, w1_ref, w2_ref, o_ref, u_scr, acc_scr):
    f = pl.program_id(1)

    @pl.when(f == 0)
    def _():
        u_scr[...] = _modulated_norm(h_ref[...], g_ref[...], sc_ref[...], sh_ref[...]).astype(BF16)

    a = jnp.maximum(_dot(u_scr[...], w1_ref[...]), 0.0)
    part = _dot((a * a).astype(BF16), w2_ref[...])

    @pl.when(f == 0)
    def _():
        acc_scr[...] = part

    @pl.when(f > 0)
    def _():
        acc_scr[...] += part

    @pl.when(f == pl.num_programs(1) - 1)
    def _():
        o_ref[...] = h_ref[...] + gate_ref[...] * acc_scr[...]


def _mlp(h, gain, mods, w1, w2, l, cfg):
    rows, d = h.shape
    dff = w1.shape[2]
    tm, tf = cfg["tm"], 1024
    modrow = cfg["modrow"]
    mod = lambda k: pl.BlockSpec((None, None, None, 1, d), lambda i, f: (l, modrow(i), k, 0, 0))
    return pl.pallas_call(
        _mlp_kernel,
        grid=(rows // tm, dff // tf),
        in_specs=[
            pl.BlockSpec((tm, d), lambda i, f: (i, 0)),
            pl.BlockSpec((None, 1, d), lambda i, f: (l, 0, 0)),
            mod(4), mod(3), mod(5),
            pl.BlockSpec((None, d, tf), lambda i, f: (l, 0, f)),
            pl.BlockSpec((None, tf, d), lambda i, f: (l, f, 0)),
        ],
        out_specs=pl.BlockSpec((tm, d), lambda i, f: (i, 0)),
        out_shape=jax.ShapeDtypeStruct((rows, d), F32),
        scratch_shapes=[pltpu.VMEM((tm, d), BF16), pltpu.VMEM((tm, d), F32)],
        compiler_params=_cparams(("parallel", "arbitrary")),
        name="mlp",
    )(h, gain, mods, mods, mods, w1, w2)


def _rope_tables(seq, tm):
    rows = seq // GRID_W
    row = jnp.broadcast_to(jnp.arange(rows)[:, None], (rows, GRID_W)).reshape(-1)
    col = jnp.broadcast_to(jnp.arange(GRID_W)[None, :], (rows, GRID_W)).reshape(-1)
    n_freq = HEAD_DIM // 4
    inv = ROPE_THETA ** (-jnp.arange(n_freq, dtype=F32) / n_freq)
    ang = jnp.concatenate([row[:, None].astype(F32) * inv, col[:, None].astype(F32) * inv], -1)
    cos, sin = jnp.cos(ang), jnp.sin(ang)
    reps = LANES // HEAD_DIM
    cos_t = jnp.tile(jnp.concatenate([cos, cos], -1), (1, reps))
    sin_t = jnp.tile(jnp.concatenate([-sin, sin], -1), (1, reps))
    cos_t = jnp.concatenate([cos_t, jnp.ones((tm, LANES), F32)], 0)
    sin_t = jnp.concatenate([sin_t, jnp.zeros((tm, LANES), F32)], 0)
    return cos_t, sin_t


def _block_ones(width):
    g = jnp.arange(width) // HEAD_DIM
    return (g[:, None] == g[None, :]).astype(BF16)


def kernel(x, c, ctx, c_ctx, ada_w, ada_b, norm1_g, norm2_g, w_in, a_q_norm, a_k_norm, a_sink, c_q_norm,
           c_k_norm, shift_mu, decay_w0, decay_up, iclr_a0, iclr_up, gate_up, k_k, k_a, r_k, gn_g, gn_b,
           w_out, mlp_w1, mlp_w2):
    batch, seq, d = x.shape
    n_ctx = ctx.shape[1]
    depth = ada_w.shape[0]
    bw = k_k.shape[1]
    lora_d, lora_i, lora_g = decay_up.shape[2], iclr_up.shape[2], gate_up.shape[1]
    a_heads = a_sink.shape[1]
    a_kv = a_heads // 4
    q_w, kv_w = a_heads * HEAD_DIM, a_kv * HEAD_DIM
    attn_w = q_w + 2 * kv_w
    lora_w = 4 * LORA_PAD + lora_g
    b_in = 3 * bw + 2 * lora_d + 2 * lora_i + lora_g
    assert lora_d <= LORA_PAD and lora_i <= LORA_PAD
    assert w_in.shape[2] == 2 * attn_w + b_in

    tm = batch * n_ctx
    n_lat_tiles = batch * seq // tm
    tiles_per_batch = seq // tm
    cfg = dict(
        batch=batch, seq=seq, ctx=n_ctx, tm=tm, tr=n_ctx, tq=n_ctx, tk=512,
        b_width=bw, lora_w=lora_w, attn_w=attn_w, attn_col0=3 * bw + lora_w, qk_w=q_w + kv_w,
        attn_heads=a_heads, attn_kv=a_kv,
        modrow=lambda i: jnp.where(i < n_lat_tiles, i // tiles_per_batch, batch),
        rope_blk=lambda i: jnp.where(i < n_lat_tiles, i % tiles_per_batch, tiles_per_batch),
    )
    assert seq % tm == 0 and seq % GRID_W == 0 and n_ctx % CHUNK == 0 and (3 * bw) % lora_w == 0
    assert cfg["attn_col0"] % attn_w == 0 and batch + 1 <= 8

    def relayout_cols(m):
        a_part, b_part, c_part = m[..., :attn_w], m[..., attn_w:attn_w + b_in], m[..., attn_w + b_in:]
        pad = lambda z, n: jnp.pad(z, [(0, 0)] * (z.ndim - 1) + [(0, n - z.shape[-1])])
        o = 3 * bw
        pieces = [b_part[..., :o]]
        for width in (lora_d, lora_d, lora_i, lora_i):
            pieces.append(pad(b_part[..., o:o + width], LORA_PAD))
            o += width
        pieces.append(b_part[..., o:])
        return jnp.concatenate(pieces + [a_part, c_part], -1)

    w_in_r = relayout_cols(w_in).astype(BF16)
    mu_r = relayout_cols(jnp.pad(shift_mu, ((0, 0), (0, 0), (attn_w, attn_w))))[..., :3 * bw + lora_w]
    pad_rows = lambda z: jnp.pad(z, ((0, 0), (0, 0), (0, LORA_PAD - z.shape[2]), (0, 0)))
    w_up_r, a_up_r = pad_rows(decay_up), pad_rows(iclr_up)
    w_out_b = w_out.astype(BF16)
    w1_b, w2_b = mlp_w1.astype(BF16), mlp_w2.astype(BF16)

    scale = HEAD_DIM ** -0.5 * LOG2E
    tile = lambda g, n: jnp.tile(g, (1, n))

    def gains(qg, kg):
        return jnp.concatenate([tile(qg, a_heads) * scale, tile(kg, a_kv), jnp.ones((depth, kv_w), F32)], -1)

    qk_gains = jnp.stack([gains(a_q_norm, a_k_norm), gains(c_q_norm, c_k_norm)], 1)[:, :, None, :]

    cos_t, sin_t = _rope_tables(seq, tm)
    e_attn = _block_ones(attn_w)
    e_b = _block_ones(2 * LANES)

    cvec = jnp.zeros((8, d), F32).at[:batch].set(c).at[batch].set(c_ctx)
    mods = _mods(cvec, ada_w, ada_b).reshape(depth, 8, 6, 1, d)

    h = jnp.concatenate([x.reshape(batch * seq, d), ctx.reshape(batch * n_ctx, d)], 0)
    n1g, n2g = norm1_g[:, None, :], norm2_g[:, None, :]
    for l in range(depth):
        p = _inproj(h, n1g, mods, w_in_r, l, cfg)
        qkv = _qkprep(p, qk_gains[l], cos_t, sin_t, e_attn, cfg)
        o_a = _attention(qkv, 0, a_sink[l], cfg, window=True)
        o_c = _attention(qkv, 1, None, cfg, window=False)
        lw = dict(mu=mu_r[l], w0=decay_w0[l], w_up=w_up_r[l], a0=iclr_a0[l], a_up=a_up_r[l], g_up=gate_up[l],
                  k_k=k_k[l][None], k_a=k_a[l][None], r_k=r_k[l][None])
        s_v, s_f, s_b, e_tot, bonus, gate = _rwkv_prep(p, lw, e_b, cfg)
        y_f, y_b = _rwkv_scan(s_v, s_f, s_b, e_tot, cfg)
        o_b = _readout(y_f, y_b, bonus, gate, gn_g[l][None], gn_b[l][None], e_b, cfg)
        h = _outproj(o_a, o_b, o_c, w_out_b, h, mods, l, cfg)
        h = _mlp(h, n2g, mods, w1_b, w2_b, l, cfg)
    return h[:batch * seq].reshape(batch, seq, d)
```

```python
import functools

import jax
import jax.numpy as jnp
from jax import lax
from jax.experimental import pallas as pl
from jax.experimental.pallas import tpu as pltpu

F32 = jnp.float32
BF16 = jnp.bfloat16

HEAD_DIM = 64
GRID_W = 64
WINDOW = 128
ROPE_THETA = 10000.0
NORM_EPS = 1e-6
GN_EPS = 64e-5
NEG_INF = -1e30
LOG2E = 1.4426950408889634
DECAY_SCALE = 0.6065306597126334
LANES = 128
LORA_PAD = 128
CHUNK = 64
VMEM_LIMIT = 52 * 1024 * 1024


def _dot(a, b, trans_a=False, trans_b=False):
    dn = (((0 if trans_a else 1,), (1 if trans_b else 0,)), ((), ()))
    return lax.dot_general(a, b, dn, preferred_element_type=F32)


def _split(x):
    hi = x.astype(BF16)
    lo = (x - hi.astype(F32)).astype(BF16)
    return hi, lo


def _dot3(a, b, trans_a=False, trans_b=False):
    ah, al = _split(a)
    bh, bl = _split(b)
    kw = dict(trans_a=trans_a, trans_b=trans_b)
    return _dot(ah, bh, **kw) + (_dot(al, bh, **kw) + _dot(ah, bl, **kw))


def _dot1(a, b, trans_a=False, trans_b=False):
    return _dot(a.astype(BF16), b.astype(BF16), trans_a=trans_a, trans_b=trans_b)


def _group_sum(x, e_ref):
    gw = e_ref.shape[0]
    e = e_ref[...]
    outs = []
    for g in range(x.shape[1] // gw):
        hi, lo = _split(x[:, g * gw:(g + 1) * gw])
        outs.append(_dot(hi, e) + _dot(lo, e))
    return outs[0] if len(outs) == 1 else jnp.concatenate(outs, axis=1)


def _cparams(sem):
    return pltpu.CompilerParams(dimension_semantics=sem, vmem_limit_bytes=VMEM_LIMIT)


def _mods_kernel(c_ref, w_ref, b_ref, o_ref):
    c = c_ref[...]
    s = c * jax.nn.sigmoid(c)
    o_ref[0] = _dot3(s, w_ref[0]) + b_ref[0]


def _mods(cvec, ada_w, ada_b):
    depth, d, n = ada_w.shape
    tn = 512
    return pl.pallas_call(
        _mods_kernel,
        grid=(depth, n // tn),
        in_specs=[
            pl.BlockSpec((8, d), lambda l, j: (0, 0)),
            pl.BlockSpec((1, d, tn), lambda l, j: (l, 0, j)),
            pl.BlockSpec((1, 1, tn), lambda l, j: (l, 0, j)),
        ],
        out_specs=pl.BlockSpec((1, 8, tn), lambda l, j: (l, 0, j)),
        out_shape=jax.ShapeDtypeStruct((depth, 8, n), F32),
        compiler_params=_cparams(("parallel", "parallel")),
        name="adaln_mods",
    )(cvec, ada_w, ada_b.reshape(depth, 1, n))


def _modulated_norm(x, g, sc, sh):
    ms = jnp.mean(x * x, axis=-1, keepdims=True)
    return (x * lax.rsqrt(ms + NORM_EPS) * g) * (1.0 + sc) + sh


def _inproj_kernel(h_ref, g_ref, sc_ref, sh_ref, w_ref, o_ref, u_scr):
    @pl.when(pl.program_id(1) == 0)
    def _():
        u_scr[...] = _modulated_norm(h_ref[...], g_ref[...], sc_ref[...], sh_ref[...]).astype(BF16)

    o_ref[...] = _dot(u_scr[...], w_ref[...])


def _inproj(h, gain, mods, w, l, cfg):
    rows, d = h.shape
    n = w.shape[2]
    tm, tn = cfg["tm"], 1792
    modrow = cfg["modrow"]
    return pl.pallas_call(
        _inproj_kernel,
        grid=(rows // tm, n // tn),
        in_specs=[
            pl.BlockSpec((tm, d), lambda i, j: (i, 0)),
            pl.BlockSpec((None, 1, d), lambda i, j: (l, 0, 0)),
            pl.BlockSpec((None, None, None, 1, d), lambda i, j: (l, modrow(i), 1, 0, 0)),
            pl.BlockSpec((None, None, None, 1, d), lambda i, j: (l, modrow(i), 0, 0, 0)),
            pl.BlockSpec((None, d, tn), lambda i, j: (l, 0, j)),
        ],
        out_specs=pl.BlockSpec((tm, tn), lambda i, j: (i, j)),
        out_shape=jax.ShapeDtypeStruct((rows, n), F32),
        scratch_shapes=[pltpu.VMEM((tm, d), BF16)],
        compiler_params=_cparams(("parallel", "arbitrary")),
        name="in_proj",
    )(h, gain, mods, mods, w)


def _qkprep_kernel(p_ref, gain_ref, cos_ref, sin_ref, e_ref, o_ref, *, qk_w):
    x = p_ref[...]
    width = x.shape[1]
    ss = _group_sum(x * x, e_ref)
    y = x * lax.rsqrt(ss * (1.0 / HEAD_DIM) + NORM_EPS) * gain_ref[0]
    cos = cos_ref[...]
    sin = sin_ref[...]
    lane = lax.broadcasted_iota(jnp.int32, (x.shape[0], LANES), 1)
    first_half = (lane & (HEAD_DIM // 2)) == 0
    for g in range(width // LANES):
        sl = slice(g * LANES, (g + 1) * LANES)
        if g * LANES < qk_w:
            yg = y[:, sl]
            partner = jnp.where(first_half, pltpu.roll(yg, LANES - HEAD_DIM // 2, 1),
                                pltpu.roll(yg, HEAD_DIM // 2, 1))
            o_ref[0, :, sl] = (yg * cos + partner * sin).astype(BF16)
        else:
            o_ref[0, :, sl] = x[:, sl].astype(BF16)


def _qkprep(p, gains, cos_t, sin_t, e_mat, cfg):
    rows = p.shape[0]
    tm, aw = cfg["tm"], cfg["attn_w"]
    first_blk = cfg["attn_col0"] // aw
    rope_blk = cfg["rope_blk"]
    return pl.pallas_call(
        functools.partial(_qkprep_kernel, qk_w=cfg["qk_w"]),
        grid=(rows // tm, 2),
        in_specs=[
            pl.BlockSpec((tm, aw), lambda i, s: (i, first_blk + s)),
            pl.BlockSpec((1, 1, aw), lambda i, s: (s, 0, 0)),
            pl.BlockSpec((tm, LANES), lambda i, s: (rope_blk(i), 0)),
            pl.BlockSpec((tm, LANES), lambda i, s: (rope_blk(i), 0)),
            pl.BlockSpec(e_mat.shape, lambda i, s: (0, 0)),
        ],
        out_specs=pl.BlockSpec((1, tm, aw), lambda i, s: (s, i, 0)),
        out_shape=jax.ShapeDtypeStruct((2, rows, aw), BF16),
        compiler_params=_cparams(("parallel", "parallel")),
        name="qk_prep",
    )(p, gains, cos_t, sin_t, e_mat)


def _attn_kernel(sink_ref, q_ref, kc_ref, vc_ref, kl_ref, vl_ref, o_ref, *,
                 window, n_heads, group, tq, tk, seq):
    i = pl.program_id(1)
    is_lat = i < seq // tq
    hd = HEAD_DIM
    n_kv = n_heads // group
    rows = group * tq
    den_lane = [((j + 1) % n_kv) * hd for j in range(n_kv)]
    if window:
        span = tq + 2 * WINDOW
        start = pl.multiple_of(jnp.clip(i * tq - WINDOW, 0, seq - span), WINDOW)
        delta = (lax.broadcasted_iota(jnp.int32, (tq, span), 0) - lax.broadcasted_iota(jnp.int32, (tq, span), 1)
                 + (i * tq - start))
        bias = jnp.where(jnp.abs(delta) <= WINDOW, 0.0, NEG_INF)
        n_iter = jnp.where(is_lat, 1, 0)
    else:
        span = tk
        n_iter = jnp.where(is_lat, seq // tk, 0)

    def with_ones(vblk, j):
        lane = lax.broadcasted_iota(jnp.int32, vblk.shape, 1)
        ones_col = jnp.where(lane == den_lane[j], 1.0, 0.0).astype(BF16)
        return jnp.where((lane >= j * hd) & (lane < (j + 1) * hd), vblk, ones_col)

    zeros = jnp.zeros((tq, hd), BF16)
    acc_lane = lax.broadcasted_iota(jnp.int32, (rows, LANES), 1)
    for j in range(n_kv):
        qs = []
        for g in range(group):
            h = j * group + g
            parts = [zeros] * n_kv
            parts[j] = q_ref[0, :, h * hd:(h + 1) * hd]
            qs.append(jnp.concatenate(parts, axis=1))
        q = jnp.concatenate(qs, axis=0)
        if sink_ref is not None:
            m0 = jnp.concatenate(
                [jnp.full((tq, 1), sink_ref[j * group + g] * LOG2E, F32) for g in range(group)], axis=0)
            acc0 = jnp.where(acc_lane == den_lane[j], 1.0, 0.0)
        else:
            m0 = jnp.full((rows, 1), NEG_INF, F32)
            acc0 = jnp.zeros((rows, LANES), F32)

        s = _dot(q, kc_ref[0], trans_b=True)
        m = jnp.maximum(m0, jnp.max(s, axis=-1, keepdims=True))
        p = jnp.exp2(s - m).astype(BF16)
        acc = jnp.exp2(m0 - m) * acc0 + _dot(p, with_ones(vc_ref[0], j))

        def body(kb, carry, q=q, j=j):
            m, acc = carry
            off = start if window else pl.multiple_of(kb * tk, tk)
            s = _dot(q, kl_ref[0, pl.ds(off, span), :], trans_b=True)
            if window:
                s = (s.reshape(group, tq, span) + bias[None]).reshape(rows, span)
            m_new = jnp.maximum(m, jnp.max(s, axis=-1, keepdims=True))
            p = jnp.exp2(s - m_new).astype(BF16)
            acc_new = jnp.exp2(m - m_new) * acc + _dot(p, with_ones(vl_ref[0, pl.ds(off, span), :], j))
            return m_new, acc_new

        m, acc = lax.fori_loop(0, n_iter, body, (m, acc))
        o = acc[:, j * hd:(j + 1) * hd] / acc[:, den_lane[j]:den_lane[j] + 1]
        for g in range(group):
            c0 = (j * group + g) * hd
            o_ref[:, c0:c0 + hd] = o[g * tq:(g + 1) * tq].astype(BF16)


def _attention(qkv, sec, sink, cfg, window):
    rows = qkv.shape[1]
    b, seq, ctx, tq = cfg["batch"], cfg["seq"], cfg["ctx"], cfg["tq"]
    n_heads, n_kv = cfg["attn_heads"], cfg["attn_kv"]
    q_w = n_heads * HEAD_DIM
    kv_w = n_kv * HEAD_DIM
    assert kv_w == LANES and q_w % kv_w == 0 and ctx == tq
    n_lat_tiles = seq // tq
    k_blk = q_w // kv_w
    ctx_blk0 = b * seq // ctx

    def q_map(bi, i, *_):
        return (sec, jnp.where(i < n_lat_tiles, bi * n_lat_tiles + i, ctx_blk0 + bi), 0)

    def o_map(bi, i, *_):
        return (jnp.where(i < n_lat_tiles, bi * n_lat_tiles + i, ctx_blk0 + bi), 0)

    tk = cfg["tk"]
    assert seq % tk == 0 and tq % WINDOW == 0 and tq + 2 * WINDOW <= seq
    kernel = functools.partial(_attn_kernel, window=window, n_heads=n_heads, group=n_heads // n_kv,
                               tq=tq, tk=tk, seq=seq)
    in_specs = [
        pl.BlockSpec((1, tq, q_w), q_map),
        pl.BlockSpec((1, ctx, kv_w), lambda bi, i, *_: (sec, ctx_blk0 + bi, k_blk)),
        pl.BlockSpec((1, ctx, kv_w), lambda bi, i, *_: (sec, ctx_blk0 + bi, k_blk + 1)),
        pl.BlockSpec((1, seq, kv_w), lambda bi, i, *_: (sec, bi, k_blk)),
        pl.BlockSpec((1, seq, kv_w), lambda bi, i, *_: (sec, bi, k_blk + 1)),
    ]
    args = [qkv, qkv, qkv, qkv, qkv]
    if sink is not None:
        in_specs = [pl.BlockSpec(memory_space=pltpu.SMEM)] + in_specs
        args = [sink] + args
    else:
        kernel = functools.partial(kernel, None)
    return pl.pallas_call(
        kernel,
        grid=(b, n_lat_tiles + 1),
        in_specs=in_specs,
        out_specs=pl.BlockSpec((tq, q_w), o_map),
        out_shape=jax.ShapeDtypeStruct((rows, q_w), BF16),
        compiler_params=_cparams(("parallel", "arbitrary")),
        name="window_attn" if window else "global_attn",
    )(*args)


def _tri3(tri, x):
    h1 = x.astype(BF16)
    r1 = x - h1.astype(F32)
    h2 = r1.astype(BF16)
    h3 = (r1 - h2.astype(F32)).astype(BF16)
    return _dot(tri, h1) + (_dot(tri, h2) + _dot(tri, h3))


def _rwkv_prep_kernel(cur_ref, prev_ref, next_ref, mu_ref, w0_ref, wup_ref, a0_ref, aup_ref, gup_ref,
                      kk_ref, ka_ref, rk_ref, e_ref, tril_ref, triu_ref,
                      sv_ref, sf_ref, sb_ref, etot_ref, bonus_ref, gate_ref, *,
                      bw, seg_lat, seg_ctx, n_lat_tiles):
    i = pl.program_id(0)
    p = cur_ref[...]
    tr = p.shape[0]
    lat = i < n_lat_tiles
    seg = jnp.where(lat, seg_lat, seg_ctx)
    pos = jnp.where(lat, i, i - n_lat_tiles) % seg
    has_prev = (pos != 0).astype(F32)
    has_next = (pos != seg - 1).astype(F32)
    row = lax.broadcasted_iota(jnp.int32, (tr, 1), 0)
    prev_row = prev_ref[7:8, :] * has_prev
    next_row = next_ref[0:1, :] * has_next
    prv = jnp.where(row == 0, prev_row, pltpu.roll(p, 1, 0))
    nxt = jnp.where(row == tr - 1, next_row, pltpu.roll(p, tr - 1, 0))
    xs = p + mu_ref[0:1, :] * (prv - p) + mu_ref[1:2, :] * (nxt - p)

    r = xs[:, 0:bw]
    k = xs[:, bw:2 * bw]
    v = xs[:, 2 * bw:3 * bw]
    lora = xs[:, 3 * bw:]
    n_hp = bw // LANES

    kk = k * kk_ref[...]
    kk = kk * lax.rsqrt(jnp.maximum(_group_sum(kk * kk, e_ref), 1e-24))
    ksum = jnp.zeros_like(k)
    for d, (s_ref, tri_ref) in enumerate(((sf_ref, tril_ref), (sb_ref, triu_ref))):
        wd = lora[:, d * LORA_PAD:(d + 1) * LORA_PAD]
        ad = lora[:, (2 + d) * LORA_PAD:(3 + d) * LORA_PAD]
        log_decay = -DECAY_SCALE * jax.nn.sigmoid(w0_ref[d:d + 1, :] + _dot3(jnp.tanh(wd), wup_ref[d]))
        a = jax.nn.sigmoid(a0_ref[d:d + 1, :] + _dot3(ad, aup_ref[d]))
        key = k * (1.0 + (a - 1.0) * ka_ref[...])
        ksum = ksum + key
        kka = kk * a
        cum = _tri3(tri_ref[...], log_decay)
        last = CHUNK - 1 if d == 0 else 0
        tot_rows = [cum[c * CHUNK + last:c * CHUNK + last + 1] for c in range(tr // CHUNK)]
        tot = jnp.concatenate([jnp.broadcast_to(t, (CHUNK, bw)) for t in tot_rows], axis=0)
        e_inv = jnp.exp(-cum)
        e_rel = jnp.exp(tot - cum)
        streams = (kk * jnp.exp(cum - log_decay), r * jnp.exp(cum), kka * e_inv, key * e_inv,
                   kka * e_rel, key * e_rel)
        for n, st in enumerate(streams):
            st = st.astype(BF16)
            for hp in range(n_hp):
                s_ref[n, hp] = st[:, hp * LANES:(hp + 1) * LANES]
        for cidx in range(tr // CHUNK):
            e_tot = jnp.exp(tot_rows[cidx])
            etot_ref[d, cidx] = jnp.concatenate(
                [e_tot[:, hp * LANES:(hp + 1) * LANES] for hp in range(n_hp)], axis=0)
    vb = v.astype(BF16)
    for hp in range(n_hp):
        sv_ref[hp] = vb[:, hp * LANES:(hp + 1) * LANES]
    bonus_ref[...] = _group_sum(r * ksum * rk_ref[...], e_ref) * v
    gate_ref[...] = _dot3(jax.nn.sigmoid(lora[:, 4 * LORA_PAD:]), gup_ref[...])


def _chunk_block_diag(tr, kind):
    t = jnp.arange(tr)
    same = (t[:, None] // CHUNK) == (t[None, :] // CHUNK)
    if kind == "lower":
        same = same & (t[None, :] <= t[:, None])
    elif kind == "upper":
        same = same & (t[None, :] >= t[:, None])
    return same.astype(BF16)


N_STREAMS = 6


def _rwkv_prep(p, lw, e_mat, cfg):
    rows = p.shape[0]
    bw, tr = cfg["b_width"], cfg["tr"]
    cw = 3 * bw + cfg["lora_w"]
    n_hp = bw // LANES
    n_lat_tiles = cfg["batch"] * cfg["seq"] // tr
    hb = tr // 8
    last8 = rows // 8 - 1
    full = lambda a: pl.BlockSpec(a.shape, lambda i: (0,) * a.ndim)
    consts = [lw["mu"], lw["w0"], lw["w_up"], lw["a0"], lw["a_up"], lw["g_up"], lw["k_k"], lw["k_a"], lw["r_k"],
              e_mat, _chunk_block_diag(tr, "lower"), _chunk_block_diag(tr, "upper")]
    stream_spec = pl.BlockSpec((N_STREAMS, n_hp, tr, LANES), lambda i: (0, 0, i, 0))
    stream_shape = jax.ShapeDtypeStruct((N_STREAMS, n_hp, rows, LANES), BF16)
    cpt = tr // CHUNK
    kernel = functools.partial(_rwkv_prep_kernel, bw=bw, seg_lat=cfg["seq"] // tr, seg_ctx=cfg["ctx"] // tr,
                               n_lat_tiles=n_lat_tiles)
    return pl.pallas_call(
        kernel,
        grid=(rows // tr,),
        in_specs=[
            pl.BlockSpec((tr, cw), lambda i: (i, 0)),
            pl.BlockSpec((8, cw), lambda i: (jnp.maximum(i * hb - 1, 0), 0)),
            pl.BlockSpec((8, cw), lambda i: (jnp.minimum((i + 1) * hb, last8), 0)),
        ] + [full(a) for a in consts],
        out_specs=[pl.BlockSpec((n_hp, tr, LANES), lambda i: (0, i, 0)),
                   stream_spec, stream_spec,
                   pl.BlockSpec((2, cpt, n_hp, LANES), lambda i: (0, i, 0, 0)),
                   pl.BlockSpec((tr, bw), lambda i: (i, 0)),
                   pl.BlockSpec((tr, bw), lambda i: (i, 0))],
        out_shape=[jax.ShapeDtypeStruct((n_hp, rows, LANES), BF16),
                   stream_shape, stream_shape,
                   jax.ShapeDtypeStruct((2, rows // CHUNK, n_hp, LANES), F32),
                   jax.ShapeDtypeStruct((rows, bw), F32),
                   jax.ShapeDtypeStruct((rows, bw), F32)],
        compiler_params=_cparams(("parallel",)),
        name="rwkv_prep",
    )(p, p, p, *consts)


def _lhs3(a):
    hi, lo = _split(a)
    return jnp.concatenate([hi, lo, hi], axis=1)


def _rhs3(b):
    hi, lo = _split(b)
    return jnp.concatenate([hi, hi, lo], axis=0)


def _scan_direction(v_ref, st_ref, e_ref, y_ref, s_ref, s_base, n_hp, upper):
    c, hd = CHUNK, HEAD_DIM
    per = LANES // hd
    heads = [(hp, h) for hp in range(n_hp) for h in range(per)]
    n = len(heads)
    rid = lax.broadcasted_iota(jnp.int32, (c, c), 0)
    cid = lax.broadcasted_iota(jnp.int32, (c, c), 1)
    incl, strict = (cid >= rid, cid > rid) if upper else (cid <= rid, cid < rid)
    eye_f = jnp.where(cid == rid, 1.0, 0.0)

    def stream(k):
        return [st_ref[k, hp, :, h * hd:(h + 1) * hd] for hp, h in heads]

    qk, rt, bt, kt, bh, kh = (stream(k) for k in range(N_STREAMS))
    vv = [v_ref[hp, :, h * hd:(h + 1) * hd] for hp, h in heads]
    p1 = [_dot(jnp.concatenate([qk[i], rt[i]], axis=0), jnp.concatenate([bt[i], kt[i]], axis=0), trans_b=True)
          for i in range(n)]
    x_pow = [jnp.where(strict, -p[:c, :c], 0.0) for p in p1]
    t = [eye_f + xi for xi in x_pow]
    x_pow = [_dot(_lhs3(xi), _rhs3(xi)) for xi in x_pow]
    sq = 2
    while sq < c:
        w = [_rhs3(xi) for xi in x_pow]
        if 2 * sq < c:
            prod = [_dot(_lhs3(jnp.concatenate([t[i], x_pow[i]], axis=0)), w[i]) for i in range(n)]
            t = [t[i] + prod[i][:c] for i in range(n)]
            x_pow = [pr[c:] for pr in prod]
        else:
            t = [t[i] + _dot(_lhs3(t[i]), w[i]) for i in range(n)]
        sq *= 2
    avy = [_dot(jnp.concatenate([jnp.where(strict, p[:c, c:], 0.0), jnp.where(incl, p[c:, c:], 0.0)],
                                axis=0).astype(BF16), vv[i]) for i, p in enumerate(p1)]
    tu = []
    for i in range(n):
        rhs = jnp.concatenate([qk[i], avy[i][:c].astype(BF16)], axis=1)
        tu.append(_dot(jnp.concatenate(_split(t[i]), axis=1), jnp.concatenate([rhs, rhs], axis=0)).astype(BF16))
    bu = [_dot(jnp.where(incl, p[c:, :c], 0.0).astype(BF16), tu[i]) for i, p in enumerate(p1)]
    rh = [(rt[i].astype(F32) - bu[i][:, :hd]).astype(BF16) for i in range(n)]
    s_old = [s_ref[s_base + i] for i in range(n)]
    s_bf = [s.astype(BF16) for s in s_old]
    y = [avy[i][c:] - bu[i][:, hd:] + _dot(rh[i], s_bf[i], trans_b=True) for i in range(n)]
    for hp in range(n_hp):
        y_ref[hp] = jnp.concatenate(y[hp * per:(hp + 1) * per], axis=1)
    mt = [_dot(tu[i], bh[i], trans_a=True) for i in range(n)]
    hct = [_dot(vv[i], kh[i], trans_a=True) - mt[i][hd:] for i in range(n)]
    for i, (hp, h) in enumerate(heads):
        e_row = e_ref[hp:hp + 1, h * hd:(h + 1) * hd]
        s_ref[s_base + i] = s_old[i] * e_row + hct[i] - _dot(s_bf[i], mt[i][:hd].astype(BF16))


def _rwkv_scan_kernel(vf_ref, sf_ref, ef_ref, vb_ref, sb_ref, eb_ref, yf_ref, yb_ref, s_ref, *, n_hp):
    @pl.when(pl.program_id(1) == 0)
    def _():
        s_ref[...] = jnp.zeros_like(s_ref)

    n_heads = n_hp * (LANES // HEAD_DIM)
    _scan_direction(vf_ref, sf_ref, ef_ref, yf_ref, s_ref, 0, n_hp, upper=False)
    _scan_direction(vb_ref, sb_ref, eb_ref, yb_ref, s_ref, n_heads, n_hp, upper=True)


def _rwkv_scan(s_v, s_f, s_b, e_tot, cfg):
    n_hp, rows, _ = s_v.shape
    b, seq, ctx = cfg["batch"], cfg["seq"], cfg["ctx"]
    c = CHUNK
    ncc, ncl = ctx // c, seq // c
    ctx0 = b * seq // c

    def fwd_blk(bi, j):
        return jnp.where(j < ncc, ctx0 + bi * ncc + j, bi * ncl + (j - ncc))

    def bwd_blk(bi, j):
        return jnp.where(j < ncc, ctx0 + bi * ncc + (ncc - 1 - j), bi * ncl + (ncl - 1 - (j - ncc)))

    def specs(blk, d):
        return [pl.BlockSpec((n_hp, c, LANES), lambda bi, j: (0, blk(bi, j), 0)),
                pl.BlockSpec((N_STREAMS, n_hp, c, LANES), lambda bi, j: (0, 0, blk(bi, j), 0)),
                pl.BlockSpec((None, None, n_hp, LANES), lambda bi, j: (d, blk(bi, j), 0, 0))]

    out_f = pl.BlockSpec((n_hp, c, LANES), lambda bi, j: (0, fwd_blk(bi, j), 0))
    out_b = pl.BlockSpec((n_hp, c, LANES), lambda bi, j: (0, bwd_blk(bi, j), 0))
    y_shape = jax.ShapeDtypeStruct((n_hp, rows, LANES), F32)
    return pl.pallas_call(
        functools.partial(_rwkv_scan_kernel, n_hp=n_hp),
        grid=(b, ncc + ncl),
        in_specs=specs(fwd_blk, 0) + specs(bwd_blk, 1),
        out_specs=[out_f, out_b],
        out_shape=[y_shape, y_shape],
        scratch_shapes=[pltpu.VMEM((2 * n_hp * (LANES // HEAD_DIM), HEAD_DIM, HEAD_DIM), F32)],
        compiler_params=_cparams(("parallel", "arbitrary")),
        name="rwkv_scan",
    )(s_v, s_f, e_tot, s_v, s_b, e_tot)


def _readout_kernel(yf_ref, yb_ref, bonus_ref, gate_ref, gg_ref, gb_ref, e_ref, o_ref):
    n_hp = yf_ref.shape[0]
    y = jnp.concatenate([yf_ref[hp] + yb_ref[hp] for hp in range(n_hp)], axis=1)
    mu = _group_sum(y, e_ref) * (1.0 / HEAD_DIM)
    yc = y - mu
    var = _group_sum(yc * yc, e_ref) * (1.0 / HEAD_DIM)
    yn = yc * lax.rsqrt(var + GN_EPS) * gg_ref[...] + gb_ref[...]
    o_ref[...] = ((yn + bonus_ref[...]) * gate_ref[...]).astype(BF16)


def _readout(y_f, y_b, bonus, gate, gn_g, gn_b, e_mat, cfg):
    n_hp, rows, _ = y_f.shape
    bw, tm = cfg["b_width"], cfg["tm"]
    y_spec = pl.BlockSpec((n_hp, tm, LANES), lambda i: (0, i, 0))
    row_spec = pl.BlockSpec((tm, bw), lambda i: (i, 0))
    vec_spec = pl.BlockSpec((1, bw), lambda i: (0, 0))
    return pl.pallas_call(
        _readout_kernel,
        grid=(rows // tm,),
        in_specs=[y_spec, y_spec, row_spec, row_spec, vec_spec, vec_spec,
                  pl.BlockSpec(e_mat.shape, lambda i: (0, 0))],
        out_specs=row_spec,
        out_shape=jax.ShapeDtypeStruct((rows, bw), BF16),
        compiler_params=_cparams(("parallel",)),
        name="rwkv_readout",
    )(y_f, y_b, bonus, gate, gn_g, gn_b, e_mat)


def _outproj_kernel(oa_ref, ob_ref, oc_ref, w_ref, h_ref, g_ref, o_ref):
    mix = jnp.concatenate([oa_ref[...], ob_ref[...], oc_ref[...]], axis=1)
    o_ref[...] = h_ref[...] + g_ref[...] * _dot(mix, w_ref[...])


def _outproj(o_a, o_b, o_c, w, h, mods, l, cfg):
    rows, d = h.shape
    tm, tn = cfg["tm"], 2048
    modrow = cfg["modrow"]
    act =lambda a: pl.BlockSpec((tm, a.shape[1]), lambda i, j: (i, 0))
    return pl.pallas_call(
        _outproj_kernel,
        grid=(rows // tm, d // tn),
        in_specs=[act(o_a), act(o_b), act(o_c),
                  pl.BlockSpec((None, d, tn), lambda i, j: (l, 0, j)),
                  pl.BlockSpec((tm, tn), lambda i, j: (i, j)),
                  pl.BlockSpec((None, None, None, 1, tn), lambda i, j: (l, modrow(i), 2, 0, j))],
        out_specs=pl.BlockSpec((tm, tn), lambda i, j: (i, j)),
        out_shape=jax.ShapeDtypeStruct((rows, d), F32),
        compiler_params=_cparams(("parallel", "arbitrary")),
        name="out_proj",
    )(o_a, o_b, o_c, w, h, mods)


def _mlp_kernel(h_ref, g_ref, sc_ref, sh_ref, gate_ref, w1_ref, w2_ref, o_ref, u_scr, acc_scr):
    f = pl.program_id(1)

    @pl.when(f == 0)
    def _():
        u_scr[...] = _modulated_norm(h_ref[...], g_ref[...], sc_ref[...], sh_ref[...]).astype(BF16)

    a = jnp.maximum(_dot(u_scr[...], w1_ref[...]), 0.0)
    part = _dot((a * a).astype(BF16), w2_ref[...])

    @pl.when(f == 0)
    def _():
        acc_scr[...] = part

    @pl.when(f > 0)
    def _():
        acc_scr[...] += part

    @pl.when(f == pl.num_programs(1) - 1)
    def _():
        o_ref[...] = h_ref[...] + gate_ref[...] * acc_scr[...]


def _mlp(h, gain, mods, w1, w2, l, cfg):
    rows, d = h.shape
    dff = w1.shape[2]
    tm, tf = cfg["tm"], 1024
    modrow = cfg["modrow"]
    mod = lambda k: pl.BlockSpec((None, None, None, 1, d), lambda i, f: (l, modrow(i), k, 0, 0))
    return pl.pallas_call(
        _mlp_kernel,
        grid=(rows // tm, dff // tf),
        in_specs=[
            pl.BlockSpec((tm, d), lambda i, f: (i, 0)),
            pl.BlockSpec((None, 1, d), lambda i, f: (l, 0, 0)),
            mod(4), mod(3), mod(5),
            pl.BlockSpec((None, d, tf), lambda i, f: (l, 0, f)),
            pl.BlockSpec((None, tf, d), lambda i, f: (l, f, 0)),
        ],
        out_specs=pl.BlockSpec((tm, d), lambda i, f: (i, 0)),
        out_shape=jax.ShapeDtypeStruct((rows, d), F32),
        scratch_shapes=[pltpu.VMEM((tm, d), BF16), pltpu.VMEM((tm, d), F32)],
        compiler_params=_cparams(("parallel", "arbitrary")),
        name="mlp",
    )(h, gain, mods, mods, mods, w1, w2)


def _rope_tables(seq, tm):
    rows = seq // GRID_W
    row = jnp.broadcast_to(jnp.arange(rows)[:, None], (rows, GRID_W)).reshape(-1)
    col = jnp.broadcast_to(jnp.arange(GRID_W)[None, :], (rows, GRID_W)).reshape(-1)
    n_freq = HEAD_DIM // 4
    inv = ROPE_THETA ** (-jnp.arange(n_freq, dtype=F32) / n_freq)
    ang = jnp.concatenate([row[:, None].astype(F32) * inv, col[:, None].astype(F32) * inv], -1)
    cos, sin = jnp.cos(ang), jnp.sin(ang)
    reps = LANES // HEAD_DIM
    cos_t = jnp.tile(jnp.concatenate([cos, cos], -1), (1, reps))
    sin_t = jnp.tile(jnp.concatenate([-sin, sin], -1), (1, reps))
    cos_t = jnp.concatenate([cos_t, jnp.ones((tm, LANES), F32)], 0)
    sin_t = jnp.concatenate([sin_t, jnp.zeros((tm, LANES), F32)], 0)
    return cos_t, sin_t


def _block_ones(width):
    g = jnp.arange(width) // HEAD_DIM
    return (g[:, None] == g[None, :]).astype(BF16)


def kernel(x, c, ctx, c_ctx, ada_w, ada_b, norm1_g, norm2_g, w_in, a_q_norm, a_k_norm, a_sink, c_q_norm,
           c_k_norm, shift_mu, decay_w0, decay_up, iclr_a0, iclr_up, gate_up, k_k, k_a, r_k, gn_g, gn_b,
           w_out, mlp_w1, mlp_w2):
    batch, seq, d = x.shape
    n_ctx = ctx.shape[1]
    depth = ada_w.shape[0]
    bw = k_k.shape[1]
    lora_d, lora_i, lora_g = decay_up.shape[2], iclr_up.shape[2], gate_up.shape[1]
    a_heads = a_sink.shape[1]
    a_kv = a_heads // 4
    q_w, kv_w = a_heads * HEAD_DIM, a_kv * HEAD_DIM
    attn_w = q_w + 2 * kv_w
    lora_w = 4 * LORA_PAD + lora_g
    b_in = 3 * bw + 2 * lora_d + 2 * lora_i + lora_g
    assert lora_d <= LORA_PAD and lora_i <= LORA_PAD
    assert w_in.shape[2] == 2 * attn_w + b_in

    tm = batch * n_ctx
    n_lat_tiles = batch * seq // tm
    tiles_per_batch = seq // tm
    cfg = dict(
        batch=batch, seq=seq, ctx=n_ctx, tm=tm, tr=n_ctx, tq=n_ctx, tk=min(seq, 2048),
        b_width=bw, lora_w=lora_w, attn_w=attn_w, attn_col0=3 * bw + lora_w, qk_w=q_w + kv_w,
        attn_heads=a_heads, attn_kv=a_kv,
        modrow=lambda i: jnp.where(i < n_lat_tiles, i // tiles_per_batch, batch),
        rope_blk=lambda i: jnp.where(i < n_lat_tiles, i % tiles_per_batch, tiles_per_batch),
    )
    assert seq % tm == 0 and seq % GRID_W == 0 and n_ctx % CHUNK == 0 and (3 * bw) % lora_w == 0
    assert cfg["attn_col0"] % attn_w == 0 and batch + 1 <= 8

    def relayout_cols(m):
        a_part, b_part, c_part = m[..., :attn_w], m[..., attn_w:attn_w + b_in], m[..., attn_w + b_in:]
        pad = lambda z, n: jnp.pad(z, [(0, 0)] * (z.ndim - 1) + [(0, n - z.shape[-1])])
        o = 3 * bw
        pieces = [b_part[..., :o]]
        for width in (lora_d, lora_d, lora_i, lora_i):
            pieces.append(pad(b_part[..., o:o + width], LORA_PAD))
            o += width
        pieces.append(b_part[..., o:])
        return jnp.concatenate(pieces + [a_part, c_part], -1)

    w_in_r = relayout_cols(w_in).astype(BF16)
    mu_r = relayout_cols(jnp.pad(shift_mu, ((0, 0), (0, 0), (attn_w, attn_w))))[..., :3 * bw + lora_w]
    pad_rows = lambda z: jnp.pad(z, ((0, 0), (0, 0), (0, LORA_PAD - z.shape[2]), (0, 0)))
    w_up_r, a_up_r = pad_rows(decay_up), pad_rows(iclr_up)
    w_out_b = w_out.astype(BF16)
    w1_b, w2_b = mlp_w1.astype(BF16), mlp_w2.astype(BF16)

    scale = HEAD_DIM ** -0.5 * LOG2E
    tile = lambda g, n: jnp.tile(g, (1, n))

    def gains(qg, kg):
        return jnp.concatenate([tile(qg, a_heads) * scale, tile(kg, a_kv), jnp.ones((depth, kv_w), F32)], -1)

    qk_gains = jnp.stack([gains(a_q_norm, a_k_norm), gains(c_q_norm, c_k_norm)], 1)[:, :, None, :]

    cos_t, sin_t = _rope_tables(seq, tm)
    e_attn = _block_ones(attn_w)
    e_b = _block_ones(2 * LANES)

    cvec = jnp.zeros((8, d), F32).at[:batch].set(c).at[batch].set(c_ctx)
    mods = _mods(cvec, ada_w, ada_b).reshape(depth, 8, 6, 1, d)

    h = jnp.concatenate([x.reshape(batch * seq, d), ctx.reshape(batch * n_ctx, d)], 0)
    n1g, n2g = norm1_g[:, None, :], norm2_g[:, None, :]
    for l in range(depth):
        p = _inproj(h, n1g, mods, w_in_r, l, cfg)
        qkv = _qkprep(p, qk_gains[l], cos_t, sin_t, e_attn, cfg)
        o_a = _attention(qkv, 0, a_sink[l], cfg, window=True)
        o_c = _attention(qkv, 1, None, cfg, window=False)
        lw = dict(mu=mu_r[l], w0=decay_w0[l], w_up=w_up_r[l], a0=iclr_a0[l], a_up=a_up_r[l], g_up=gate_up[l],
                  k_k=k_k[l][None], k_a=k_a[l][None], r_k=r_k[l][None])
        s_v, s_f, s_b, e_tot, bonus, gate = _rwkv_prep(p, lw, e_b, cfg)
        y_f, y_b = _rwkv_scan(s_v, s_f, s_b, e_tot, cfg)
        o_b = _readout(y_f, y_b, bonus, gate, gn_g[l][None], gn_b[l][None], e_b, cfg)
        h = _outproj(o_a, o_b, o_c, w_out_b, h, mods, l, cfg)
        h = _mlp(h, n2g, mods, w1_b, w2_b, l, cfg)
    return h[:batch * seq].reshape(batch, seq, d)
```

```python
import functools

import jax
import jax.numpy as jnp
from jax import lax
from jax.experimental import pallas as pl
from jax.experimental.pallas import tpu as pltpu

F32 = jnp.float32
BF16 = jnp.bfloat16

HEAD_DIM = 64
GRID_W = 64
WINDOW = 128
ROPE_THETA = 10000.0
NORM_EPS = 1e-6
GN_EPS = 64e-5
NEG_INF = -1e30
LOG2E = 1.4426950408889634
DECAY_SCALE = 0.6065306597126334
LANES = 128
LORA_PAD = 128
CHUNK = 64
VMEM_LIMIT = 52 * 1024 * 1024


def _dot(a, b, trans_a=False, trans_b=False):
    dn = (((0 if trans_a else 1,), (1 if trans_b else 0,)), ((), ()))
    return lax.dot_general(a, b, dn, preferred_element_type=F32)


def _split(x):
    hi = x.astype(BF16)
    lo = (x - hi.astype(F32)).astype(BF16)
    return hi, lo


def _dot3(a, b, trans_a=False, trans_b=False):
    ah, al = _split(a)
    bh, bl = _split(b)
    kw = dict(trans_a=trans_a, trans_b=trans_b)
    return _dot(ah, bh, **kw) + (_dot(al, bh, **kw) + _dot(ah, bl, **kw))


def _dot1(a, b, trans_a=False, trans_b=False):
    return _dot(a.astype(BF16), b.astype(BF16), trans_a=trans_a, trans_b=trans_b)


def _group_sum(x, e_ref):
    gw = e_ref.shape[0]
    e = e_ref[...]
    outs = []
    for g in range(x.shape[1] // gw):
        hi, lo = _split(x[:, g * gw:(g + 1) * gw])
        outs.append(_dot(hi, e) + _dot(lo, e))
    return outs[0] if len(outs) == 1 else jnp.concatenate(outs, axis=1)


def _cparams(sem):
    return pltpu.CompilerParams(dimension_semantics=sem, vmem_limit_bytes=VMEM_LIMIT)


def _mods_kernel(c_ref, w_ref, b_ref, o_ref):
    c = c_ref[...]
    s = c * jax.nn.sigmoid(c)
    o_ref[0] = _dot3(s, w_ref[0]) + b_ref[0]


def _mods(cvec, ada_w, ada_b):
    depth, d, n = ada_w.shape
    tn = 512
    return pl.pallas_call(
        _mods_kernel,
        grid=(depth, n // tn),
        in_specs=[
            pl.BlockSpec((8, d), lambda l, j: (0, 0)),
            pl.BlockSpec((1, d, tn), lambda l, j: (l, 0, j)),
            pl.BlockSpec((1, 1, tn), lambda l, j: (l, 0, j)),
        ],
        out_specs=pl.BlockSpec((1, 8, tn), lambda l, j: (l, 0, j)),
        out_shape=jax.ShapeDtypeStruct((depth, 8, n), F32),
        compiler_params=_cparams(("parallel", "parallel")),
        name="adaln_mods",
    )(cvec, ada_w, ada_b.reshape(depth, 1, n))


def _modulated_norm(x, g, sc, sh):
    ms = jnp.mean(x * x, axis=-1, keepdims=True)
    return (x * lax.rsqrt(ms + NORM_EPS) * g) * (1.0 + sc) + sh


def _inproj_kernel(h_ref, g_ref, sc_ref, sh_ref, w_ref, o_ref, u_scr):
    @pl.when(pl.program_id(1) == 0)
    def _():
        u_scr[...] = _modulated_norm(h_ref[...], g_ref[...], sc_ref[...], sh_ref[...]).astype(BF16)

    o_ref[...] = _dot(u_scr[...], w_ref[...])


def _inproj(h, gain, mods, w, l, cfg):
    rows, d = h.shape
    n = w.shape[2]
    tm, tn = cfg["tm"], 1792
    modrow = cfg["modrow"]
    return pl.pallas_call(
        _inproj_kernel,
        grid=(rows // tm, n // tn),
        in_specs=[
            pl.BlockSpec((tm, d), lambda i, j: (i, 0)),
            pl.BlockSpec((None, 1, d), lambda i, j: (l, 0, 0)),
            pl.BlockSpec((None, None, None, 1, d), lambda i, j: (l, modrow(i), 1, 0, 0)),
            pl.BlockSpec((None, None, None, 1, d), lambda i, j: (l, modrow(i), 0, 0, 0)),
            pl.BlockSpec((None, d, tn), lambda i, j: (l, 0, j)),
        ],
        out_specs=pl.BlockSpec((tm, tn), lambda i, j: (i, j)),
        out_shape=jax.ShapeDtypeStruct((rows, n), F32),
        scratch_shapes=[pltpu.VMEM((tm, d), BF16)],
        compiler_params=_cparams(("parallel", "arbitrary")),
        name="in_proj",
    )(h, gain, mods, mods, w)


def _qkprep_kernel(p_ref, gain_ref, cos_ref, sin_ref, e_ref, o_ref, *, qk_w):
    x = p_ref[...]
    width = x.shape[1]
    ss = _group_sum(x * x, e_ref)
    y = x * lax.rsqrt(ss * (1.0 / HEAD_DIM) + NORM_EPS) * gain_ref[0]
    cos = cos_ref[...]
    sin = sin_ref[...]
    lane = lax.broadcasted_iota(jnp.int32, (x.shape[0], LANES), 1)
    first_half = (lane & (HEAD_DIM // 2)) == 0
    for g in range(width // LANES):
        sl = slice(g * LANES, (g + 1) * LANES)
        if g * LANES < qk_w:
            yg = y[:, sl]
            partner = jnp.where(first_half, pltpu.roll(yg, LANES - HEAD_DIM // 2, 1),
                                pltpu.roll(yg, HEAD_DIM // 2, 1))
            o_ref[0, :, sl] = (yg * cos + partner * sin).astype(BF16)
        else:
            o_ref[0, :, sl] = x[:, sl].astype(BF16)


def _qkprep(p, gains, cos_t, sin_t, e_mat, cfg):
    rows = p.shape[0]
    tm, aw = cfg["tm"], cfg["attn_w"]
    first_blk = cfg["attn_col0"] // aw
    rope_blk = cfg["rope_blk"]
    return pl.pallas_call(
        functools.partial(_qkprep_kernel, qk_w=cfg["qk_w"]),
        grid=(rows // tm, 2),
        in_specs=[
            pl.BlockSpec((tm, aw), lambda i, s: (i, first_blk + s)),
            pl.BlockSpec((1, 1, aw), lambda i, s: (s, 0, 0)),
            pl.BlockSpec((tm, LANES), lambda i, s: (rope_blk(i), 0)),
            pl.BlockSpec((tm, LANES), lambda i, s: (rope_blk(i), 0)),
            pl.BlockSpec(e_mat.shape, lambda i, s: (0, 0)),
        ],
        out_specs=pl.BlockSpec((1, tm, aw), lambda i, s: (s, i, 0)),
        out_shape=jax.ShapeDtypeStruct((2, rows, aw), BF16),
        compiler_params=_cparams(("parallel", "parallel")),
        name="qk_prep",
    )(p, gains, cos_t, sin_t, e_mat)


def _attn_kernel(sink_ref, q_ref, kc_ref, vc_ref, kl_ref, vl_ref, o_ref, *,
                 window, n_heads, group, tq, tk, seq):
    i = pl.program_id(1)
    is_lat = i < seq // tq
    hd = HEAD_DIM
    n_kv = n_heads // group
    rows = group * tq
    den_lane = [((j + 1) % n_kv) * hd for j in range(n_kv)]
    if window:
        span = tq + 2 * WINDOW
        start = pl.multiple_of(jnp.clip(i * tq - WINDOW, 0, seq - span), WINDOW)
        delta = (lax.broadcasted_iota(jnp.int32, (tq, span), 0) - lax.broadcasted_iota(jnp.int32, (tq, span), 1)
                 + (i * tq - start))
        bias = jnp.where(jnp.abs(delta) <= WINDOW, 0.0, NEG_INF)
        n_iter = jnp.where(is_lat, 1, 0)
    else:
        span = tk
        n_iter = jnp.where(is_lat, seq // tk, 0)

    def with_ones(vblk, j):
        lane = lax.broadcasted_iota(jnp.int32, vblk.shape, 1)
        ones_col = jnp.where(lane == den_lane[j], 1.0, 0.0).astype(BF16)
        return jnp.where((lane >= j * hd) & (lane < (j + 1) * hd), vblk, ones_col)

    zeros = jnp.zeros((tq, hd), BF16)
    acc_lane = lax.broadcasted_iota(jnp.int32, (rows, LANES), 1)
    for j in range(n_kv):
        qs = []
        for g in range(group):
            h = j * group + g
            parts = [zeros] * n_kv
            parts[j] = q_ref[0, :, h * hd:(h + 1) * hd]
            qs.append(jnp.concatenate(parts, axis=1))
        q = jnp.concatenate(qs, axis=0)
        if sink_ref is not None:
            m0 = jnp.concatenate(
                [jnp.full((tq, 1), sink_ref[j * group + g] * LOG2E, F32) for g in range(group)], axis=0)
            acc0 = jnp.where(acc_lane == den_lane[j], 1.0, 0.0)
        else:
            m0 = jnp.full((rows, 1), NEG_INF, F32)
            acc0 = jnp.zeros((rows, LANES), F32)

        s = _dot(q, kc_ref[0], trans_b=True)
        m = jnp.maximum(m0, jnp.max(s, axis=-1, keepdims=True))
        p = jnp.exp2(s - m).astype(BF16)
        acc = jnp.exp2(m0 - m) * acc0 + _dot(p, with_ones(vc_ref[0], j))

        def body(kb, carry, q=q, j=j):
            m, acc = carry
            off = start if window else pl.multiple_of(kb * tk, tk)
            s = _dot(q, kl_ref[0, pl.ds(off, span), :], trans_b=True)
            if window:
                s = (s.reshape(group, tq, span) + bias[None]).reshape(rows, span)
            m_new = jnp.maximum(m, jnp.max(s, axis=-1, keepdims=True))
            p = jnp.exp2(s - m_new).astype(BF16)
            acc_new = jnp.exp2(m - m_new) * acc + _dot(p, with_ones(vl_ref[0, pl.ds(off, span), :], j))
            return m_new, acc_new

        m, acc = lax.fori_loop(0, n_iter, body, (m, acc))
        o = acc[:, j * hd:(j + 1) * hd] / acc[:, den_lane[j]:den_lane[j] + 1]
        for g in range(group):
            c0 = (j * group + g) * hd
            o_ref[:, c0:c0 + hd] = o[g * tq:(g + 1) * tq].astype(BF16)


def _attention(qkv, sec, sink, cfg, window):
    rows = qkv.shape[1]
    b, seq, ctx, tq = cfg["batch"], cfg["seq"], cfg["ctx"], cfg["tq"]
    n_heads, n_kv = cfg["attn_heads"], cfg["attn_kv"]
    q_w = n_heads * HEAD_DIM
    kv_w = n_kv * HEAD_DIM
    assert kv_w == LANES and q_w % kv_w == 0 and ctx == tq
    n_lat_tiles = seq // tq
    k_blk = q_w // kv_w
    ctx_blk0 = b * seq // ctx

    def q_map(bi, i, *_):
        return (sec, jnp.where(i < n_lat_tiles, bi * n_lat_tiles + i, ctx_blk0 + bi), 0)

    def o_map(bi, i, *_):
        return (jnp.where(i < n_lat_tiles, bi * n_lat_tiles + i, ctx_blk0 + bi), 0)

    tk = cfg["tk"]
    assert seq % tk == 0 and tq % WINDOW == 0 and tq + 2 * WINDOW <= seq
    kernel = functools.partial(_attn_kernel, window=window, n_heads=n_heads, group=n_heads // n_kv,
                               tq=tq, tk=tk, seq=seq)
    in_specs = [
        pl.BlockSpec((1, tq, q_w), q_map),
        pl.BlockSpec((1, ctx, kv_w), lambda bi, i, *_: (sec, ctx_blk0 + bi, k_blk)),
        pl.BlockSpec((1, ctx, kv_w), lambda bi, i, *_: (sec, ctx_blk0 + bi, k_blk + 1)),
        pl.BlockSpec((1, seq, kv_w), lambda bi, i, *_: (sec, bi, k_blk)),
        pl.BlockSpec((1, seq, kv_w), lambda bi, i, *_: (sec, bi, k_blk + 1)),
    ]
    args = [qkv, qkv, qkv, qkv, qkv]
    if sink is not None:
        in_specs = [pl.BlockSpec(memory_space=pltpu.SMEM)] + in_specs
        args = [sink] + args
    else:
        kernel = functools.partial(kernel, None)
    return pl.pallas_call(
        kernel,
        grid=(b, n_lat_tiles + 1),
        in_specs=in_specs,
        out_specs=pl.BlockSpec((tq, q_w), o_map),
        out_shape=jax.ShapeDtypeStruct((rows, q_w), BF16),
        compiler_params=_cparams(("parallel", "arbitrary")),
        name="window_attn" if window else "global_attn",
    )(*args)


def _tri3(tri, x):
    h1 = x.astype(BF16)
    r1 = x - h1.astype(F32)
    h2 = r1.astype(BF16)
    h3 = (r1 - h2.astype(F32)).astype(BF16)
    return _dot(tri, h1) + (_dot(tri, h2) + _dot(tri, h3))


def _rwkv_prep_kernel(cur_ref, prev_ref, next_ref, mu_ref, w0_ref, wup_ref, a0_ref, aup_ref, gup_ref,
                      kk_ref, ka_ref, rk_ref, e_ref, tril_ref, triu_ref,
                      sv_ref, sf_ref, sb_ref, etot_ref, bonus_ref, gate_ref, *,
                      bw, seg_lat, seg_ctx, n_lat_tiles):
    i = pl.program_id(0)
    p = cur_ref[...]
    tr = p.shape[0]
    lat = i < n_lat_tiles
    seg = jnp.where(lat, seg_lat, seg_ctx)
    pos = jnp.where(lat, i, i - n_lat_tiles) % seg
    has_prev = (pos != 0).astype(F32)
    has_next = (pos != seg - 1).astype(F32)
    row = lax.broadcasted_iota(jnp.int32, (tr, 1), 0)
    prev_row = prev_ref[7:8, :] * has_prev
    next_row = next_ref[0:1, :] * has_next
    prv = jnp.where(row == 0, prev_row, pltpu.roll(p, 1, 0))
    nxt = jnp.where(row == tr - 1, next_row, pltpu.roll(p, tr - 1, 0))
    xs = p + mu_ref[0:1, :] * (prv - p) + mu_ref[1:2, :] * (nxt - p)

    r = xs[:, 0:bw]
    k = xs[:, bw:2 * bw]
    v = xs[:, 2 * bw:3 * bw]
    lora = xs[:, 3 * bw:]
    n_hp = bw // LANES

    kk = k * kk_ref[...]
    kk = kk * lax.rsqrt(jnp.maximum(_group_sum(kk * kk, e_ref), 1e-24))
    ksum = jnp.zeros_like(k)
    for d, (s_ref, tri_ref) in enumerate(((sf_ref, tril_ref), (sb_ref, triu_ref))):
        wd = lora[:, d * LORA_PAD:(d + 1) * LORA_PAD]
        ad = lora[:, (2 + d) * LORA_PAD:(3 + d) * LORA_PAD]
        log_decay = -DECAY_SCALE * jax.nn.sigmoid(w0_ref[d:d + 1, :] + _dot3(jnp.tanh(wd), wup_ref[d]))
        a = jax.nn.sigmoid(a0_ref[d:d + 1, :] + _dot3(ad, aup_ref[d]))
        key = k * (1.0 + (a - 1.0) * ka_ref[...])
        ksum = ksum + key
        kka = kk * a
        cum = _tri3(tri_ref[...], log_decay)
        last = CHUNK - 1 if d == 0 else 0
        tot_rows = [cum[c * CHUNK + last:c * CHUNK + last + 1] for c in range(tr // CHUNK)]
        tot = jnp.concatenate([jnp.broadcast_to(t, (CHUNK, bw)) for t in tot_rows], axis=0)
        e_inv = jnp.exp(-cum)
        e_rel = jnp.exp(tot - cum)
        streams = (kk * jnp.exp(cum - log_decay), r * jnp.exp(cum), kka * e_inv, key * e_inv,
                   kka * e_rel, key * e_rel)
        for n, st in enumerate(streams):
            st = st.astype(BF16)
            for hp in range(n_hp):
                s_ref[n, hp] = st[:, hp * LANES:(hp + 1) * LANES]
        for cidx in range(tr // CHUNK):
            e_tot = jnp.exp(tot_rows[cidx])
            etot_ref[d, cidx] = jnp.concatenate(
                [e_tot[:, hp * LANES:(hp + 1) * LANES] for hp in range(n_hp)], axis=0)
    vb = v.astype(BF16)
    for hp in range(n_hp):
        sv_ref[hp] = vb[:, hp * LANES:(hp + 1) * LANES]
    bonus_ref[...] = _group_sum(r * ksum * rk_ref[...], e_ref) * v
    gate_ref[...] = _dot3(jax.nn.sigmoid(lora[:, 4 * LORA_PAD:]), gup_ref[...])


def _chunk_block_diag(tr, kind):
    t = jnp.arange(tr)
    same = (t[:, None] // CHUNK) == (t[None, :] // CHUNK)
    if kind == "lower":
        same = same & (t[None, :] <= t[:, None])
    elif kind == "upper":
        same = same & (t[None, :] >= t[:, None])
    return same.astype(BF16)


N_STREAMS = 6


def _rwkv_prep(p, lw, e_mat, cfg):
    rows = p.shape[0]
    bw, tr = cfg["b_width"], cfg["tr"]
    cw = 3 * bw + cfg["lora_w"]
    n_hp = bw // LANES
    n_lat_tiles = cfg["batch"] * cfg["seq"] // tr
    hb = tr // 8
    last8 = rows // 8 - 1
    full = lambda a: pl.BlockSpec(a.shape, lambda i: (0,) * a.ndim)
    consts = [lw["mu"], lw["w0"], lw["w_up"], lw["a0"], lw["a_up"], lw["g_up"], lw["k_k"], lw["k_a"], lw["r_k"],
              e_mat, _chunk_block_diag(tr, "lower"), _chunk_block_diag(tr, "upper")]
    stream_spec = pl.BlockSpec((N_STREAMS, n_hp, tr, LANES), lambda i: (0, 0, i, 0))
    stream_shape = jax.ShapeDtypeStruct((N_STREAMS, n_hp, rows, LANES), BF16)
    cpt = tr // CHUNK
    kernel = functools.partial(_rwkv_prep_kernel, bw=bw, seg_lat=cfg["seq"] // tr, seg_ctx=cfg["ctx"] // tr,
                               n_lat_tiles=n_lat_tiles)
    return pl.pallas_call(
        kernel,
        grid=(rows // tr,),
        in_specs=[
            pl.BlockSpec((tr, cw), lambda i: (i, 0)),
            pl.BlockSpec((8, cw), lambda i: (jnp.maximum(i * hb - 1, 0), 0)),
            pl.BlockSpec((8, cw), lambda i: (jnp.minimum((i + 1) * hb, last8), 0)),
        ] + [full(a) for a in consts],
        out_specs=[pl.BlockSpec((n_hp, tr, LANES), lambda i: (0, i, 0)),
                   stream_spec, stream_spec,
                   pl.BlockSpec((2, cpt, n_hp, LANES), lambda i: (0, i, 0, 0)),
                   pl.BlockSpec((tr, bw), lambda i: (i, 0)),
                   pl.BlockSpec((tr, bw), lambda i: (i, 0))],
        out_shape=[jax.ShapeDtypeStruct((n_hp, rows, LANES), BF16),
                   stream_shape, stream_shape,
                   jax.ShapeDtypeStruct((2, rows // CHUNK, n_hp, LANES), F32),
                   jax.ShapeDtypeStruct((rows, bw), F32),
                   jax.ShapeDtypeStruct((rows, bw), F32)],
        compiler_params=_cparams(("parallel",)),
        name="rwkv_prep",
    )(p, p, p, *consts)


def _lhs3(a):
    hi, lo = _split(a)
    return jnp.concatenate([hi, lo, hi], axis=1)


def _rhs3(b):
    hi, lo = _split(b)
    return jnp.concatenate([hi, hi, lo], axis=0)


def _block_diag(x):
    lo = lax.broadcasted_iota(jnp.int32, x.shape, 1) < HEAD_DIM
    zero = jnp.zeros_like(x)
    return jnp.concatenate([jnp.where(lo, x, zero), jnp.where(lo, zero, x)], axis=0)


def _scan_direction(v_ref, st_ref, e_ref, ones_ref, y_ref, z_ref, z_base, n_hp, upper):
    c, hd = CHUNK, HEAD_DIM
    assert c == hd and LANES == 2 * hd
    row = lax.broadcasted_iota(jnp.int32, (c, LANES), 0)
    lane = lax.broadcasted_iota(jnp.int32, (c, LANES), 1)
    col = lane & (hd - 1)
    lo = lane < hd
    incl, strict = (col >= row, col > row) if upper else (col <= row, col < row)
    eye = col == row
    eye_f = jnp.where(eye, 1.0, 0.0)
    pairs = range(n_hp)

    def half(x, h):
        keep = lo if h == 0 else ~lo
        return jnp.where(keep, x, jnp.zeros_like(x))

    qk, rt, bt, kt, bh, kh = ([st_ref[k, hp] for hp in pairs] for k in range(N_STREAMS))
    vv = [v_ref[hp] for hp in pairs]
    p1 = [_dot(jnp.concatenate([qk[i], rt[i]], axis=0),
               jnp.concatenate([half(bt[i], 0), half(bt[i], 1), half(kt[i], 0), half(kt[i], 1)], axis=0),
               trans_b=True) for i in pairs]
    x_pow = [jnp.where(strict, -p[:c, :LANES], 0.0) for p in p1]
    t = [eye_f + x for x in x_pow]
    x_pow = [_dot(x.astype(BF16), _block_diag(x.astype(BF16))) for x in x_pow]
    sq = 2
    while sq < c:
        w = [_block_diag(x.astype(BF16)) for x in x_pow]
        if 2 * sq < c:
            prod = [_dot(jnp.concatenate([t[i], x_pow[i]], axis=0).astype(BF16), w[i]) for i in pairs]
            t = [t[i] + prod[i][:c] for i in pairs]
            x_pow = [pr[c:] for pr in prod]
        else:
            t = [t[i] + _dot(t[i].astype(BF16), w[i]) for i in pairs]
        sq *= 2
    avy = [_dot(jnp.concatenate([jnp.where(strict, p[:c, LANES:], 0.0), jnp.where(incl, p[c:, LANES:], 0.0)],
                                axis=0).astype(BF16), _block_diag(vv[i])) for i, p in enumerate(p1)]
    tu = [_dot(t[i].astype(BF16),
               jnp.concatenate([_block_diag(qk[i]), _block_diag(avy[i][:c].astype(BF16))], axis=1)).astype(BF16)
          for i in pairs]
    bu = [_dot(jnp.where(incl, p[c:, :LANES], 0.0).astype(BF16),
               jnp.concatenate([_block_diag(tu[i][:, :LANES]), _block_diag(tu[i][:, LANES:])], axis=1))
          for i, p in enumerate(p1)]
    rh = [(rt[i].astype(F32) - bu[i][:, :LANES]).astype(BF16) for i in pairs]
    yh = [avy[i][c:] - bu[i][:, LANES:] for i in pairs]
    rp = [_dot(bh[i], tu[i], trans_a=True) for i in pairs]
    kv = [_dot(kh[i], vv[i], trans_a=True) for i in pairs]
    g = [jnp.where(lo, r[:hd, :LANES], r[hd:, :LANES]).astype(BF16) for r in rp]
    hc = [jnp.where(lo, kv[i][:hd], kv[i][hd:]) - jnp.where(lo, rp[i][:hd, LANES:], rp[i][hd:, LANES:])
          for i in pairs]
    ones_bd = ones_ref[...]
    for i in pairs:
        e_diag = jnp.where(eye, jnp.broadcast_to(e_ref[i:i + 1, :], (c, LANES)), 0.0)
        e_col = _dot(jnp.concatenate(_split(e_diag), axis=1), jnp.concatenate([ones_bd, ones_bd], axis=0))
        z = z_ref[z_base + i]
        yz = _dot(jnp.concatenate([rh[i], g[i]], axis=0), _block_diag(z.astype(BF16)))
        y_ref[i] = yh[i] + yz[:c]
        z_ref[z_base + i] = e_col * z - yz[c:] + hc[i]


def _rwkv_scan_kernel(vf_ref, sf_ref, ef_ref, vb_ref, sb_ref, eb_ref, ones_ref, yf_ref, yb_ref, z_ref, *, n_hp):
    @pl.when(pl.program_id(1) == 0)
    def _():
        z_ref[...] = jnp.zeros_like(z_ref)

    _scan_direction(vf_ref, sf_ref, ef_ref, ones_ref, yf_ref, z_ref, 0, n_hp, upper=False)
    _scan_direction(vb_ref, sb_ref, eb_ref, ones_ref, yb_ref, z_ref, n_hp, n_hp, upper=True)


def _rwkv_scan(s_v, s_f, s_b, e_tot, cfg):
    n_hp, rows, _ = s_v.shape
    b, seq, ctx = cfg["batch"], cfg["seq"], cfg["ctx"]
    c = CHUNK
    ncc, ncl = ctx // c, seq // c
    ctx0 = b * seq // c

    def fwd_blk(bi, j):
        return jnp.where(j < ncc, ctx0 + bi * ncc + j, bi * ncl + (j - ncc))

    def bwd_blk(bi, j):
        return jnp.where(j < ncc, ctx0 + bi * ncc + (ncc - 1 - j), bi * ncl + (ncl - 1 - (j - ncc)))

    def specs(blk, d):
        return [pl.BlockSpec((n_hp, c, LANES), lambda bi, j: (0, blk(bi, j), 0)),
                pl.BlockSpec((N_STREAMS, n_hp, c, LANES), lambda bi, j: (0, 0, blk(bi, j), 0)),
                pl.BlockSpec((None, None, n_hp, LANES), lambda bi, j: (d, blk(bi, j), 0, 0))]

    out_f = pl.BlockSpec((n_hp, c, LANES), lambda bi, j: (0, fwd_blk(bi, j), 0))
    out_b = pl.BlockSpec((n_hp, c, LANES), lambda bi, j: (0, bwd_blk(bi, j), 0))
    y_shape = jax.ShapeDtypeStruct((n_hp, rows, LANES), F32)
    return pl.pallas_call(
        functools.partial(_rwkv_scan_kernel, n_hp=n_hp),
        grid=(b, ncc + ncl),
        in_specs=specs(fwd_blk, 0) + specs(bwd_blk, 1) + [pl.BlockSpec((LANES, LANES), lambda bi, j: (0, 0))],
        out_specs=[out_f, out_b],
        out_shape=[y_shape, y_shape],
        scratch_shapes=[pltpu.VMEM((2 * n_hp, HEAD_DIM, LANES), F32)],
        compiler_params=_cparams(("parallel", "arbitrary")),
        name="rwkv_scan",
    )(s_v, s_f, e_tot, s_v, s_b, e_tot, _block_ones(LANES))


def _readout_kernel(yf_ref, yb_ref, bonus_ref, gate_ref, gg_ref, gb_ref, e_ref, o_ref):
    n_hp = yf_ref.shape[0]
    y = jnp.concatenate([yf_ref[hp] + yb_ref[hp] for hp in range(n_hp)], axis=1)
    mu = _group_sum(y, e_ref) * (1.0 / HEAD_DIM)
    yc = y - mu
    var = _group_sum(yc * yc, e_ref) * (1.0 / HEAD_DIM)
    yn = yc * lax.rsqrt(var + GN_EPS) * gg_ref[...] + gb_ref[...]
    o_ref[...] = ((yn + bonus_ref[...]) * gate_ref[...]).astype(BF16)


def _readout(y_f, y_b, bonus, gate, gn_g, gn_b, e_mat, cfg):
    n_hp, rows, _ = y_f.shape
    bw, tm = cfg["b_width"], cfg["tm"]
    y_spec = pl.BlockSpec((n_hp, tm, LANES), lambda i: (0, i, 0))
    row_spec = pl.BlockSpec((tm, bw), lambda i: (i, 0))
    vec_spec = pl.BlockSpec((1, bw), lambda i: (0, 0))
    return pl.pallas_call(
        _readout_kernel,
        grid=(rows // tm,),
        in_specs=[y_spec, y_spec, row_spec, row_spec, vec_spec, vec_spec,
                  pl.BlockSpec(e_mat.shape, lambda i: (0, 0))],
        out_specs=row_spec,
        out_shape=jax.ShapeDtypeStruct((rows, bw), BF16),
        compiler_params=_cparams(("parallel",)),
        name="rwkv_readout",
    )(y_f, y_b, bonus, gate, gn_g, gn_b, e_mat)


def _outproj_kernel(oa_ref, ob_ref, oc_ref, w_ref, h_ref, g_ref, o_ref):
    mix = jnp.concatenate([oa_ref[...], ob_ref[...], oc_ref[...]], axis=1)
    o_ref[...] = h_ref[...] + g_ref[...] * _dot(mix, w_ref[...])


def _outproj(o_a, o_b, o_c, w, h, mods, l, cfg):
    rows, d = h.shape
    tm, tn = cfg["tm"], 2048
    modrow = cfg["modrow"]
    act =lambda a: pl.BlockSpec((tm, a.shape[1]), lambda i, j: (i, 0))
    return pl.pallas_call(
        _outproj_kernel,
        grid=(rows // tm, d // tn),
        in_specs=[act(o_a), act(o_b), act(o_c),
                  pl.BlockSpec((None, d, tn), lambda i, j: (l, 0, j)),
                  pl.BlockSpec((tm, tn), lambda i, j: (i, j)),
                  pl.BlockSpec((None, None, None, 1, tn), lambda i, j: (l, modrow(i), 2, 0, j))],
        out_specs=pl.BlockSpec((tm, tn), lambda i, j: (i, j)),
        out_shape=jax.ShapeDtypeStruct((rows, d), F32),
        compiler_params=_cparams(("parallel", "arbitrary")),
        name="out_proj",
    )(o_a, o_b, o_c, w, h, mods)


def _mlp_kernel(h_ref, g_ref, sc_ref, sh_ref, gate_ref, w1_ref, w2_ref, o_ref, u_scr, acc_scr):
    f = pl.program_id(1)

    @pl.when(f == 0)
    def _():
        u_scr[...] = _modulated_norm(h_ref[...], g_ref[...], sc_ref[...], sh_ref[...]).astype(BF16)

    a = jnp.maximum(_dot(u_scr[...], w1_ref[...]), 0.0)
    part = _dot((a * a).astype(BF16), w2_ref[...])

    @pl.when(f == 0)
    def _():
        acc_scr[...] = part

    @pl.when(f > 0)
    def _():
        acc_scr[...] += part

    @pl.when(f == pl.num_programs(1) - 1)
    def _():
        o_ref[...] = h_ref[...] + gate_ref[...] * acc_scr[...]


def _mlp(h, gain, mods, w1, w2, l, cfg):
    rows, d = h.shape
    dff = w1.shape[2]
    tm, tf = cfg["tm"], 1024
    modrow = cfg["modrow"]
    mod = lambda k: pl.BlockSpec((None, None, None, 1, d), lambda i, f: (l, modrow(i), k, 0, 0))
    return pl.pallas_call(
        _mlp_kernel,
        grid=(rows // tm, dff // tf),
        in_specs=[
            pl.BlockSpec((tm, d), lambda i, f: (i, 0)),
            pl.BlockSpec((None, 1, d), lambda i, f: (l, 0, 0)),
            mod(4), mod(3), mod(5),
            pl.BlockSpec((None, d, tf), lambda i, f: (l, 0, f)),
            pl.BlockSpec((None, tf, d), lambda i, f: (l, f, 0)),
        ],
        out_specs=pl.BlockSpec((tm, d), lambda i, f: (i, 0)),
        out_shape=jax.ShapeDtypeStruct((rows, d), F32),
        scratch_shapes=[pltpu.VMEM((tm, d), BF16), pltpu.VMEM((tm, d), F32)],
        compiler_params=_cparams(("parallel", "arbitrary")),
        name="mlp",
    )(h, gain, mods, mods, mods, w1, w2)


def _rope_tables(seq, tm):
    rows = seq // GRID_W
    row = jnp.broadcast_to(jnp.arange(rows)[:, None], (rows, GRID_W)).reshape(-1)
    col = jnp.broadcast_to(jnp.arange(GRID_W)[None, :], (rows, GRID_W)).reshape(-1)
    n_freq = HEAD_DIM // 4
    inv = ROPE_THETA ** (-jnp.arange(n_freq, dtype=F32) / n_freq)
    ang = jnp.concatenate([row[:, None].astype(F32) * inv, col[:, None].astype(F32) * inv], -1)
    cos, sin = jnp.cos(ang), jnp.sin(ang)
    reps = LANES // HEAD_DIM
    cos_t = jnp.tile(jnp.concatenate([cos, cos], -1), (1, reps))
    sin_t = jnp.tile(jnp.concatenate([-sin, sin], -1), (1, reps))
    cos_t = jnp.concatenate([cos_t, jnp.ones((tm, LANES), F32)], 0)
    sin_t = jnp.concatenate([sin_t, jnp.zeros((tm, LANES), F32)], 0)
    return cos_t, sin_t


def _block_ones(width):
    g = jnp.arange(width) // HEAD_DIM
    return (g[:, None] == g[None, :]).astype(BF16)


def kernel(x, c, ctx, c_ctx, ada_w, ada_b, norm1_g, norm2_g, w_in, a_q_norm, a_k_norm, a_sink, c_q_norm,
           c_k_norm, shift_mu, decay_w0, decay_up, iclr_a0, iclr_up, gate_up, k_k, k_a, r_k, gn_g, gn_b,
           w_out, mlp_w1, mlp_w2):
    batch, seq, d = x.shape
    n_ctx = ctx.shape[1]
    depth = ada_w.shape[0]
    bw = k_k.shape[1]
    lora_d, lora_i, lora_g = decay_up.shape[2], iclr_up.shape[2], gate_up.shape[1]
    a_heads = a_sink.shape[1]
    a_kv = a_heads // 4
    q_w, kv_w = a_heads * HEAD_DIM, a_kv * HEAD_DIM
    attn_w = q_w + 2 * kv_w
    lora_w = 4 * LORA_PAD + lora_g
    b_in = 3 * bw + 2 * lora_d + 2 * lora_i + lora_g
    assert lora_d <= LORA_PAD and lora_i <= LORA_PAD
    assert w_in.shape[2] == 2 * attn_w + b_in

    tm = batch * n_ctx
    n_lat_tiles = batch * seq // tm
    tiles_per_batch = seq // tm
    cfg = dict(
        batch=batch, seq=seq, ctx=n_ctx, tm=tm, tr=n_ctx, tq=n_ctx, tk=min(seq, 2048),
        b_width=bw, lora_w=lora_w, attn_w=attn_w, attn_col0=3 * bw + lora_w, qk_w=q_w + kv_w,
        attn_heads=a_heads, attn_kv=a_kv,
        modrow=lambda i: jnp.where(i < n_lat_tiles, i // tiles_per_batch, batch),
        rope_blk=lambda i: jnp.where(i < n_lat_tiles, i % tiles_per_batch, tiles_per_batch),
    )
    assert seq % tm == 0 and seq % GRID_W == 0 and n_ctx % CHUNK == 0 and (3 * bw) % lora_w == 0
    assert cfg["attn_col0"] % attn_w == 0 and batch + 1 <= 8

    def relayout_cols(m):
        a_part, b_part, c_part = m[..., :attn_w], m[..., attn_w:attn_w + b_in], m[..., attn_w + b_in:]
        pad = lambda z, n: jnp.pad(z, [(0, 0)] * (z.ndim - 1) + [(0, n - z.shape[-1])])
        o = 3 * bw
        pieces = [b_part[..., :o]]
        for width in (lora_d, lora_d, lora_i, lora_i):
            pieces.append(pad(b_part[..., o:o + width], LORA_PAD))
            o += width
        pieces.append(b_part[..., o:])
        return jnp.concatenate(pieces + [a_part, c_part], -1)

    w_in_r = relayout_cols(w_in).astype(BF16)
    mu_r = relayout_cols(jnp.pad(shift_mu, ((0, 0), (0, 0), (attn_w, attn_w))))[..., :3 * bw + lora_w]
    pad_rows = lambda z: jnp.pad(z, ((0, 0), (0, 0), (0, LORA_PAD - z.shape[2]), (0, 0)))
    w_up_r, a_up_r = pad_rows(decay_up), pad_rows(iclr_up)
    w_out_b = w_out.astype(BF16)
    w1_b, w2_b = mlp_w1.astype(BF16), mlp_w2.astype(BF16)

    scale = HEAD_DIM ** -0.5 * LOG2E
    tile = lambda g, n: jnp.tile(g, (1, n))

    def gains(qg, kg):
        return jnp.concatenate([tile(qg, a_heads) * scale, tile(kg, a_kv), jnp.ones((depth, kv_w), F32)], -1)

    qk_gains = jnp.stack([gains(a_q_norm, a_k_norm), gains(c_q_norm, c_k_norm)], 1)[:, :, None, :]

    cos_t, sin_t = _rope_tables(seq, tm)
    e_attn = _block_ones(attn_w)
    e_b = _block_ones(2 * LANES)

    cvec = jnp.zeros((8, d), F32).at[:batch].set(c).at[batch].set(c_ctx)
    mods = _mods(cvec, ada_w, ada_b).reshape(depth, 8, 6, 1, d)

    h = jnp.concatenate([x.reshape(batch * seq, d), ctx.reshape(batch * n_ctx, d)], 0)
    n1g, n2g = norm1_g[:, None, :], norm2_g[:, None, :]
    for l in range(depth):
        p = _inproj(h, n1g, mods, w_in_r, l, cfg)
        qkv = _qkprep(p, qk_gains[l], cos_t, sin_t, e_attn, cfg)
        o_a = _attention(qkv, 0, a_sink[l], cfg, window=True)
        o_c = _attention(qkv, 1, None, cfg, window=False)
        lw = dict(mu=mu_r[l], w0=decay_w0[l], w_up=w_up_r[l], a0=iclr_a0[l], a_up=a_up_r[l], g_up=gate_up[l],
                  k_k=k_k[l][None], k_a=k_a[l][None], r_k=r_k[l][None])
        s_v, s_f, s_b, e_tot, bonus, gate = _rwkv_prep(p, lw, e_b, cfg)
        y_f, y_b = _rwkv_scan(s_v, s_f, s_b, e_tot, cfg)
        o_b = _readout(y_f, y_b, bonus, gate, gn_g[l][None], gn_b[l][None], e_b, cfg)
        h = _outproj(o_a, o_b, o_c, w_out_b, h, mods, l, cfg)
        h = _mlp(h, n2g, mods, w1_b, w2_b, l, cfg)
    return h[:batch * seq].reshape(batch, seq, d)
```

```python
import functools

import jax
import jax.numpy as jnp
from jax import lax
from jax.experimental import pallas as pl
from jax.experimental.pallas import tpu as pltpu

F32 = jnp.float32
BF16 = jnp.bfloat16

HEAD_DIM = 64
GRID_W = 64
WINDOW = 128
ROPE_THETA = 10000.0
NORM_EPS = 1e-6
GN_EPS = 64e-5
NEG_INF = -1e30
LOG2E = 1.4426950408889634
DECAY_SCALE = 0.6065306597126334
LANES = 128
LORA_PAD = 128
SUB = 16
CHUNK = 64
VMEM_LIMIT = 52 * 1024 * 1024


def _dot(a, b, trans_a=False, trans_b=False):
    dn = (((0 if trans_a else 1,), (1 if trans_b else 0,)), ((), ()))
    return lax.dot_general(a, b, dn, preferred_element_type=F32)


def _split(x):
    hi = x.astype(BF16)
    lo = (x - hi.astype(F32)).astype(BF16)
    return hi, lo


def _dot3(a, b, trans_a=False, trans_b=False):
    ah, al = _split(a)
    bh, bl = _split(b)
    kw = dict(trans_a=trans_a, trans_b=trans_b)
    return _dot(ah, bh, **kw) + (_dot(al, bh, **kw) + _dot(ah, bl, **kw))


def _dot1(a, b, trans_a=False, trans_b=False):
    return _dot(a.astype(BF16), b.astype(BF16), trans_a=trans_a, trans_b=trans_b)


def _group_sum(x, e_ref):
    gw = e_ref.shape[0]
    e = e_ref[...]
    outs = []
    for g in range(x.shape[1] // gw):
        hi, lo = _split(x[:, g * gw:(g + 1) * gw])
        outs.append(_dot(hi, e) + _dot(lo, e))
    return outs[0] if len(outs) == 1 else jnp.concatenate(outs, axis=1)


def _cparams(sem):
    return pltpu.CompilerParams(dimension_semantics=sem, vmem_limit_bytes=VMEM_LIMIT)


def _mods_kernel(c_ref, w_ref, b_ref, o_ref):
    c = c_ref[...]
    s = c * jax.nn.sigmoid(c)
    o_ref[0] = _dot3(s, w_ref[0]) + b_ref[0]


def _mods(cvec, ada_w, ada_b):
    depth, d, n = ada_w.shape
    tn = 512
    return pl.pallas_call(
        _mods_kernel,
        grid=(depth, n // tn),
        in_specs=[
            pl.BlockSpec((8, d), lambda l, j: (0, 0)),
            pl.BlockSpec((1, d, tn), lambda l, j: (l, 0, j)),
            pl.BlockSpec((1, 1, tn), lambda l, j: (l, 0, j)),
        ],
        out_specs=pl.BlockSpec((1, 8, tn), lambda l, j: (l, 0, j)),
        out_shape=jax.ShapeDtypeStruct((depth, 8, n), F32),
        compiler_params=_cparams(("parallel", "parallel")),
        name="adaln_mods",
    )(cvec, ada_w, ada_b.reshape(depth, 1, n))


def _modulated_norm(x, g, sc, sh):
    ms = jnp.mean(x * x, axis=-1, keepdims=True)
    return (x * lax.rsqrt(ms + NORM_EPS) * g) * (1.0 + sc) + sh


def _inproj_kernel(h_ref, g_ref, sc_ref, sh_ref, w_ref, o_ref, u_scr):
    @pl.when(pl.program_id(1) == 0)
    def _():
        u_scr[...] = _modulated_norm(h_ref[...], g_ref[...], sc_ref[...], sh_ref[...]).astype(BF16)

    o_ref[...] = _dot(u_scr[...], w_ref[...])


def _inproj(h, gain, mods, w, l, cfg):
    rows, d = h.shape
    n = w.shape[2]
    tm, tn = cfg["tm"], 1792
    modrow = cfg["modrow"]
    return pl.pallas_call(
        _inproj_kernel,
        grid=(rows // tm, n // tn),
        in_specs=[
            pl.BlockSpec((tm, d), lambda i, j: (i, 0)),
            pl.BlockSpec((None, 1, d), lambda i, j: (l, 0, 0)),
            pl.BlockSpec((None, None, None, 1, d), lambda i, j: (l, modrow(i), 1, 0, 0)),
            pl.BlockSpec((None, None, None, 1, d), lambda i, j: (l, modrow(i), 0, 0, 0)),
            pl.BlockSpec((None, d, tn), lambda i, j: (l, 0, j)),
        ],
        out_specs=pl.BlockSpec((tm, tn), lambda i, j: (i, j)),
        out_shape=jax.ShapeDtypeStruct((rows, n), F32),
        scratch_shapes=[pltpu.VMEM((tm, d), BF16)],
        compiler_params=_cparams(("parallel", "arbitrary")),
        name="in_proj",
    )(h, gain, mods, mods, w)


def _qkprep_kernel(p_ref, gain_ref, cos_ref, sin_ref, e_ref, o_ref, *, qk_w):
    x = p_ref[...]
    width = x.shape[1]
    ss = _group_sum(x * x, e_ref)
    y = x * lax.rsqrt(ss * (1.0 / HEAD_DIM) + NORM_EPS) * gain_ref[0]
    cos = cos_ref[...]
    sin = sin_ref[...]
    lane = lax.broadcasted_iota(jnp.int32, (x.shape[0], LANES), 1)
    first_half = (lane & (HEAD_DIM // 2)) == 0
    for g in range(width // LANES):
        sl = slice(g * LANES, (g + 1) * LANES)
        if g * LANES < qk_w:
            yg = y[:, sl]
            partner = jnp.where(first_half, pltpu.roll(yg, LANES - HEAD_DIM // 2, 1),
                                pltpu.roll(yg, HEAD_DIM // 2, 1))
            o_ref[0, :, sl] = (yg * cos + partner * sin).astype(BF16)
        else:
            o_ref[0, :, sl] = x[:, sl].astype(BF16)


def _qkprep(p, gains, cos_t, sin_t, e_mat, cfg):
    rows = p.shape[0]
    tm, aw = cfg["tm"], cfg["attn_w"]
    first_blk = cfg["attn_col0"] // aw
    rope_blk = cfg["rope_blk"]
    return pl.pallas_call(
        functools.partial(_qkprep_kernel, qk_w=cfg["qk_w"]),
        grid=(rows // tm, 2),
        in_specs=[
            pl.BlockSpec((tm, aw), lambda i, s: (i, first_blk + s)),
            pl.BlockSpec((1, 1, aw), lambda i, s: (s, 0, 0)),
            pl.BlockSpec((tm, LANES), lambda i, s: (rope_blk(i), 0)),
            pl.BlockSpec((tm, LANES), lambda i, s: (rope_blk(i), 0)),
            pl.BlockSpec(e_mat.shape, lambda i, s: (0, 0)),
        ],
        out_specs=pl.BlockSpec((1, tm, aw), lambda i, s: (s, i, 0)),
        out_shape=jax.ShapeDtypeStruct((2, rows, aw), BF16),
        compiler_params=_cparams(("parallel", "parallel")),
        name="qk_prep",
    )(p, gains, cos_t, sin_t, e_mat)


def _attn_kernel(sink_ref, q_ref, kc_ref, vc_ref, kl_ref, vl_ref, o_ref, *,
                 window, n_heads, group, tq, tk, seq):
    i = pl.program_id(1)
    is_lat = i < seq // tq
    hd = HEAD_DIM
    n_kv = n_heads // group
    rows = group * tq
    den_lane = [((j + 1) % n_kv) * hd for j in range(n_kv)]
    if window:
        span = tq + 2 * WINDOW
        start = pl.multiple_of(jnp.clip(i * tq - WINDOW, 0, seq - span), WINDOW)
        delta = (lax.broadcasted_iota(jnp.int32, (tq, span), 0) - lax.broadcasted_iota(jnp.int32, (tq, span), 1)
                 + (i * tq - start))
        bias = jnp.where(jnp.abs(delta) <= WINDOW, 0.0, NEG_INF)
        n_iter = jnp.where(is_lat, 1, 0)
    else:
        span = tk
        n_iter = jnp.where(is_lat, seq // tk, 0)

    def with_ones(vblk, j):
        lane = lax.broadcasted_iota(jnp.int32, vblk.shape, 1)
        ones_col = jnp.where(lane == den_lane[j], 1.0, 0.0).astype(BF16)
        return jnp.where((lane >= j * hd) & (lane < (j + 1) * hd), vblk, ones_col)

    zeros = jnp.zeros((tq, hd), BF16)
    acc_lane = lax.broadcasted_iota(jnp.int32, (rows, LANES), 1)
    for j in range(n_kv):
        qs = []
        for g in range(group):
            h = j * group + g
            parts = [zeros] * n_kv
            parts[j] = q_ref[0, :, h * hd:(h + 1) * hd]
            qs.append(jnp.concatenate(parts, axis=1))
        q = jnp.concatenate(qs, axis=0)
        if sink_ref is not None:
            m0 = jnp.concatenate(
                [jnp.full((tq, 1), sink_ref[j * group + g] * LOG2E, F32) for g in range(group)], axis=0)
            acc0 = jnp.where(acc_lane == den_lane[j], 1.0, 0.0)
        else:
            m0 = jnp.full((rows, 1), NEG_INF, F32)
            acc0 = jnp.zeros((rows, LANES), F32)

        s = _dot(q, kc_ref[0], trans_b=True)
        m = jnp.maximum(m0, jnp.max(s, axis=-1, keepdims=True))
        p = jnp.exp2(s - m).astype(BF16)
        acc = jnp.exp2(m0 - m) * acc0 + _dot(p, with_ones(vc_ref[0], j))

        def body(kb, carry, q=q, j=j):
            m, acc = carry
            off = start if window else pl.multiple_of(kb * tk, tk)
            s = _dot(q, kl_ref[0, pl.ds(off, span), :], trans_b=True)
            if window:
                s = (s.reshape(group, tq, span) + bias[None]).reshape(rows, span)
            m_new = jnp.maximum(m, jnp.max(s, axis=-1, keepdims=True))
            p = jnp.exp2(s - m_new).astype(BF16)
            acc_new = jnp.exp2(m - m_new) * acc + _dot(p, with_ones(vl_ref[0, pl.ds(off, span), :], j))
            return m_new, acc_new

        m, acc = lax.fori_loop(0, n_iter, body, (m, acc))
        o = acc[:, j * hd:(j + 1) * hd] / acc[:, den_lane[j]:den_lane[j] + 1]
        for g in range(group):
            c0 = (j * group + g) * hd
            o_ref[:, c0:c0 + hd] = o[g * tq:(g + 1) * tq].astype(BF16)


def _window_attn_kernel(sink_ref, q_ref, kc_ref, vc_ref, kl_ref, vl_ref, o_ref, *,
                        window, n_heads, group, tq, tk, seq):
    del window, tk
    i = pl.program_id(1)
    is_lat = i < seq // tq
    hd = HEAD_DIM
    n_kv = n_heads // group
    rows = group * tq
    span = tq + 2 * WINDOW
    den_lane = [((j + 1) % n_kv) * hd for j in range(n_kv)]
    start = pl.multiple_of(jnp.clip(i * tq - WINDOW, 0, seq - span), WINDOW)
    delta = (lax.broadcasted_iota(jnp.int32, (tq, span), 0) - lax.broadcasted_iota(jnp.int32, (tq, span), 1)
             + (i * tq - start))
    bias = jnp.where(jnp.abs(delta) <= jnp.where(is_lat, WINDOW, -1), 0.0, NEG_INF)
    k_ctx, v_ctx = kc_ref[0], vc_ref[0]
    k_win, v_win = kl_ref[0, pl.ds(start, span), :], vl_ref[0, pl.ds(start, span), :]

    def with_ones(vblk, j):
        lane = lax.broadcasted_iota(jnp.int32, vblk.shape, 1)
        ones_col = jnp.where(lane == den_lane[j], 1.0, 0.0).astype(BF16)
        return jnp.where((lane >= j * hd) & (lane < (j + 1) * hd), vblk, ones_col)

    zeros = jnp.zeros((tq, hd), BF16)
    for j in range(n_kv):
        qs = []
        for g in range(group):
            h = j * group + g
            parts = [zeros] * n_kv
            parts[j] = q_ref[0, :, h * hd:(h + 1) * hd]
            qs.append(jnp.concatenate(parts, axis=1))
        q = jnp.concatenate(qs, axis=0)
        m0 = jnp.concatenate(
            [jnp.full((tq, 1), sink_ref[j * group + g] * LOG2E, F32) for g in range(group)], axis=0)
        s_ctx = _dot(q, k_ctx, trans_b=True)
        s_win = (_dot(q, k_win, trans_b=True).reshape(group, tq, span) + bias[None]).reshape(rows, span)
        m = jnp.maximum(m0, jnp.maximum(jnp.max(s_ctx, axis=-1, keepdims=True),
                                        jnp.max(s_win, axis=-1, keepdims=True)))
        acc = (_dot(jnp.exp2(s_ctx - m).astype(BF16), with_ones(v_ctx, j))
               + _dot(jnp.exp2(s_win - m).astype(BF16), with_ones(v_win, j)))
        den = acc[:, den_lane[j]:den_lane[j] + 1] + jnp.exp2(m0 - m)
        o = acc[:, j * hd:(j + 1) * hd] / den
        for g in range(group):
            c0 = (j * group + g) * hd
            o_ref[:, c0:c0 + hd] = o[g * tq:(g + 1) * tq].astype(BF16)


def _attention(qkv, sec, sink, cfg, window):
    rows = qkv.shape[1]
    b, seq, ctx, tq = cfg["batch"], cfg["seq"], cfg["ctx"], cfg["tq"]
    n_heads, n_kv = cfg["attn_heads"], cfg["attn_kv"]
    q_w = n_heads * HEAD_DIM
    kv_w = n_kv * HEAD_DIM
    assert kv_w == LANES and q_w % kv_w == 0 and ctx == tq
    n_lat_tiles = seq // tq
    k_blk = q_w // kv_w
    ctx_blk0 = b * seq // ctx

    def q_map(bi, i, *_):
        return (sec, jnp.where(i < n_lat_tiles, bi * n_lat_tiles + i, ctx_blk0 + bi), 0)

    def o_map(bi, i, *_):
        return (jnp.where(i < n_lat_tiles, bi * n_lat_tiles + i, ctx_blk0 + bi), 0)

    tk = cfg["tk"]
    assert seq % tk == 0 and tq % WINDOW == 0 and tq + 2 * WINDOW <= seq
    assert (sink is not None) == window
    kernel = functools.partial(_window_attn_kernel if window else _attn_kernel, window=window, n_heads=n_heads,
                               group=n_heads // n_kv, tq=tq, tk=tk, seq=seq)
    in_specs = [
        pl.BlockSpec((1, tq, q_w), q_map),
        pl.BlockSpec((1, ctx, kv_w), lambda bi, i, *_: (sec, ctx_blk0 + bi, k_blk)),
        pl.BlockSpec((1, ctx, kv_w), lambda bi, i, *_: (sec, ctx_blk0 + bi, k_blk + 1)),
        pl.BlockSpec((1, seq, kv_w), lambda bi, i, *_: (sec, bi, k_blk)),
        pl.BlockSpec((1, seq, kv_w), lambda bi, i, *_: (sec, bi, k_blk + 1)),
    ]
    args = [qkv, qkv, qkv, qkv, qkv]
    if sink is not None:
        in_specs = [pl.BlockSpec(memory_space=pltpu.SMEM)] + in_specs
        args = [sink] + args
    else:
        kernel = functools.partial(kernel, None)
    return pl.pallas_call(
        kernel,
        grid=(b, n_lat_tiles + 1),
        in_specs=in_specs,
        out_specs=pl.BlockSpec((tq, q_w), o_map),
        out_shape=jax.ShapeDtypeStruct((rows, q_w), BF16),
        compiler_params=_cparams(("parallel", "arbitrary")),
        name="window_attn" if window else "global_attn",
    )(*args)


def _tri3(tri, x):
    h1 = x.astype(BF16)
    r1 = x - h1.astype(F32)
    h2 = r1.astype(BF16)
    h3 = (r1 - h2.astype(F32)).astype(BF16)
    return _dot(tri, h1) + (_dot(tri, h2) + _dot(tri, h3))


def _rwkv_prep_kernel(cur_ref, prev_ref, next_ref, mu_ref, w0_ref, wup_ref, a0_ref, aup_ref, gup_ref,
                      kk_ref, ka_ref, rk_ref, e_ref, tril_ref, triu_ref,
                      sv_ref, sf_ref, sb_ref, etot_ref, bonus_ref, gate_ref, *,
                      bw, seg_lat, seg_ctx, n_lat_tiles):
    i = pl.program_id(0)
    p = cur_ref[...]
    tr = p.shape[0]
    lat = i < n_lat_tiles
    seg = jnp.where(lat, seg_lat, seg_ctx)
    pos = jnp.where(lat, i, i - n_lat_tiles) % seg
    has_prev = (pos != 0).astype(F32)
    has_next = (pos != seg - 1).astype(F32)
    row = lax.broadcasted_iota(jnp.int32, (tr, 1), 0)
    prev_row = prev_ref[7:8, :] * has_prev
    next_row = next_ref[0:1, :] * has_next
    prv = jnp.where(row == 0, prev_row, pltpu.roll(p, 1, 0))
    nxt = jnp.where(row == tr - 1, next_row, pltpu.roll(p, tr - 1, 0))
    xs = p + mu_ref[0:1, :] * (prv - p) + mu_ref[1:2, :] * (nxt - p)

    r = xs[:, 0:bw]
    k = xs[:, bw:2 * bw]
    v = xs[:, 2 * bw:3 * bw]
    lora = xs[:, 3 * bw:]
    n_hp = bw // LANES

    kk = k * kk_ref[...]
    kk = kk * lax.rsqrt(jnp.maximum(_group_sum(kk * kk, e_ref), 1e-24))
    ksum = jnp.zeros_like(k)
    for d, (s_ref, tri_ref) in enumerate(((sf_ref, tril_ref), (sb_ref, triu_ref))):
        wd = lora[:, d * LORA_PAD:(d + 1) * LORA_PAD]
        ad = lora[:, (2 + d) * LORA_PAD:(3 + d) * LORA_PAD]
        log_decay = -DECAY_SCALE * jax.nn.sigmoid(w0_ref[d:d + 1, :] + _dot3(jnp.tanh(wd), wup_ref[d]))
        a = jax.nn.sigmoid(a0_ref[d:d + 1, :] + _dot3(ad, aup_ref[d]))
        key = k * (1.0 + (a - 1.0) * ka_ref[...])
        ksum = ksum + key
        kka = kk * a
        cum = _tri3(tri_ref[...], log_decay)
        last = CHUNK - 1 if d == 0 else 0
        tot_rows = [cum[c * CHUNK + last:c * CHUNK + last + 1] for c in range(tr // CHUNK)]
        tot = jnp.concatenate([jnp.broadcast_to(t, (CHUNK, bw)) for t in tot_rows], axis=0)
        e_inv = jnp.exp(-cum)
        e_rel = jnp.exp(tot - cum)
        streams = (kk * jnp.exp(cum - log_decay), r * jnp.exp(cum), kka * e_inv, key * e_inv,
                   kka * e_rel, key * e_rel)
        for n, st in enumerate(streams):
            st = st.astype(BF16)
            for hp in range(n_hp):
                s_ref[n, hp] = st[:, hp * LANES:(hp + 1) * LANES]
        for cidx in range(tr // CHUNK):
            e_tot = jnp.exp(tot_rows[cidx])
            etot_ref[d, cidx] = jnp.concatenate(
                [e_tot[:, hp * LANES:(hp + 1) * LANES] for hp in range(n_hp)], axis=0)
    vb = v.astype(BF16)
    for hp in range(n_hp):
        sv_ref[hp] = vb[:, hp * LANES:(hp + 1) * LANES]
    bonus_ref[...] = _group_sum(r * ksum * rk_ref[...], e_ref) * v
    gate_ref[...] = _dot3(jax.nn.sigmoid(lora[:, 4 * LORA_PAD:]), gup_ref[...])


def _chunk_block_diag(tr, kind):
    t = jnp.arange(tr)
    same = (t[:, None] // CHUNK) == (t[None, :] // CHUNK)
    if kind == "lower":
        same = same & (t[None, :] <= t[:, None])
    elif kind == "upper":
        same = same & (t[None, :] >= t[:, None])
    return same.astype(BF16)


N_STREAMS = 6


def _rwkv_prep(p, lw, e_mat, cfg):
    rows = p.shape[0]
    bw, tr = cfg["b_width"], cfg["tr"]
    cw = 3 * bw + cfg["lora_w"]
    n_hp = bw // LANES
    n_lat_tiles = cfg["batch"] * cfg["seq"] // tr
    hb = tr // 8
    last8 = rows // 8 - 1
    full = lambda a: pl.BlockSpec(a.shape, lambda i: (0,) * a.ndim)
    consts = [lw["mu"], lw["w0"], lw["w_up"], lw["a0"], lw["a_up"], lw["g_up"], lw["k_k"], lw["k_a"], lw["r_k"],
              e_mat, _chunk_block_diag(tr, "lower"), _chunk_block_diag(tr, "upper")]
    stream_spec = pl.BlockSpec((N_STREAMS, n_hp, tr, LANES), lambda i: (0, 0, i, 0))
    stream_shape = jax.ShapeDtypeStruct((N_STREAMS, n_hp, rows, LANES), BF16)
    cpt = tr // CHUNK
    kernel = functools.partial(_rwkv_prep_kernel, bw=bw, seg_lat=cfg["seq"] // tr, seg_ctx=cfg["ctx"] // tr,
                               n_lat_tiles=n_lat_tiles)
    return pl.pallas_call(
        kernel,
        grid=(rows // tr,),
        in_specs=[
            pl.BlockSpec((tr, cw), lambda i: (i, 0)),
            pl.BlockSpec((8, cw), lambda i: (jnp.maximum(i * hb - 1, 0), 0)),
            pl.BlockSpec((8, cw), lambda i: (jnp.minimum((i + 1) * hb, last8), 0)),
        ] + [full(a) for a in consts],
        out_specs=[pl.BlockSpec((n_hp, tr, LANES), lambda i: (0, i, 0)),
                   stream_spec, stream_spec,
                   pl.BlockSpec((2, cpt, n_hp, LANES), lambda i: (0, i, 0, 0)),
                   pl.BlockSpec((tr, bw), lambda i: (i, 0)),
                   pl.BlockSpec((tr, bw), lambda i: (i, 0))],
        out_shape=[jax.ShapeDtypeStruct((n_hp, rows, LANES), BF16),
                   stream_shape, stream_shape,
                   jax.ShapeDtypeStruct((2, rows // CHUNK, n_hp, LANES), F32),
                   jax.ShapeDtypeStruct((rows, bw), F32),
                   jax.ShapeDtypeStruct((rows, bw), F32)],
        compiler_params=_cparams(("parallel",)),
        name="rwkv_prep",
    )(p, p, p, *consts)


def _lhs3(a):
    hi, lo = _split(a)
    return jnp.concatenate([hi, lo, hi], axis=1)


def _rhs3(b):
    hi, lo = _split(b)
    return jnp.concatenate([hi, hi, lo], axis=0)


def _block_diag(x):
    lo = lax.broadcasted_iota(jnp.int32, x.shape, 1) < HEAD_DIM
    zero = jnp.zeros_like(x)
    return jnp.concatenate([jnp.where(lo, x, zero), jnp.where(lo, zero, x)], axis=0)


def _scan_step(dirs, ones_ref, z_ref, n_hp):
    c, hd = CHUNK, HEAD_DIM
    assert c == hd and LANES == 2 * hd
    row = lax.broadcasted_iota(jnp.int32, (c, LANES), 0)
    lane = lax.broadcasted_iota(jnp.int32, (c, LANES), 1)
    col = lane & (hd - 1)
    lo = lane < hd
    eye = col == row
    inst = [(d, hp) for d in range(len(dirs)) for hp in range(n_hp)]
    pairs = range(len(inst))
    incl = [(col >= row) if dirs[d][4] else (col <= row) for d, _ in inst]
    strict = [(col > row) if dirs[d][4] else (col < row) for d, _ in inst]

    def half(x, h):
        keep = lo if h == 0 else ~lo
        return jnp.where(keep, x, jnp.zeros_like(x))

    qk, rt, bt, kt, bh, kh = ([dirs[d][1][k, hp] for d, hp in inst] for k in range(N_STREAMS))
    vv = [dirs[d][0][hp] for d, hp in inst]
    p1 = [_dot(jnp.concatenate([qk[i], rt[i]], axis=0),
               jnp.concatenate([half(bt[i], 0), half(bt[i], 1), half(kt[i], 0), half(kt[i], 1)], axis=0),
               trans_b=True) for i in pairs]
    same_blk = (row // SUB) == (col // SUB)
    a_ab = [jnp.where(strict[i], p[:c, :LANES], 0.0) for i, p in enumerate(p1)]
    x_pow = [jnp.where(same_blk, -a, 0.0) for a in a_ab]
    u = x_pow
    x_pow = [_dot(x.astype(BF16), _block_diag(x.astype(BF16))) for x in x_pow]
    sq = 2
    while sq < SUB:
        w = [_block_diag(x.astype(BF16)) for x in x_pow]
        if 2 * sq < SUB:
            prod = [_dot(jnp.concatenate([u[i], x_pow[i]], axis=0).astype(BF16), w[i]) for i in pairs]
            u = [u[i] + x_pow[i] + prod[i][:c] for i in pairs]
            x_pow = [pr[c:] for pr in prod]
        else:
            u = [u[i] + x_pow[i] + _dot(u[i].astype(BF16), w[i]) for i in pairs]
        sq *= 2
    u_bf = [x.astype(BF16) for x in u]
    n_off = [jnp.where(same_blk, 0.0, a) for a in a_ab]
    m1_bf = [(-(n_off[i] + _dot(u_bf[i], _block_diag(n_off[i].astype(BF16))))).astype(BF16) for i in pairs]
    m2_bf = [_dot(m1_bf[i], _block_diag(m1_bf[i])).astype(BF16) for i in pairs]
    avy = [_dot(jnp.concatenate([jnp.where(strict[i], p[:c, LANES:], 0.0), jnp.where(incl[i], p[c:, LANES:], 0.0)],
                                axis=0).astype(BF16), _block_diag(vv[i])) for i, p in enumerate(p1)]

    def pair_cols(x):
        return jnp.concatenate([_block_diag(x[:, :LANES].astype(BF16)), _block_diag(x[:, LANES:].astype(BF16))],
                               axis=1)

    r = [jnp.concatenate([qk[i].astype(F32), avy[i][:c]], axis=1) for i in pairs]
    r = [r[i] + _dot(u_bf[i], pair_cols(r[i])) for i in pairs]
    r = [r[i] + _dot(m2_bf[i], pair_cols(r[i])) for i in pairs]
    tu = [(r[i] + _dot(m1_bf[i], pair_cols(r[i]))).astype(BF16) for i in pairs]
    bu = [_dot(jnp.where(incl[i], p[c:, :LANES], 0.0).astype(BF16),
               jnp.concatenate([_block_diag(tu[i][:, :LANES]), _block_diag(tu[i][:, LANES:])], axis=1))
          for i, p in enumerate(p1)]
    rh = [(rt[i].astype(F32) - bu[i][:, :LANES]).astype(BF16) for i in pairs]
    yh = [avy[i][c:] - bu[i][:, LANES:] for i in pairs]
    rp = [_dot(bh[i], tu[i], trans_a=True) for i in pairs]
    kv = [_dot(kh[i], vv[i], trans_a=True) for i in pairs]
    g = [jnp.where(lo, r[:hd, :LANES], r[hd:, :LANES]).astype(BF16) for r in rp]
    hc = [jnp.where(lo, kv[i][:hd], kv[i][hd:]) - jnp.where(lo, rp[i][:hd, LANES:], rp[i][hd:, LANES:])
          for i in pairs]
    ones2 = jnp.concatenate([ones_ref[...], ones_ref[...]], axis=0)
    e_col = []
    for d, hp in inst:
        e_diag = jnp.where(eye, jnp.broadcast_to(dirs[d][2][hp:hp + 1, :], (c, LANES)), 0.0)
        e_col.append(_dot(jnp.concatenate(_split(e_diag), axis=1), ones2))
    z = [z_ref[i] for i in pairs]
    yz = [_dot(jnp.concatenate([rh[i], g[i]], axis=0), _block_diag(z[i].astype(BF16))) for i in pairs]
    for i, (d, hp) in enumerate(inst):
        dirs[d][3][hp] = yh[i] + yz[i][:c]
        z_ref[i] = e_col[i] * z[i] - yz[i][c:] + hc[i]


def _rwkv_scan_kernel(vf_ref, sf_ref, ef_ref, vb_ref, sb_ref, eb_ref, ones_ref, yf_ref, yb_ref, z_ref, *, n_hp):
    @pl.when(pl.program_id(1) == 0)
    def _():
        z_ref[...] = jnp.zeros_like(z_ref)

    _scan_step([(vf_ref, sf_ref, ef_ref, yf_ref, False), (vb_ref, sb_ref, eb_ref, yb_ref, True)],
               ones_ref, z_ref, n_hp)


def _rwkv_scan(s_v, s_f, s_b, e_tot, cfg):
    n_hp, rows, _ = s_v.shape
    b, seq, ctx = cfg["batch"], cfg["seq"], cfg["ctx"]
    c = CHUNK
    ncc, ncl = ctx // c, seq // c
    ctx0 = b * seq // c

    def fwd_blk(bi, j):
        return jnp.where(j < ncc, ctx0 + bi * ncc + j, bi * ncl + (j - ncc))

    def bwd_blk(bi, j):
        return jnp.where(j < ncc, ctx0 + bi * ncc + (ncc - 1 - j), bi * ncl + (ncl - 1 - (j - ncc)))

    def specs(blk, d):
        return [pl.BlockSpec((n_hp, c, LANES), lambda bi, j: (0, blk(bi, j), 0)),
                pl.BlockSpec((N_STREAMS, n_hp, c, LANES), lambda bi, j: (0, 0, blk(bi, j), 0)),
                pl.BlockSpec((None, None, n_hp, LANES), lambda bi, j: (d, blk(bi, j), 0, 0))]

    out_f = pl.BlockSpec((n_hp, c, LANES), lambda bi, j: (0, fwd_blk(bi, j), 0))
    out_b = pl.BlockSpec((n_hp, c, LANES), lambda bi, j: (0, bwd_blk(bi, j), 0))
    y_shape = jax.ShapeDtypeStruct((n_hp, rows, LANES), F32)
    return pl.pallas_call(
        functools.partial(_rwkv_scan_kernel, n_hp=n_hp),
        grid=(b, ncc + ncl),
        in_specs=specs(fwd_blk, 0) + specs(bwd_blk, 1) + [pl.BlockSpec((LANES, LANES), lambda bi, j: (0, 0))],
        out_specs=[out_f, out_b],
        out_shape=[y_shape, y_shape],
        scratch_shapes=[pltpu.VMEM((2 * n_hp, HEAD_DIM, LANES), F32)],
        compiler_params=_cparams(("parallel", "arbitrary")),
        name="rwkv_scan",
    )(s_v, s_f, e_tot, s_v, s_b, e_tot, _block_ones(LANES))


def _readout_kernel(yf_ref, yb_ref, bonus_ref, gate_ref, gg_ref, gb_ref, e_ref, o_ref):
    n_hp = yf_ref.shape[0]
    y = jnp.concatenate([yf_ref[hp] + yb_ref[hp] for hp in range(n_hp)], axis=1)
    mu = _group_sum(y, e_ref) * (1.0 / HEAD_DIM)
    yc = y - mu
    var = _group_sum(yc * yc, e_ref) * (1.0 / HEAD_DIM)
    yn = yc * lax.rsqrt(var + GN_EPS) * gg_ref[...] + gb_ref[...]
    o_ref[...] = ((yn + bonus_ref[...]) * gate_ref[...]).astype(BF16)


def _readout(y_f, y_b, bonus, gate, gn_g, gn_b, e_mat, cfg):
    n_hp, rows, _ = y_f.shape
    bw, tm = cfg["b_width"], cfg["tm"]
    y_spec = pl.BlockSpec((n_hp, tm, LANES), lambda i: (0, i, 0))
    row_spec = pl.BlockSpec((tm, bw), lambda i: (i, 0))
    vec_spec = pl.BlockSpec((1, bw), lambda i: (0, 0))
    return pl.pallas_call(
        _readout_kernel,
        grid=(rows // tm,),
        in_specs=[y_spec, y_spec, row_spec, row_spec, vec_spec, vec_spec,
                  pl.BlockSpec(e_mat.shape, lambda i: (0, 0))],
        out_specs=row_spec,
        out_shape=jax.ShapeDtypeStruct((rows, bw), BF16),
        compiler_params=_cparams(("parallel",)),
        name="rwkv_readout",
    )(y_f, y_b, bonus, gate, gn_g, gn_b, e_mat)


def _outproj_kernel(oa_ref, ob_ref, oc_ref, w_ref, h_ref, g_ref, o_ref):
    mix = jnp.concatenate([oa_ref[...], ob_ref[...], oc_ref[...]], axis=1)
    o_ref[...] = h_ref[...] + g_ref[...] * _dot(mix, w_ref[...])


def _outproj(o_a, o_b, o_c, w, h, mods, l, cfg):
    rows, d = h.shape
    tm, tn = cfg["tm"], 2048
    modrow = cfg["modrow"]
    act =lambda a: pl.BlockSpec((tm, a.shape[1]), lambda i, j: (i, 0))
    return pl.pallas_call(
        _outproj_kernel,
        grid=(rows // tm, d // tn),
        in_specs=[act(o_a), act(o_b), act(o_c),
                  pl.BlockSpec((None, d, tn), lambda i, j: (l, 0, j)),
                  pl.BlockSpec((tm, tn), lambda i, j: (i, j)),
                  pl.BlockSpec((None, None, None, 1, tn), lambda i, j: (l, modrow(i), 2, 0, j))],
        out_specs=pl.BlockSpec((tm, tn), lambda i, j: (i, j)),
        out_shape=jax.ShapeDtypeStruct((rows, d), F32),
        compiler_params=_cparams(("parallel", "arbitrary")),
        name="out_proj",
    )(o_a, o_b, o_c, w, h, mods)


def _mlp_kernel(h_ref, g_ref, sc_ref, sh_ref, gate_ref, w1_ref, w2_ref, o_ref, u_scr, acc_scr):
    f = pl.program_id(1)

    @pl.when(f == 0)
    def _():
        u_scr[...] = _modulated_norm(h_ref[...], g_ref[...], sc_ref[...], sh_ref[...]).astype(BF16)

    a = jnp.maximum(_dot(u_scr[...], w1_ref[...]), 0.0)
    part = _dot((a * a).astype(BF16), w2_ref[...])

    @pl.when(f == 0)
    def _():
        acc_scr[...] = part

    @pl.when(f > 0)
    def _():
        acc_scr[...] += part

    @pl.when(f == pl.num_programs(1) - 1)
    def _():
        o_ref[...] = h_ref[...] + gate_ref[...] * acc_scr[...]


def _mlp(h, gain, mods, w1, w2, l, cfg):
    rows, d = h.shape
    dff = w1.shape[2]
    tm, tf = cfg["tm"], 1024
    modrow = cfg["modrow"]
    mod = lambda k: pl.BlockSpec((None, None, None, 1, d), lambda i, f: (l, modrow(i), k, 0, 0))
    return pl.pallas_call(
        _mlp_kernel,
        grid=(rows // tm, dff // tf),
        in_specs=[
            pl.BlockSpec((tm, d), lambda i, f: (i, 0)),
            pl.BlockSpec((None, 1, d), lambda i, f: (l, 0, 0)),
            mod(4), mod(3), mod(5),
            pl.BlockSpec((None, d, tf), lambda i, f: (l, 0, f)),
            pl.BlockSpec((None, tf, d), lambda i, f: (l, f, 0)),
        ],
        out_specs=pl.BlockSpec((tm, d), lambda i, f: (i, 0)),
        out_shape=jax.ShapeDtypeStruct((rows, d), F32),
        scratch_shapes=[pltpu.VMEM((tm, d), BF16), pltpu.VMEM((tm, d), F32)],
        compiler_params=_cparams(("parallel", "arbitrary")),
        name="mlp",
    )(h, gain, mods, mods, mods, w1, w2)


def _rope_tables(seq, tm):
    rows = seq // GRID_W
    row = jnp.broadcast_to(jnp.arange(rows)[:, None], (rows, GRID_W)).reshape(-1)
    col = jnp.broadcast_to(jnp.arange(GRID_W)[None, :], (rows, GRID_W)).reshape(-1)
    n_freq = HEAD_DIM // 4
    inv = ROPE_THETA ** (-jnp.arange(n_freq, dtype=F32) / n_freq)
    ang = jnp.concatenate([row[:, None].astype(F32) * inv, col[:, None].astype(F32) * inv], -1)
    cos, sin = jnp.cos(ang), jnp.sin(ang)
    reps = LANES // HEAD_DIM
    cos_t = jnp.tile(jnp.concatenate([cos, cos], -1), (1, reps))
    sin_t = jnp.tile(jnp.concatenate([-sin, sin], -1), (1, reps))
    cos_t = jnp.concatenate([cos_t, jnp.ones((tm, LANES), F32)], 0)
    sin_t = jnp.concatenate([sin_t, jnp.zeros((tm, LANES), F32)], 0)
    return cos_t, sin_t


def _block_ones(width):
    g = jnp.arange(width) // HEAD_DIM
    return (g[:, None] == g[None, :]).astype(BF16)


def kernel(x, c, ctx, c_ctx, ada_w, ada_b, norm1_g, norm2_g, w_in, a_q_norm, a_k_norm, a_sink, c_q_norm,
           c_k_norm, shift_mu, decay_w0, decay_up, iclr_a0, iclr_up, gate_up, k_k, k_a, r_k, gn_g, gn_b,
           w_out, mlp_w1, mlp_w2):
    batch, seq, d = x.shape
    n_ctx = ctx.shape[1]
    depth = ada_w.shape[0]
    bw = k_k.shape[1]
    lora_d, lora_i, lora_g = decay_up.shape[2], iclr_up.shape[2], gate_up.shape[1]
    a_heads = a_sink.shape[1]
    a_kv = a_heads // 4
    q_w, kv_w = a_heads * HEAD_DIM, a_kv * HEAD_DIM
    attn_w = q_w + 2 * kv_w
    lora_w = 4 * LORA_PAD + lora_g
    b_in = 3 * bw + 2 * lora_d + 2 * lora_i + lora_g
    assert lora_d <= LORA_PAD and lora_i <= LORA_PAD
    assert w_in.shape[2] == 2 * attn_w + b_in

    tm = batch * n_ctx
    n_lat_tiles = batch * seq // tm
    tiles_per_batch = seq // tm
    cfg = dict(
        batch=batch, seq=seq, ctx=n_ctx, tm=tm, tr=n_ctx, tq=n_ctx, tk=min(seq, 2048),
        b_width=bw, lora_w=lora_w, attn_w=attn_w, attn_col0=3 * bw + lora_w, qk_w=q_w + kv_w,
        attn_heads=a_heads, attn_kv=a_kv,
        modrow=lambda i: jnp.where(i < n_lat_tiles, i // tiles_per_batch, batch),
        rope_blk=lambda i: jnp.where(i < n_lat_tiles, i % tiles_per_batch, tiles_per_batch),
    )
    assert seq % tm == 0 and seq % GRID_W == 0 and n_ctx % CHUNK == 0 and (3 * bw) % lora_w == 0
    assert cfg["attn_col0"] % attn_w == 0 and batch + 1 <= 8

    def relayout_cols(m):
        a_part, b_part, c_part = m[..., :attn_w], m[..., attn_w:attn_w + b_in], m[..., attn_w + b_in:]
        pad = lambda z, n: jnp.pad(z, [(0, 0)] * (z.ndim - 1) + [(0, n - z.shape[-1])])
        o = 3 * bw
        pieces = [b_part[..., :o]]
        for width in (lora_d, lora_d, lora_i, lora_i):
            pieces.append(pad(b_part[..., o:o + width], LORA_PAD))
            o += width
        pieces.append(b_part[..., o:])
        return jnp.concatenate(pieces + [a_part, c_part], -1)

    w_in_r = relayout_cols(w_in).astype(BF16)
    mu_r = relayout_cols(jnp.pad(shift_mu, ((0, 0), (0, 0), (attn_w, attn_w))))[..., :3 * bw + lora_w]
    pad_rows = lambda z: jnp.pad(z, ((0, 0), (0, 0), (0, LORA_PAD - z.shape[2]), (0, 0)))
    w_up_r, a_up_r = pad_rows(decay_up), pad_rows(iclr_up)
    w_out_b = w_out.astype(BF16)
    w1_b, w2_b = mlp_w1.astype(BF16), mlp_w2.astype(BF16)

    scale = HEAD_DIM ** -0.5 * LOG2E
    tile = lambda g, n: jnp.tile(g, (1, n))

    def gains(qg, kg):
        return jnp.concatenate([tile(qg, a_heads) * scale, tile(kg, a_kv), jnp.ones((depth, kv_w), F32)], -1)

    qk_gains = jnp.stack([gains(a_q_norm, a_k_norm), gains(c_q_norm, c_k_norm)], 1)[:, :, None, :]

    cos_t, sin_t = _rope_tables(seq, tm)
    e_attn = _block_ones(attn_w)
    e_b = _block_ones(2 * LANES)

    cvec = jnp.zeros((8, d), F32).at[:batch].set(c).at[batch].set(c_ctx)
    mods = _mods(cvec, ada_w, ada_b).reshape(depth, 8, 6, 1, d)

    h = jnp.concatenate([x.reshape(batch * seq, d), ctx.reshape(batch * n_ctx, d)], 0)
    n1g, n2g = norm1_g[:, None, :], norm2_g[:, None, :]
    for l in range(depth):
        p = _inproj(h, n1g, mods, w_in_r, l, cfg)
        qkv = _qkprep(p, qk_gains[l], cos_t, sin_t, e_attn, cfg)
        o_a = _attention(qkv, 0, a_sink[l], cfg, window=True)
        o_c = _attention(qkv, 1, None, cfg, window=False)
        lw = dict(mu=mu_r[l], w0=decay_w0[l], w_up=w_up_r[l], a0=iclr_a0[l], a_up=a_up_r[l], g_up=gate_up[l],
                  k_k=k_k[l][None], k_a=k_a[l][None], r_k=r_k[l][None])
        s_v, s_f, s_b, e_tot, bonus, gate = _rwkv_prep(p, lw, e_b, cfg)
        y_f, y_b = _rwkv_scan(s_v, s_f, s_b, e_tot, cfg)
        o_b = _readout(y_f, y_b, bonus, gate, gn_g[l][None], gn_b[l][None], e_b, cfg)
        h = _outproj(o_a, o_b, o_c, w_out_b, h, mods, l, cfg)
        h = _mlp(h, n2g, mods, w1_b, w2_b, l, cfg)
    return h[:batch * seq].reshape(batch, seq, d)
```

```python
import functools

import jax
import jax.numpy as jnp
from jax import lax
from jax.experimental import pallas as pl
from jax.experimental.pallas import tpu as pltpu

F32 = jnp.float32
BF16 = jnp.bfloat16

HEAD_DIM = 64
GRID_W = 64
WINDOW = 128
ROPE_THETA = 10000.0
NORM_EPS = 1e-6
GN_EPS = 64e-5
NEG_INF = -1e30
LOG2E = 1.4426950408889634
DECAY_SCALE = 0.6065306597126334
LANES = 128
LORA_PAD = 128
SUB = 16
CHUNK = 64
VMEM_LIMIT = 52 * 1024 * 1024


def _dot(a, b, trans_a=False, trans_b=False):
    dn = (((0 if trans_a else 1,), (1 if trans_b else 0,)), ((), ()))
    return lax.dot_general(a, b, dn, preferred_element_type=F32)


def _split(x):
    hi = x.astype(BF16)
    lo = (x - hi.astype(F32)).astype(BF16)
    return hi, lo


def _dot3(a, b, trans_a=False, trans_b=False):
    ah, al = _split(a)
    bh, bl = _split(b)
    kw = dict(trans_a=trans_a, trans_b=trans_b)
    return _dot(ah, bh, **kw) + (_dot(al, bh, **kw) + _dot(ah, bl, **kw))


def _dot1(a, b, trans_a=False, trans_b=False):
    return _dot(a.astype(BF16), b.astype(BF16), trans_a=trans_a, trans_b=trans_b)


def _group_sum(x, e_ref):
    gw = e_ref.shape[0]
    e = e_ref[...]
    outs = []
    for g in range(x.shape[1] // gw):
        hi, lo = _split(x[:, g * gw:(g + 1) * gw])
        outs.append(_dot(hi, e) + _dot(lo, e))
    return outs[0] if len(outs) == 1 else jnp.concatenate(outs, axis=1)


def _cparams(sem):
    return pltpu.CompilerParams(dimension_semantics=sem, vmem_limit_bytes=VMEM_LIMIT)


def _mods_kernel(c_ref, w_ref, b_ref, o_ref):
    c = c_ref[...]
    s = c * jax.nn.sigmoid(c)
    o_ref[0] = _dot3(s, w_ref[0]) + b_ref[0]


def _mods(cvec, ada_w, ada_b):
    depth, d, n = ada_w.shape
    tn = 512
    return pl.pallas_call(
        _mods_kernel,
        grid=(depth, n // tn),
        in_specs=[
            pl.BlockSpec((8, d), lambda l, j: (0, 0)),
            pl.BlockSpec((1, d, tn), lambda l, j: (l, 0, j)),
            pl.BlockSpec((1, 1, tn), lambda l, j: (l, 0, j)),
        ],
        out_specs=pl.BlockSpec((1, 8, tn), lambda l, j: (l, 0, j)),
        out_shape=jax.ShapeDtypeStruct((depth, 8, n), F32),
        compiler_params=_cparams(("parallel", "parallel")),
        name="adaln_mods",
    )(cvec, ada_w, ada_b.reshape(depth, 1, n))


def _modulated_norm(x, g, sc, sh):
    ms = jnp.mean(x * x, axis=-1, keepdims=True)
    return (x * lax.rsqrt(ms + NORM_EPS) * g) * (1.0 + sc) + sh


def _inproj_kernel(h_ref, g_ref, sc_ref, sh_ref, w_ref, o_ref, u_scr):
    @pl.when(pl.program_id(1) == 0)
    def _():
        u_scr[...] = _modulated_norm(h_ref[...], g_ref[...], sc_ref[...], sh_ref[...]).astype(BF16)

    o_ref[...] = _dot(u_scr[...], w_ref[...])


def _inproj(h, gain, mods, w, l, cfg):
    rows, d = h.shape
    n = w.shape[2]
    tm, tn = cfg["tm"], 1792
    modrow = cfg["modrow"](tm)
    return pl.pallas_call(
        _inproj_kernel,
        grid=(rows // tm, n // tn),
        in_specs=[
            pl.BlockSpec((tm, d), lambda i, j: (i, 0)),
            pl.BlockSpec((None, 1, d), lambda i, j: (l, 0, 0)),
            pl.BlockSpec((None, None, None, 1, d), lambda i, j: (l, modrow(i), 1, 0, 0)),
            pl.BlockSpec((None, None, None, 1, d), lambda i, j: (l, modrow(i), 0, 0, 0)),
            pl.BlockSpec((None, d, tn), lambda i, j: (l, 0, j)),
        ],
        out_specs=pl.BlockSpec((tm, tn), lambda i, j: (i, j)),
        out_shape=jax.ShapeDtypeStruct((rows, n), F32),
        scratch_shapes=[pltpu.VMEM((tm, d), BF16)],
        compiler_params=_cparams(("parallel", "arbitrary")),
        name="in_proj",
    )(h, gain, mods, mods, w)


def _qkprep_kernel(p_ref, gain_ref, cos_ref, sin_ref, e_ref, o_ref, *, qk_w):
    x = p_ref[...]
    width = x.shape[1]
    ss = _group_sum(x * x, e_ref)
    y = x * lax.rsqrt(ss * (1.0 / HEAD_DIM) + NORM_EPS) * gain_ref[0]
    cos = cos_ref[...]
    sin = sin_ref[...]
    lane = lax.broadcasted_iota(jnp.int32, (x.shape[0], LANES), 1)
    first_half = (lane & (HEAD_DIM // 2)) == 0
    for g in range(width // LANES):
        sl = slice(g * LANES, (g + 1) * LANES)
        if g * LANES < qk_w:
            yg = y[:, sl]
            partner = jnp.where(first_half, pltpu.roll(yg, LANES - HEAD_DIM // 2, 1),
                                pltpu.roll(yg, HEAD_DIM // 2, 1))
            o_ref[0, :, sl] = (yg * cos + partner * sin).astype(BF16)
        else:
            o_ref[0, :, sl] = x[:, sl].astype(BF16)


def _qkprep(p, gains, cos_t, sin_t, e_mat, cfg):
    rows = p.shape[0]
    tm, aw = cfg["tm"], cfg["attn_w"]
    first_blk = cfg["attn_col0"] // aw
    rope_blk = cfg["rope_blk"]
    return pl.pallas_call(
        functools.partial(_qkprep_kernel, qk_w=cfg["qk_w"]),
        grid=(rows // tm, 2),
        in_specs=[
            pl.BlockSpec((tm, aw), lambda i, s: (i, first_blk + s)),
            pl.BlockSpec((1, 1, aw), lambda i, s: (s, 0, 0)),
            pl.BlockSpec((tm, LANES), lambda i, s: (rope_blk(i), 0)),
            pl.BlockSpec((tm, LANES), lambda i, s: (rope_blk(i), 0)),
            pl.BlockSpec(e_mat.shape, lambda i, s: (0, 0)),
        ],
        out_specs=pl.BlockSpec((1, tm, aw), lambda i, s: (s, i, 0)),
        out_shape=jax.ShapeDtypeStruct((2, rows, aw), BF16),
        compiler_params=_cparams(("parallel", "parallel")),
        name="qk_prep",
    )(p, gains, cos_t, sin_t, e_mat)


def _attn_kernel(sink_ref, q_ref, kc_ref, vc_ref, kl_ref, vl_ref, o_ref, *,
                 window, n_heads, group, tq, tk, seq):
    i = pl.program_id(1)
    is_lat = i < seq // tq
    hd = HEAD_DIM
    n_kv = n_heads // group
    rows = group * tq
    den_lane = [((j + 1) % n_kv) * hd for j in range(n_kv)]
    if window:
        span = tq + 2 * WINDOW
        start = pl.multiple_of(jnp.clip(i * tq - WINDOW, 0, seq - span), WINDOW)
        delta = (lax.broadcasted_iota(jnp.int32, (tq, span), 0) - lax.broadcasted_iota(jnp.int32, (tq, span), 1)
                 + (i * tq - start))
        bias = jnp.where(jnp.abs(delta) <= WINDOW, 0.0, NEG_INF)
        n_iter = jnp.where(is_lat, 1, 0)
    else:
        span = tk
        n_iter = jnp.where(is_lat, seq // tk, 0)

    def with_ones(vblk, j):
        lane = lax.broadcasted_iota(jnp.int32, vblk.shape, 1)
        ones_col = jnp.where(lane == den_lane[j], 1.0, 0.0).astype(BF16)
        return jnp.where((lane >= j * hd) & (lane < (j + 1) * hd), vblk, ones_col)

    zeros = jnp.zeros((tq, hd), BF16)
    acc_lane = lax.broadcasted_iota(jnp.int32, (rows, LANES), 1)
    for j in range(n_kv):
        qs = []
        for g in range(group):
            h = j * group + g
            parts = [zeros] * n_kv
            parts[j] = q_ref[0, :, h * hd:(h + 1) * hd]
            qs.append(jnp.concatenate(parts, axis=1))
        q = jnp.concatenate(qs, axis=0)
        if sink_ref is not None:
            m0 = jnp.concatenate(
                [jnp.full((tq, 1), sink_ref[j * group + g] * LOG2E, F32) for g in range(group)], axis=0)
            acc0 = jnp.where(acc_lane == den_lane[j], 1.0, 0.0)
        else:
            m0 = jnp.full((rows, 1), NEG_INF, F32)
            acc0 = jnp.zeros((rows, LANES), F32)

        s = _dot(q, kc_ref[0], trans_b=True)
        m = jnp.maximum(m0, jnp.max(s, axis=-1, keepdims=True))
        p = jnp.exp2(s - m).astype(BF16)
        acc = jnp.exp2(m0 - m) * acc0 + _dot(p, with_ones(vc_ref[0], j))

        def body(kb, carry, q=q, j=j):
            m, acc = carry
            off = start if window else pl.multiple_of(kb * tk, tk)
            s = _dot(q, kl_ref[0, pl.ds(off, span), :], trans_b=True)
            if window:
                s = (s.reshape(group, tq, span) + bias[None]).reshape(rows, span)
            m_new = jnp.maximum(m, jnp.max(s, axis=-1, keepdims=True))
            p = jnp.exp2(s - m_new).astype(BF16)
            acc_new = jnp.exp2(m - m_new) * acc + _dot(p, with_ones(vl_ref[0, pl.ds(off, span), :], j))
            return m_new, acc_new

        m, acc = lax.fori_loop(0, n_iter, body, (m, acc))
        o = acc[:, j * hd:(j + 1) * hd] / acc[:, den_lane[j]:den_lane[j] + 1]
        for g in range(group):
            c0 = (j * group + g) * hd
            o_ref[:, c0:c0 + hd] = o[g * tq:(g + 1) * tq].astype(BF16)


def _window_attn_kernel(sink_ref, q_ref, kc_ref, vc_ref, kl_ref, vl_ref, o_ref, *,
                        window, n_heads, group, tq, tk, seq):
    del window, tk
    i = pl.program_id(1)
    is_lat = i < seq // tq
    hd = HEAD_DIM
    n_kv = n_heads // group
    rows = group * tq
    span = tq + 2 * WINDOW
    den_lane = [((j + 1) % n_kv) * hd for j in range(n_kv)]
    start = pl.multiple_of(jnp.clip(i * tq - WINDOW, 0, seq - span), WINDOW)
    delta = (lax.broadcasted_iota(jnp.int32, (tq, span), 0) - lax.broadcasted_iota(jnp.int32, (tq, span), 1)
             + (i * tq - start))
    bias = jnp.where(jnp.abs(delta) <= jnp.where(is_lat, WINDOW, -1), 0.0, NEG_INF)
    k_ctx, v_ctx = kc_ref[0], vc_ref[0]
    k_win, v_win = kl_ref[0, pl.ds(start, span), :], vl_ref[0, pl.ds(start, span), :]

    def with_ones(vblk, j):
        lane = lax.broadcasted_iota(jnp.int32, vblk.shape, 1)
        ones_col = jnp.where(lane == den_lane[j], 1.0, 0.0).astype(BF16)
        return jnp.where((lane >= j * hd) & (lane < (j + 1) * hd), vblk, ones_col)

    zeros = jnp.zeros((tq, hd), BF16)
    for j in range(n_kv):
        qs = []
        for g in range(group):
            h = j * group + g
            parts = [zeros] * n_kv
            parts[j] = q_ref[0, :, h * hd:(h + 1) * hd]
            qs.append(jnp.concatenate(parts, axis=1))
        q = jnp.concatenate(qs, axis=0)
        m0 = jnp.concatenate(
            [jnp.full((tq, 1), sink_ref[j * group + g] * LOG2E, F32) for g in range(group)], axis=0)
        s_ctx = _dot(q, k_ctx, trans_b=True)
        s_win = (_dot(q, k_win, trans_b=True).reshape(group, tq, span) + bias[None]).reshape(rows, span)
        m = jnp.maximum(m0, jnp.maximum(jnp.max(s_ctx, axis=-1, keepdims=True),
                                        jnp.max(s_win, axis=-1, keepdims=True)))
        acc = (_dot(jnp.exp2(s_ctx - m).astype(BF16), with_ones(v_ctx, j))
               + _dot(jnp.exp2(s_win - m).astype(BF16), with_ones(v_win, j)))
        den = acc[:, den_lane[j]:den_lane[j] + 1] + jnp.exp2(m0 - m)
        o = acc[:, j * hd:(j + 1) * hd] / den
        for g in range(group):
            c0 = (j * group + g) * hd
            o_ref[:, c0:c0 + hd] = o[g * tq:(g + 1) * tq].astype(BF16)


def _attention(qkv, sec, sink, cfg, window):
    rows = qkv.shape[1]
    b, seq, ctx, tq = cfg["batch"], cfg["seq"], cfg["ctx"], cfg["tq"]
    n_heads, n_kv = cfg["attn_heads"], cfg["attn_kv"]
    q_w = n_heads * HEAD_DIM
    kv_w = n_kv * HEAD_DIM
    assert kv_w == LANES and q_w % kv_w == 0 and ctx == tq
    n_lat_tiles = seq // tq
    k_blk = q_w // kv_w
    ctx_blk0 = b * seq // ctx

    def q_map(bi, i, *_):
        return (sec, jnp.where(i < n_lat_tiles, bi * n_lat_tiles + i, ctx_blk0 + bi), 0)

    def o_map(bi, i, *_):
        return (jnp.where(i < n_lat_tiles, bi * n_lat_tiles + i, ctx_blk0 + bi), 0)

    tk = cfg["tk"]
    assert seq % tk == 0 and tq % WINDOW == 0 and tq + 2 * WINDOW <= seq
    assert (sink is not None) == window
    kernel = functools.partial(_window_attn_kernel if window else _attn_kernel, window=window, n_heads=n_heads,
                               group=n_heads // n_kv, tq=tq, tk=tk, seq=seq)
    in_specs = [
        pl.BlockSpec((1, tq, q_w), q_map),
        pl.BlockSpec((1, ctx, kv_w), lambda bi, i, *_: (sec, ctx_blk0 + bi, k_blk)),
        pl.BlockSpec((1, ctx, kv_w), lambda bi, i, *_: (sec, ctx_blk0 + bi, k_blk + 1)),
        pl.BlockSpec((1, seq, kv_w), lambda bi, i, *_: (sec, bi, k_blk)),
        pl.BlockSpec((1, seq, kv_w), lambda bi, i, *_: (sec, bi, k_blk + 1)),
    ]
    args = [qkv, qkv, qkv, qkv, qkv]
    if sink is not None:
        in_specs = [pl.BlockSpec(memory_space=pltpu.SMEM)] + in_specs
        args = [sink] + args
    else:
        kernel = functools.partial(kernel, None)
    return pl.pallas_call(
        kernel,
        grid=(b, n_lat_tiles + 1),
        in_specs=in_specs,
        out_specs=pl.BlockSpec((tq, q_w), o_map),
        out_shape=jax.ShapeDtypeStruct((rows, q_w), BF16),
        compiler_params=_cparams(("parallel", "arbitrary")),
        name="window_attn" if window else "global_attn",
    )(*args)


def _tri3(tri, x):
    h1 = x.astype(BF16)
    r1 = x - h1.astype(F32)
    h2 = r1.astype(BF16)
    h3 = (r1 - h2.astype(F32)).astype(BF16)
    return _dot(tri, h1) + (_dot(tri, h2) + _dot(tri, h3))


def _rwkv_prep_kernel(cur_ref, prev_ref, next_ref, mu_ref, w0_ref, wup_ref, a0_ref, aup_ref, gup_ref,
                      kk_ref, ka_ref, rk_ref, e_ref, tril_ref, triu_ref,
                      sv_ref, sf_ref, sb_ref, etot_ref, bonus_ref, gate_ref, *,
                      bw, seg_lat, seg_ctx, n_lat_tiles):
    i = pl.program_id(0)
    p = cur_ref[...]
    tr = p.shape[0]
    lat = i < n_lat_tiles
    seg = jnp.where(lat, seg_lat, seg_ctx)
    pos = jnp.where(lat, i, i - n_lat_tiles) % seg
    has_prev = (pos != 0).astype(F32)
    has_next = (pos != seg - 1).astype(F32)
    row = lax.broadcasted_iota(jnp.int32, (tr, 1), 0)
    prev_row = prev_ref[7:8, :] * has_prev
    next_row = next_ref[0:1, :] * has_next
    prv = jnp.where(row == 0, prev_row, pltpu.roll(p, 1, 0))
    nxt = jnp.where(row == tr - 1, next_row, pltpu.roll(p, tr - 1, 0))
    xs = p + mu_ref[0:1, :] * (prv - p) + mu_ref[1:2, :] * (nxt - p)

    r = xs[:, 0:bw]
    k = xs[:, bw:2 * bw]
    v = xs[:, 2 * bw:3 * bw]
    lora = xs[:, 3 * bw:]
    n_hp = bw // LANES

    kk = k * kk_ref[...]
    kk = kk * lax.rsqrt(jnp.maximum(_group_sum(kk * kk, e_ref), 1e-24))
    ksum = jnp.zeros_like(k)
    for d, (s_ref, tri_ref) in enumerate(((sf_ref, tril_ref), (sb_ref, triu_ref))):
        wd = lora[:, d * LORA_PAD:(d + 1) * LORA_PAD]
        ad = lora[:, (2 + d) * LORA_PAD:(3 + d) * LORA_PAD]
        log_decay = -DECAY_SCALE * jax.nn.sigmoid(w0_ref[d:d + 1, :] + _dot3(jnp.tanh(wd), wup_ref[d]))
        a = jax.nn.sigmoid(a0_ref[d:d + 1, :] + _dot3(ad, aup_ref[d]))
        key = k * (1.0 + (a - 1.0) * ka_ref[...])
        ksum = ksum + key
        kka = kk * a
        cum = _tri3(tri_ref[...], log_decay)
        last = CHUNK - 1 if d == 0 else 0
        tot_rows = [cum[c * CHUNK + last:c * CHUNK + last + 1] for c in range(tr // CHUNK)]
        tot = jnp.concatenate([jnp.broadcast_to(t, (CHUNK, bw)) for t in tot_rows], axis=0)
        e_inv = jnp.exp(-cum)
        e_rel = jnp.exp(tot - cum)
        streams = (kk * jnp.exp(cum - log_decay), r * jnp.exp(cum), kka * e_inv, key * e_inv,
                   kka * e_rel, key * e_rel)
        for n, st in enumerate(streams):
            st = st.astype(BF16)
            for hp in range(n_hp):
                s_ref[n, hp] = st[:, hp * LANES:(hp + 1) * LANES]
        for cidx in range(tr // CHUNK):
            e_tot = jnp.exp(tot_rows[cidx])
            etot_ref[d, cidx] = jnp.concatenate(
                [e_tot[:, hp * LANES:(hp + 1) * LANES] for hp in range(n_hp)], axis=0)
    vb = v.astype(BF16)
    for hp in range(n_hp):
        sv_ref[hp] = vb[:, hp * LANES:(hp + 1) * LANES]
    bonus_ref[...] = _group_sum(r * ksum * rk_ref[...], e_ref) * v
    gate_ref[...] = _dot3(jax.nn.sigmoid(lora[:, 4 * LORA_PAD:]), gup_ref[...])


def _chunk_block_diag(tr, kind):
    t = jnp.arange(tr)
    same = (t[:, None] // CHUNK) == (t[None, :] // CHUNK)
    if kind == "lower":
        same = same & (t[None, :] <= t[:, None])
    elif kind == "upper":
        same = same & (t[None, :] >= t[:, None])
    return same.astype(BF16)


N_STREAMS = 6


def _rwkv_prep(p, lw, e_mat, cfg):
    rows = p.shape[0]
    bw, tr = cfg["b_width"], cfg["tr"]
    cw = 3 * bw + cfg["lora_w"]
    n_hp = bw // LANES
    n_lat_tiles = cfg["batch"] * cfg["seq"] // tr
    hb = tr // 8
    last8 = rows // 8 - 1
    full = lambda a: pl.BlockSpec(a.shape, lambda i: (0,) * a.ndim)
    consts = [lw["mu"], lw["w0"], lw["w_up"], lw["a0"], lw["a_up"], lw["g_up"], lw["k_k"], lw["k_a"], lw["r_k"],
              e_mat, _chunk_block_diag(tr, "lower"), _chunk_block_diag(tr, "upper")]
    stream_spec = pl.BlockSpec((N_STREAMS, n_hp, tr, LANES), lambda i: (0, 0, i, 0))
    stream_shape = jax.ShapeDtypeStruct((N_STREAMS, n_hp, rows, LANES), BF16)
    cpt = tr // CHUNK
    kernel = functools.partial(_rwkv_prep_kernel, bw=bw, seg_lat=cfg["seq"] // tr, seg_ctx=cfg["ctx"] // tr,
                               n_lat_tiles=n_lat_tiles)
    return pl.pallas_call(
        kernel,
        grid=(rows // tr,),
        in_specs=[
            pl.BlockSpec((tr, cw), lambda i: (i, 0)),
            pl.BlockSpec((8, cw), lambda i: (jnp.maximum(i * hb - 1, 0), 0)),
            pl.BlockSpec((8, cw), lambda i: (jnp.minimum((i + 1) * hb, last8), 0)),
        ] + [full(a) for a in consts],
        out_specs=[pl.BlockSpec((n_hp, tr, LANES), lambda i: (0, i, 0)),
                   stream_spec, stream_spec,
                   pl.BlockSpec((2, cpt, n_hp, LANES), lambda i: (0, i, 0, 0)),
                   pl.BlockSpec((tr, bw), lambda i: (i, 0)),
                   pl.BlockSpec((tr, bw), lambda i: (i, 0))],
        out_shape=[jax.ShapeDtypeStruct((n_hp, rows, LANES), BF16),
                   stream_shape, stream_shape,
                   jax.ShapeDtypeStruct((2, rows // CHUNK, n_hp, LANES), F32),
                   jax.ShapeDtypeStruct((rows, bw), F32),
                   jax.ShapeDtypeStruct((rows, bw), F32)],
        compiler_params=_cparams(("parallel",)),
        name="rwkv_prep",
    )(p, p, p, *consts)


def _lhs3(a):
    hi, lo = _split(a)
    return jnp.concatenate([hi, lo, hi], axis=1)


def _rhs3(b):
    hi, lo = _split(b)
    return jnp.concatenate([hi, hi, lo], axis=0)


def _block_diag(x):
    lo = lax.broadcasted_iota(jnp.int32, x.shape, 1) < HEAD_DIM
    zero = jnp.zeros_like(x)
    return jnp.concatenate([jnp.where(lo, x, zero), jnp.where(lo, zero, x)], axis=0)


def _scan_step(dirs, ones_ref, z_ref, n_hp):
    c, hd = CHUNK, HEAD_DIM
    assert c == hd and LANES == 2 * hd
    row = lax.broadcasted_iota(jnp.int32, (c, LANES), 0)
    lane = lax.broadcasted_iota(jnp.int32, (c, LANES), 1)
    col = lane & (hd - 1)
    lo = lane < hd
    eye = col == row
    inst = [(d, hp) for d in range(len(dirs)) for hp in range(n_hp)]
    pairs = range(len(inst))
    incl = [(col >= row) if dirs[d][4] else (col <= row) for d, _ in inst]
    strict = [(col > row) if dirs[d][4] else (col < row) for d, _ in inst]

    def half(x, h):
        keep = lo if h == 0 else ~lo
        return jnp.where(keep, x, jnp.zeros_like(x))

    qk, rt, bt, kt, bh, kh = ([dirs[d][1][k, hp] for d, hp in inst] for k in range(N_STREAMS))
    vv = [dirs[d][0][hp] for d, hp in inst]
    p1 = [_dot(jnp.concatenate([qk[i], rt[i]], axis=0),
               jnp.concatenate([half(bt[i], 0), half(bt[i], 1), half(kt[i], 0), half(kt[i], 1)], axis=0),
               trans_b=True) for i in pairs]
    same_blk = (row // SUB) == (col // SUB)
    a_ab = [jnp.where(strict[i], p[:c, :LANES], 0.0) for i, p in enumerate(p1)]
    x_pow = [jnp.where(same_blk, -a, 0.0) for a in a_ab]
    u = x_pow
    x_pow = [_dot(x.astype(BF16), _block_diag(x.astype(BF16))) for x in x_pow]
    sq = 2
    while sq < SUB:
        w = [_block_diag(x.astype(BF16)) for x in x_pow]
        if 2 * sq < SUB:
            prod = [_dot(jnp.concatenate([u[i], x_pow[i]], axis=0).astype(BF16), w[i]) for i in pairs]
            u = [u[i] + x_pow[i] + prod[i][:c] for i in pairs]
            x_pow = [pr[c:] for pr in prod]
        else:
            u = [u[i] + x_pow[i] + _dot(u[i].astype(BF16), w[i]) for i in pairs]
        sq *= 2
    u_bf = [x.astype(BF16) for x in u]
    n_off = [jnp.where(same_blk, 0.0, a) for a in a_ab]
    m1_bf = [(-(n_off[i] + _dot(u_bf[i], _block_diag(n_off[i].astype(BF16))))).astype(BF16) for i in pairs]
    m2_bf = [_dot(m1_bf[i], _block_diag(m1_bf[i])).astype(BF16) for i in pairs]
    avy = [_dot(jnp.concatenate([jnp.where(strict[i], p[:c, LANES:], 0.0), jnp.where(incl[i], p[c:, LANES:], 0.0)],
                                axis=0).astype(BF16), _block_diag(vv[i])) for i, p in enumerate(p1)]

    def pair_cols(x):
        return jnp.concatenate([_block_diag(x[:, :LANES].astype(BF16)), _block_diag(x[:, LANES:].astype(BF16))],
                               axis=1)

    r = [jnp.concatenate([qk[i].astype(F32), avy[i][:c]], axis=1) for i in pairs]
    r = [r[i] + _dot(u_bf[i], pair_cols(r[i])) for i in pairs]
    r = [r[i] + _dot(m2_bf[i], pair_cols(r[i])) for i in pairs]
    tu = [(r[i] + _dot(m1_bf[i], pair_cols(r[i]))).astype(BF16) for i in pairs]
    bu = [_dot(jnp.where(incl[i], p[c:, :LANES], 0.0).astype(BF16),
               jnp.concatenate([_block_diag(tu[i][:, :LANES]), _block_diag(tu[i][:, LANES:])], axis=1))
          for i, p in enumerate(p1)]
    rh = [(rt[i].astype(F32) - bu[i][:, :LANES]).astype(BF16) for i in pairs]
    yh = [avy[i][c:] - bu[i][:, LANES:] for i in pairs]
    rp = [_dot(bh[i], tu[i], trans_a=True) for i in pairs]
    kv = [_dot(kh[i], vv[i], trans_a=True) for i in pairs]
    g = [jnp.where(lo, r[:hd, :LANES], r[hd:, :LANES]).astype(BF16) for r in rp]
    hc = [jnp.where(lo, kv[i][:hd], kv[i][hd:]) - jnp.where(lo, rp[i][:hd, LANES:], rp[i][hd:, LANES:])
          for i in pairs]
    ones2 = jnp.concatenate([ones_ref[...], ones_ref[...]], axis=0)
    e_col = []
    for d, hp in inst:
        e_diag = jnp.where(eye, jnp.broadcast_to(dirs[d][2][hp:hp + 1, :], (c, LANES)), 0.0)
        e_col.append(_dot(jnp.concatenate(_split(e_diag), axis=1), ones2))
    z = [z_ref[i] for i in pairs]
    yz = [_dot(jnp.concatenate([rh[i], g[i]], axis=0), _block_diag(z[i].astype(BF16))) for i in pairs]
    for i, (d, hp) in enumerate(inst):
        dirs[d][3][hp] = yh[i] + yz[i][:c]
        z_ref[i] = e_col[i] * z[i] - yz[i][c:] + hc[i]


def _rwkv_scan_kernel(vf_ref, sf_ref, ef_ref, vb_ref, sb_ref, eb_ref, ones_ref, yf_ref, yb_ref, z_ref, *, n_hp):
    @pl.when(pl.program_id(1) == 0)
    def _():
        z_ref[...] = jnp.zeros_like(z_ref)

    _scan_step([(vf_ref, sf_ref, ef_ref, yf_ref, False), (vb_ref, sb_ref, eb_ref, yb_ref, True)],
               ones_ref, z_ref, n_hp)


def _rwkv_scan(s_v, s_f, s_b, e_tot, cfg):
    n_hp, rows, _ = s_v.shape
    b, seq, ctx = cfg["batch"], cfg["seq"], cfg["ctx"]
    c = CHUNK
    ncc, ncl = ctx // c, seq // c
    ctx0 = b * seq // c

    def fwd_blk(bi, j):
        return jnp.where(j < ncc, ctx0 + bi * ncc + j, bi * ncl + (j - ncc))

    def bwd_blk(bi, j):
        return jnp.where(j < ncc, ctx0 + bi * ncc + (ncc - 1 - j), bi * ncl + (ncl - 1 - (j - ncc)))

    def specs(blk, d):
        return [pl.BlockSpec((n_hp, c, LANES), lambda bi, j: (0, blk(bi, j), 0)),
                pl.BlockSpec((N_STREAMS, n_hp, c, LANES), lambda bi, j: (0, 0, blk(bi, j), 0)),
                pl.BlockSpec((None, None, n_hp, LANES), lambda bi, j: (d, blk(bi, j), 0, 0))]

    out_f = pl.BlockSpec((n_hp, c, LANES), lambda bi, j: (0, fwd_blk(bi, j), 0))
    out_b = pl.BlockSpec((n_hp, c, LANES), lambda bi, j: (0, bwd_blk(bi, j), 0))
    y_shape = jax.ShapeDtypeStruct((n_hp, rows, LANES), F32)
    return pl.pallas_call(
        functools.partial(_rwkv_scan_kernel, n_hp=n_hp),
        grid=(b, ncc + ncl),
        in_specs=specs(fwd_blk, 0) + specs(bwd_blk, 1) + [pl.BlockSpec((LANES, LANES), lambda bi, j: (0, 0))],
        out_specs=[out_f, out_b],
        out_shape=[y_shape, y_shape],
        scratch_shapes=[pltpu.VMEM((2 * n_hp, HEAD_DIM, LANES), F32)],
        compiler_params=_cparams(("parallel", "arbitrary")),
        name="rwkv_scan",
    )(s_v, s_f, e_tot, s_v, s_b, e_tot, _block_ones(LANES))


def _outproj_kernel(oa_ref, yf_ref, yb_ref, bonus_ref, gate_ref, gg_ref, gb_ref, e_ref, oc_ref, w_ref, h_ref,
                    g_ref, o_ref):
    n_hp = yf_ref.shape[0]
    y = jnp.concatenate([yf_ref[hp] + yb_ref[hp] for hp in range(n_hp)], axis=1)
    mu = _group_sum(y, e_ref) * (1.0 / HEAD_DIM)
    yc = y - mu
    var = _group_sum(yc * yc, e_ref) * (1.0 / HEAD_DIM)
    yn = yc * lax.rsqrt(var + GN_EPS) * gg_ref[...] + gb_ref[...]
    o_b = ((yn + bonus_ref[...]) * gate_ref[...]).astype(BF16)
    mix = jnp.concatenate([oa_ref[...], o_b, oc_ref[...]], axis=1)
    o_ref[...] = h_ref[...] + g_ref[...] * _dot(mix, w_ref[...])


def _outproj(o_a, y_f, y_b, bonus, gate, gn_g, gn_b, e_mat, o_c, w, h, mods, l, n_rows, cfg):
    d = h.shape[1]
    n_hp = y_f.shape[0]
    bw, tm = cfg["b_width"], cfg["tm"] // 2
    modrow = cfg["modrow"](tm)
    rows_of = lambda width: pl.BlockSpec((tm, width), lambda i: (i, 0))
    y_spec = pl.BlockSpec((n_hp, tm, LANES), lambda i: (0, i, 0))
    vec_spec = pl.BlockSpec((None, 1, bw), lambda i: (l, 0, 0))
    return pl.pallas_call(
        _outproj_kernel,
        grid=(n_rows // tm,),
        in_specs=[rows_of(o_a.shape[1]), y_spec, y_spec, rows_of(bw), rows_of(bw), vec_spec, vec_spec,
                  pl.BlockSpec(e_mat.shape, lambda i: (0, 0)),
                  rows_of(o_c.shape[1]),
                  pl.BlockSpec((None, d, d), lambda i: (l, 0, 0)),
                  rows_of(d),
                  pl.BlockSpec((None, None, None, 1, d), lambda i: (l, modrow(i), 2, 0, 0))],
        out_specs=rows_of(d),
        out_shape=jax.ShapeDtypeStruct((n_rows, d), F32),
        compiler_params=_cparams(("parallel",)),
        name="out_proj",
    )(o_a, y_f, y_b, bonus, gate, gn_g, gn_b, e_mat, o_c, w, h, mods)


def _mlp_kernel(h_ref, g_ref, sc_ref, sh_ref, hcol_ref, gate_ref, w1_ref, w2_ref, o_ref, u_scr, mid_scr, *, n_up):
    s = pl.program_id(1)

    @pl.when(s == 0)
    def _():
        u_scr[...] = _modulated_norm(h_ref[...], g_ref[...], sc_ref[...], sh_ref[...]).astype(BF16)

    @pl.when(s < n_up)
    def _():
        a = jnp.maximum(_dot(u_scr[...], w1_ref[...]), 0.0)
        mid_scr[s] = (a * a).astype(BF16)

    @pl.when(s >= n_up)
    def _():
        mid = jnp.concatenate([mid_scr[k] for k in range(n_up)], axis=1)
        o_ref[...] = hcol_ref[...] + gate_ref[...] * _dot(mid, w2_ref[...])


def _mlp(h, gain, mods, w1, w2, l, cfg):
    rows, d = h.shape
    dff = w1.shape[2]
    tm, tf, tn = cfg["tm"], 1024, 512
    n_up = dff // tf
    modrow = cfg["modrow"](tm)
    mod = lambda k: pl.BlockSpec((None, None, None, 1, d), lambda i, s: (l, modrow(i), k, 0, 0))
    col = lambda s: jnp.maximum(s - n_up, 0)
    return pl.pallas_call(
        functools.partial(_mlp_kernel, n_up=n_up),
        grid=(rows // tm, n_up + d // tn),
        in_specs=[
            pl.BlockSpec((tm, d), lambda i, s: (i, 0)),
            pl.BlockSpec((None, 1, d), lambda i, s: (l, 0, 0)),
            mod(4), mod(3),
            pl.BlockSpec((tm, tn), lambda i, s: (i, col(s))),
            pl.BlockSpec((None, None, None, 1, tn), lambda i, s: (l, modrow(i), 5, 0, col(s))),
            pl.BlockSpec((None, d, tf), lambda i, s: (l, 0, jnp.minimum(s, n_up - 1))),
            pl.BlockSpec((None, dff, tn), lambda i, s: (l, 0, col(s))),
        ],
        out_specs=pl.BlockSpec((tm, tn), lambda i, s: (i, col(s))),
        out_shape=jax.ShapeDtypeStruct((rows, d), F32),
        scratch_shapes=[pltpu.VMEM((tm, d), BF16), pltpu.VMEM((n_up, tm, tf), BF16)],
        compiler_params=_cparams(("parallel", "arbitrary")),
        name="mlp",
    )(h, gain, mods, mods, h, mods, w1, w2)


def _rope_tables(seq, tm):
    rows = seq // GRID_W
    row = jnp.broadcast_to(jnp.arange(rows)[:, None], (rows, GRID_W)).reshape(-1)
    col = jnp.broadcast_to(jnp.arange(GRID_W)[None, :], (rows, GRID_W)).reshape(-1)
    n_freq = HEAD_DIM // 4
    inv = ROPE_THETA ** (-jnp.arange(n_freq, dtype=F32) / n_freq)
    ang = jnp.concatenate([row[:, None].astype(F32) * inv, col[:, None].astype(F32) * inv], -1)
    cos, sin = jnp.cos(ang), jnp.sin(ang)
    reps = LANES // HEAD_DIM
    cos_t = jnp.tile(jnp.concatenate([cos, cos], -1), (1, reps))
    sin_t = jnp.tile(jnp.concatenate([-sin, sin], -1), (1, reps))
    cos_t = jnp.concatenate([cos_t, jnp.ones((tm, LANES), F32)], 0)
    sin_t = jnp.concatenate([sin_t, jnp.zeros((tm, LANES), F32)], 0)
    return cos_t, sin_t


def _block_ones(width):
    g = jnp.arange(width) // HEAD_DIM
    return (g[:, None] == g[None, :]).astype(BF16)


def kernel(x, c, ctx, c_ctx, ada_w, ada_b, norm1_g, norm2_g, w_in, a_q_norm, a_k_norm, a_sink, c_q_norm,
           c_k_norm, shift_mu, decay_w0, decay_up, iclr_a0, iclr_up, gate_up, k_k, k_a, r_k, gn_g, gn_b,
           w_out, mlp_w1, mlp_w2):
    batch, seq, d = x.shape
    n_ctx = ctx.shape[1]
    depth = ada_w.shape[0]
    bw = k_k.shape[1]
    lora_d, lora_i, lora_g = decay_up.shape[2], iclr_up.shape[2], gate_up.shape[1]
    a_heads = a_sink.shape[1]
    a_kv = a_heads // 4
    q_w, kv_w = a_heads * HEAD_DIM, a_kv * HEAD_DIM
    attn_w = q_w + 2 * kv_w
    lora_w = 4 * LORA_PAD + lora_g
    b_in = 3 * bw + 2 * lora_d + 2 * lora_i + lora_g
    assert lora_d <= LORA_PAD and lora_i <= LORA_PAD
    assert w_in.shape[2] == 2 * attn_w + b_in

    tm = batch * n_ctx
    n_lat_tiles = batch * seq // tm
    tiles_per_batch = seq // tm
    cfg = dict(
        batch=batch, seq=seq, ctx=n_ctx, tm=tm, tr=n_ctx, tq=n_ctx, tk=min(seq, 2048),
        b_width=bw, lora_w=lora_w, attn_w=attn_w, attn_col0=3 * bw + lora_w, qk_w=q_w + kv_w,
        attn_heads=a_heads, attn_kv=a_kv,
        modrow=lambda t: (lambda i: jnp.where(i < batch * seq // t, i // (seq // t), batch)),
        rope_blk=lambda i: jnp.where(i < n_lat_tiles, i % tiles_per_batch, tiles_per_batch),
    )
    assert seq % tm == 0 and seq % GRID_W == 0 and n_ctx % CHUNK == 0 and (3 * bw) % lora_w == 0
    assert cfg["attn_col0"] % attn_w == 0 and batch + 1 <= 8

    def relayout_cols(m):
        a_part, b_part, c_part = m[..., :attn_w], m[..., attn_w:attn_w + b_in], m[..., attn_w + b_in:]
        pad = lambda z, n: jnp.pad(z, [(0, 0)] * (z.ndim - 1) + [(0, n - z.shape[-1])])
        o = 3 * bw
        pieces = [b_part[..., :o]]
        for width in (lora_d, lora_d, lora_i, lora_i):
            pieces.append(pad(b_part[..., o:o + width], LORA_PAD))
            o += width
        pieces.append(b_part[..., o:])
        return jnp.concatenate(pieces + [a_part, c_part], -1)

    w_in_r = relayout_cols(w_in).astype(BF16)
    mu_r = relayout_cols(jnp.pad(shift_mu, ((0, 0), (0, 0), (attn_w, attn_w))))[..., :3 * bw + lora_w]
    pad_rows = lambda z: jnp.pad(z, ((0, 0), (0, 0), (0, LORA_PAD - z.shape[2]), (0, 0)))
    w_up_r, a_up_r = pad_rows(decay_up), pad_rows(iclr_up)
    w_out_b = w_out.astype(BF16)
    w1_b, w2_b = mlp_w1.astype(BF16), mlp_w2.astype(BF16)

    scale = HEAD_DIM ** -0.5 * LOG2E
    tile = lambda g, n: jnp.tile(g, (1, n))

    def gains(qg, kg):
        return jnp.concatenate([tile(qg, a_heads) * scale, tile(kg, a_kv), jnp.ones((depth, kv_w), F32)], -1)

    qk_gains = jnp.stack([gains(a_q_norm, a_k_norm), gains(c_q_norm, c_k_norm)], 1)[:, :, None, :]

    cos_t, sin_t = _rope_tables(seq, tm)
    e_attn = _block_ones(attn_w)
    e_b = _block_ones(2 * LANES)

    cvec = jnp.zeros((8, d), F32).at[:batch].set(c).at[batch].set(c_ctx)
    mods = _mods(cvec, ada_w, ada_b).reshape(depth, 8, 6, 1, d)

    h = jnp.concatenate([x.reshape(batch * seq, d), ctx.reshape(batch * n_ctx, d)], 0)
    n1g, n2g = norm1_g[:, None, :], norm2_g[:, None, :]
    gn_g3, gn_b3 = gn_g[:, None, :], gn_b[:, None, :]
    for l in range(depth):
        p = _inproj(h, n1g, mods, w_in_r, l, cfg)
        qkv = _qkprep(p, qk_gains[l], cos_t, sin_t, e_attn, cfg)
        o_a = _attention(qkv, 0, a_sink[l], cfg, window=True)
        o_c = _attention(qkv, 1, None, cfg, window=False)
        lw = dict(mu=mu_r[l], w0=decay_w0[l], w_up=w_up_r[l], a0=iclr_a0[l], a_up=a_up_r[l], g_up=gate_up[l],
                  k_k=k_k[l][None], k_a=k_a[l][None], r_k=r_k[l][None])
        s_v, s_f, s_b, e_tot, bonus, gate = _rwkv_prep(p, lw, e_b, cfg)
        y_f, y_b = _rwkv_scan(s_v, s_f, s_b, e_tot, cfg)
        n_rows = batch * seq if l == depth - 1 else h.shape[0]
        h = _outproj(o_a, y_f, y_b, bonus, gate, gn_g3, gn_b3, e_b, o_c, w_out_b, h, mods, l, n_rows, cfg)
        h = _mlp(h, n2g, mods, w1_b, w2_b, l, cfg)
    return h.reshape(batch, seq, d)
```

```python
import functools

import jax
import jax.numpy as jnp
from jax import lax
from jax.experimental import pallas as pl
from jax.experimental.pallas import tpu as pltpu

F32 = jnp.float32
BF16 = jnp.bfloat16

HEAD_DIM = 64
GRID_W = 64
WINDOW = 128
ROPE_THETA = 10000.0
NORM_EPS = 1e-6
GN_EPS = 64e-5
NEG_INF = -1e30
LOG2E = 1.4426950408889634
DECAY_SCALE = 0.6065306597126334
LANES = 128
SUBLANES = 8
LORA_PAD = LANES
SUB = 16
CHUNK = 64
VMEM_LIMIT = 52 * 1024 * 1024


def _dot(a, b, trans_a=False, trans_b=False):
    dn = (((0 if trans_a else 1,), (1 if trans_b else 0,)), ((), ()))
    return lax.dot_general(a, b, dn, preferred_element_type=F32)


def _split(x):
    hi = x.astype(BF16)
    lo = (x - hi.astype(F32)).astype(BF16)
    return hi, lo


def _dot3(a, b, trans_a=False, trans_b=False):
    ah, al = _split(a)
    bh, bl = _split(b)
    kw = dict(trans_a=trans_a, trans_b=trans_b)
    return _dot(ah, bh, **kw) + (_dot(al, bh, **kw) + _dot(ah, bl, **kw))


def _group_sum(x, e_ref):
    gw = e_ref.shape[0]
    e = e_ref[...]
    outs = []
    for g in range(x.shape[1] // gw):
        hi, lo = _split(x[:, g * gw:(g + 1) * gw])
        outs.append(_dot(hi, e) + _dot(lo, e))
    return outs[0] if len(outs) == 1 else jnp.concatenate(outs, axis=1)


def _cparams(sem):
    return pltpu.CompilerParams(dimension_semantics=sem, vmem_limit_bytes=VMEM_LIMIT)


def _mods_kernel(c_ref, w_ref, b_ref, o_ref):
    c = c_ref[...]
    s = c * jax.nn.sigmoid(c)
    o_ref[0] = _dot3(s, w_ref[0]) + b_ref[0]


def _mods(cvec, ada_w, ada_b):
    depth, d, n = ada_w.shape
    tn = 512
    return pl.pallas_call(
        _mods_kernel,
        grid=(depth, n // tn),
        in_specs=[
            pl.BlockSpec((SUBLANES, d), lambda l, j: (0, 0)),
            pl.BlockSpec((1, d, tn), lambda l, j: (l, 0, j)),
            pl.BlockSpec((1, 1, tn), lambda l, j: (l, 0, j)),
        ],
        out_specs=pl.BlockSpec((1, SUBLANES, tn), lambda l, j: (l, 0, j)),
        out_shape=jax.ShapeDtypeStruct((depth, SUBLANES, n), F32),
        compiler_params=_cparams(("parallel", "parallel")),
        name="adaln_mods",
    )(cvec, ada_w, ada_b.reshape(depth, 1, n))


def _modulated_norm(x, g, sc, sh):
    ms = jnp.mean(x * x, axis=-1, keepdims=True)
    return (x * lax.rsqrt(ms + NORM_EPS) * g) * (1.0 + sc) + sh


def _inproj_kernel(h_ref, g_ref, sc_ref, sh_ref, w_ref, o_ref, u_scr):
    @pl.when(pl.program_id(1) == 0)
    def _():
        u_scr[...] = _modulated_norm(h_ref[...], g_ref[...], sc_ref[...], sh_ref[...]).astype(BF16)

    o_ref[...] = _dot(u_scr[...], w_ref[...])


def _inproj(h, gain, mods, w, l, cfg):
    rows, d = h.shape
    n = w.shape[2]
    tm, tn = cfg["tm"], 2688
    modrow = cfg["modrow"](tm)
    return pl.pallas_call(
        _inproj_kernel,
        grid=(rows // tm, n // tn),
        in_specs=[
            pl.BlockSpec((tm, d), lambda i, j: (i, 0)),
            pl.BlockSpec((None, 1, d), lambda i, j: (l, 0, 0)),
            pl.BlockSpec((None, None, None, 1, d), lambda i, j: (l, modrow(i), 1, 0, 0)),
            pl.BlockSpec((None, None, None, 1, d), lambda i, j: (l, modrow(i), 0, 0, 0)),
            pl.BlockSpec((None, d, tn), lambda i, j: (l, 0, j)),
        ],
        out_specs=pl.BlockSpec((tm, tn), lambda i, j: (i, j)),
        out_shape=jax.ShapeDtypeStruct((rows, n), F32),
        scratch_shapes=[pltpu.VMEM((tm, d), BF16)],
        compiler_params=_cparams(("parallel", "arbitrary")),
        name="in_proj",
    )(h, gain, mods, mods, w)


def _qkprep_kernel(p_ref, gain_ref, cos_ref, sin_ref, e_ref, o_ref, *, qk_w):
    x = p_ref[...]
    width = x.shape[1]
    ss = _group_sum(x * x, e_ref)
    y = x * lax.rsqrt(ss * (1.0 / HEAD_DIM) + NORM_EPS) * gain_ref[0]
    cos = cos_ref[...]
    sin = sin_ref[...]
    lane = lax.broadcasted_iota(jnp.int32, (x.shape[0], LANES), 1)
    first_half = (lane & (HEAD_DIM // 2)) == 0
    for g in range(width // LANES):
        sl = slice(g * LANES, (g + 1) * LANES)
        if g * LANES < qk_w:
            yg = y[:, sl]
            partner = jnp.where(first_half, pltpu.roll(yg, LANES - HEAD_DIM // 2, 1),
                                pltpu.roll(yg, HEAD_DIM // 2, 1))
            o_ref[0, :, sl] = (yg * cos + partner * sin).astype(BF16)
        else:
            o_ref[0, :, sl] = x[:, sl].astype(BF16)


def _qkprep(p, gains, cos_t, sin_t, e_mat, cfg):
    rows = p.shape[0]
    tm, aw = cfg["tm"], cfg["attn_w"]
    first_blk = cfg["attn_col0"] // aw
    rope_blk = cfg["rope_blk"]
    return pl.pallas_call(
        functools.partial(_qkprep_kernel, qk_w=cfg["qk_w"]),
        grid=(rows // tm, 2),
        in_specs=[
            pl.BlockSpec((tm, aw), lambda i, s: (i, first_blk + s)),
            pl.BlockSpec((1, 1, aw), lambda i, s: (s, 0, 0)),
            pl.BlockSpec((tm, LANES), lambda i, s: (rope_blk(i), 0)),
            pl.BlockSpec((tm, LANES), lambda i, s: (rope_blk(i), 0)),
            pl.BlockSpec(e_mat.shape, lambda i, s: (0, 0)),
        ],
        out_specs=pl.BlockSpec((1, tm, aw), lambda i, s: (s, i, 0)),
        out_shape=jax.ShapeDtypeStruct((2, rows, aw), BF16),
        compiler_params=_cparams(("parallel", "parallel")),
        name="qk_prep",
    )(p, gains, cos_t, sin_t, e_mat)


def _attn_kernel(sink_ref, q_ref, kc_ref, vc_ref, kl_ref, vl_ref, o_ref, *,
                 window, n_heads, group, tq, tk, seq):
    i = pl.program_id(1)
    is_lat = i < seq // tq
    hd = HEAD_DIM
    n_kv = n_heads // group
    rows = group * tq
    den_lane = [((j + 1) % n_kv) * hd for j in range(n_kv)]
    if window:
        span = tq + 2 * WINDOW
        start = pl.multiple_of(jnp.clip(i * tq - WINDOW, 0, seq - span), WINDOW)
        delta = (lax.broadcasted_iota(jnp.int32, (tq, span), 0) - lax.broadcasted_iota(jnp.int32, (tq, span), 1)
                 + (i * tq - start))
        bias = jnp.where(jnp.abs(delta) <= WINDOW, 0.0, NEG_INF)
        n_iter = jnp.where(is_lat, 1, 0)
    else:
        span = tk
        n_iter = jnp.where(is_lat, seq // tk, 0)

    def with_ones(vblk, j):
        lane = lax.broadcasted_iota(jnp.int32, vblk.shape, 1)
        ones_col = jnp.where(lane == den_lane[j], 1.0, 0.0).astype(BF16)
        return jnp.where((lane >= j * hd) & (lane < (j + 1) * hd), vblk, ones_col)

    zeros = jnp.zeros((tq, hd), BF16)
    acc_lane = lax.broadcasted_iota(jnp.int32, (rows, LANES), 1)
    for j in range(n_kv):
        qs = []
        for g in range(group):
            h = j * group + g
            parts = [zeros] * n_kv
            parts[j] = q_ref[0, :, h * hd:(h + 1) * hd]
            qs.append(jnp.concatenate(parts, axis=1))
        q = jnp.concatenate(qs, axis=0)
        if sink_ref is not None:
            m0 = jnp.concatenate(
                [jnp.full((tq, 1), sink_ref[j * group + g] * LOG2E, F32) for g in range(group)], axis=0)
            acc0 = jnp.where(acc_lane == den_lane[j], 1.0, 0.0)
        else:
            m0 = jnp.full((rows, 1), NEG_INF, F32)
            acc0 = jnp.zeros((rows, LANES), F32)

        s = _dot(q, kc_ref[0], trans_b=True)
        m = jnp.maximum(m0, jnp.max(s, axis=-1, keepdims=True))
        p = jnp.exp2(s - m).astype(BF16)
        acc = jnp.exp2(m0 - m) * acc0 + _dot(p, with_ones(vc_ref[0], j))

        def body(kb, carry, q=q, j=j):
            m, acc = carry
            off = start if window else pl.multiple_of(kb * tk, tk)
            s = _dot(q, kl_ref[0, pl.ds(off, span), :], trans_b=True)
            if window:
                s = (s.reshape(group, tq, span) + bias[None]).reshape(rows, span)
            m_new = jnp.maximum(m, jnp.max(s, axis=-1, keepdims=True))
            p = jnp.exp2(s - m_new).astype(BF16)
            acc_new = jnp.exp2(m - m_new) * acc + _dot(p, with_ones(vl_ref[0, pl.ds(off, span), :], j))
            return m_new, acc_new

        m, acc = lax.fori_loop(0, n_iter, body, (m, acc))
        o = acc[:, j * hd:(j + 1) * hd] / acc[:, den_lane[j]:den_lane[j] + 1]
        for g in range(group):
            c0 = (j * group + g) * hd
            o_ref[:, c0:c0 + hd] = o[g * tq:(g + 1) * tq].astype(BF16)


def _window_attn_kernel(sink_ref, q_ref, kc_ref, vc_ref, kl_ref, vl_ref, o_ref, *,
                        window, n_heads, group, tq, tk, seq):
    del window, tk
    i = pl.program_id(1)
    is_lat = i < seq // tq
    hd = HEAD_DIM
    n_kv = n_heads // group
    rows = group * tq
    span = tq + 2 * WINDOW
    den_lane = [((j + 1) % n_kv) * hd for j in range(n_kv)]
    start = pl.multiple_of(jnp.clip(i * tq - WINDOW, 0, seq - span), WINDOW)
    delta = (lax.broadcasted_iota(jnp.int32, (tq, span), 0) - lax.broadcasted_iota(jnp.int32, (tq, span), 1)
             + (i * tq - start))
    bias = jnp.where(jnp.abs(delta) <= jnp.where(is_lat, WINDOW, -1), 0.0, NEG_INF)
    k_ctx, v_ctx = kc_ref[0], vc_ref[0]
    k_win, v_win = kl_ref[0, pl.ds(start, span), :], vl_ref[0, pl.ds(start, span), :]

    def with_ones(vblk, j):
        lane = lax.broadcasted_iota(jnp.int32, vblk.shape, 1)
        ones_col = jnp.where(lane == den_lane[j], 1.0, 0.0).astype(BF16)
        return jnp.where((lane >= j * hd) & (lane < (j + 1) * hd), vblk, ones_col)

    zeros = jnp.zeros((tq, hd), BF16)
    for j in range(n_kv):
        qs = []
        for g in range(group):
            h = j * group + g
            parts = [zeros] * n_kv
            parts[j] = q_ref[0, :, h * hd:(h + 1) * hd]
            qs.append(jnp.concatenate(parts, axis=1))
        q = jnp.concatenate(qs, axis=0)
        m0 = jnp.concatenate(
            [jnp.full((tq, 1), sink_ref[j * group + g] * LOG2E, F32) for g in range(group)], axis=0)
        s_ctx = _dot(q, k_ctx, trans_b=True)
        s_win = (_dot(q, k_win, trans_b=True).reshape(group, tq, span) + bias[None]).reshape(rows, span)
        m = jnp.maximum(m0, jnp.maximum(jnp.max(s_ctx, axis=-1, keepdims=True),
                                        jnp.max(s_win, axis=-1, keepdims=True)))
        acc = (_dot(jnp.exp2(s_ctx - m).astype(BF16), with_ones(v_ctx, j))
               + _dot(jnp.exp2(s_win - m).astype(BF16), with_ones(v_win, j)))
        den = acc[:, den_lane[j]:den_lane[j] + 1] + jnp.exp2(m0 - m)
        o = acc[:, j * hd:(j + 1) * hd] / den
        for g in range(group):
            c0 = (j * group + g) * hd
            o_ref[:, c0:c0 + hd] = o[g * tq:(g + 1) * tq].astype(BF16)


def _attention(qkv, sec, sink, cfg, window):
    rows = qkv.shape[1]
    b, seq, ctx, tq = cfg["batch"], cfg["seq"], cfg["ctx"], cfg["tq"]
    n_heads, n_kv = cfg["attn_heads"], cfg["attn_kv"]
    q_w = n_heads * HEAD_DIM
    kv_w = n_kv * HEAD_DIM
    assert kv_w == LANES and q_w % kv_w == 0 and ctx == tq
    n_lat_tiles = seq // tq
    k_blk = q_w // kv_w
    ctx_blk0 = b * seq // ctx

    def q_map(bi, i, *_):
        return (sec, jnp.where(i < n_lat_tiles, bi * n_lat_tiles + i, ctx_blk0 + bi), 0)

    def o_map(bi, i, *_):
        return (jnp.where(i < n_lat_tiles, bi * n_lat_tiles + i, ctx_blk0 + bi), 0)

    tk = cfg["tk"]
    assert seq % tk == 0 and tq % WINDOW == 0 and tq + 2 * WINDOW <= seq
    assert (sink is not None) == window
    kernel = functools.partial(_window_attn_kernel if window else _attn_kernel, window=window, n_heads=n_heads,
                               group=n_heads // n_kv, tq=tq, tk=tk, seq=seq)
    in_specs = [
        pl.BlockSpec((1, tq, q_w), q_map),
        pl.BlockSpec((1, ctx, kv_w), lambda bi, i, *_: (sec, ctx_blk0 + bi, k_blk)),
        pl.BlockSpec((1, ctx, kv_w), lambda bi, i, *_: (sec, ctx_blk0 + bi, k_blk + 1)),
        pl.BlockSpec((1, seq, kv_w), lambda bi, i, *_: (sec, bi, k_blk)),
        pl.BlockSpec((1, seq, kv_w), lambda bi, i, *_: (sec, bi, k_blk + 1)),
    ]
    args = [qkv, qkv, qkv, qkv, qkv]
    if sink is not None:
        in_specs = [pl.BlockSpec(memory_space=pltpu.SMEM)] + in_specs
        args = [sink] + args
    else:
        kernel = functools.partial(kernel, None)
    return pl.pallas_call(
        kernel,
        grid=(b, n_lat_tiles + 1),
        in_specs=in_specs,
        out_specs=pl.BlockSpec((tq, q_w), o_map),
        out_shape=jax.ShapeDtypeStruct((rows, q_w), BF16),
        compiler_params=_cparams(("parallel", "arbitrary")),
        name="window_attn" if window else "global_attn",
    )(*args)


def _tri3(tri, x):
    h1 = x.astype(BF16)
    r1 = x - h1.astype(F32)
    h2 = r1.astype(BF16)
    h3 = (r1 - h2.astype(F32)).astype(BF16)
    return _dot(tri, h1) + (_dot(tri, h2) + _dot(tri, h3))


def _rwkv_prep_kernel(cur_ref, prev_ref, next_ref, mu_ref, w0_ref, wup_ref, a0_ref, aup_ref, gup_ref,
                      kk_ref, ka_ref, rk_ref, e_ref, tril_ref, triu_ref,
                      sv_ref, sf_ref, sb_ref, etot_ref, bonus_ref, gate_ref, *,
                      bw, seg_lat, seg_ctx, n_lat_tiles):
    i = pl.program_id(0)
    p = cur_ref[...]
    tr = p.shape[0]
    lat = i < n_lat_tiles
    seg = jnp.where(lat, seg_lat, seg_ctx)
    pos = jnp.where(lat, i, i - n_lat_tiles) % seg
    row = lax.broadcasted_iota(jnp.int32, (tr, 1), 0)
    prev_row = jnp.where(pos != 0, prev_ref[SUBLANES - 1:SUBLANES, :], 0.0)
    next_row = jnp.where(pos != seg - 1, next_ref[0:1, :], 0.0)
    prv = jnp.where(row == 0, prev_row, pltpu.roll(p, 1, 0))
    nxt = jnp.where(row == tr - 1, next_row, pltpu.roll(p, tr - 1, 0))
    xs = p + mu_ref[0:1, :] * (prv - p) + mu_ref[1:2, :] * (nxt - p)

    r = xs[:, 0:bw]
    k = xs[:, bw:2 * bw]
    v = xs[:, 2 * bw:3 * bw]
    lora = xs[:, 3 * bw:]
    n_hp = bw // LANES

    kk = k * kk_ref[...]
    kk = kk * lax.rsqrt(jnp.maximum(_group_sum(kk * kk, e_ref), 1e-24))
    ksum = jnp.zeros_like(k)
    for d, (s_ref, tri_ref) in enumerate(((sf_ref, tril_ref), (sb_ref, triu_ref))):
        wd = lora[:, d * LORA_PAD:(d + 1) * LORA_PAD]
        ad = lora[:, (2 + d) * LORA_PAD:(3 + d) * LORA_PAD]
        log_decay = -DECAY_SCALE * jax.nn.sigmoid(w0_ref[d:d + 1, :] + _dot3(jnp.tanh(wd), wup_ref[d]))
        a = jax.nn.sigmoid(a0_ref[d:d + 1, :] + _dot3(ad, aup_ref[d]))
        key = k * (1.0 + (a - 1.0) * ka_ref[...])
        ksum = ksum + key
        kka = kk * a
        cum = _tri3(tri_ref[...], log_decay)
        last = CHUNK - 1 if d == 0 else 0
        tot_rows = [cum[c * CHUNK + last:c * CHUNK + last + 1] for c in range(tr // CHUNK)]
        tot = jnp.concatenate([jnp.broadcast_to(t, (CHUNK, bw)) for t in tot_rows], axis=0)
        e_inv = jnp.exp(-cum)
        e_rel = jnp.exp(tot - cum)
        streams = (kk * jnp.exp(cum - log_decay), r * jnp.exp(cum), kka * e_inv, key * e_inv,
                   kka * e_rel, key * e_rel)
        for n, st in enumerate(streams):
            st = st.astype(BF16)
            for hp in range(n_hp):
                s_ref[n, hp] = st[:, hp * LANES:(hp + 1) * LANES]
        for cidx in range(tr // CHUNK):
            e_tot = jnp.exp(tot_rows[cidx])
            etot_ref[d, cidx] = jnp.concatenate(
                [e_tot[:, hp * LANES:(hp + 1) * LANES] for hp in range(n_hp)], axis=0)
    vb = v.astype(BF16)
    for hp in range(n_hp):
        sv_ref[hp] = vb[:, hp * LANES:(hp + 1) * LANES]
    bonus_ref[...] = _group_sum(r * ksum * rk_ref[...], e_ref) * v
    gate_ref[...] = _dot3(jax.nn.sigmoid(lora[:, 4 * LORA_PAD:]), gup_ref[...])


def _chunk_block_diag(tr, kind):
    t = jnp.arange(tr)
    same = (t[:, None] // CHUNK) == (t[None, :] // CHUNK)
    if kind == "lower":
        same = same & (t[None, :] <= t[:, None])
    elif kind == "upper":
        same = same & (t[None, :] >= t[:, None])
    return same.astype(BF16)


N_STREAMS = 6


def _rwkv_prep(p, lw, e_mat, cfg):
    rows = p.shape[0]
    bw, tr = cfg["b_width"], cfg["tr"]
    cw = 3 * bw + cfg["lora_w"]
    n_hp = bw // LANES
    n_lat_tiles = cfg["batch"] * cfg["seq"] // tr
    hb = tr // SUBLANES
    last8 = rows // SUBLANES - 1
    full = lambda a: pl.BlockSpec(a.shape, lambda i: (0,) * a.ndim)
    consts = [lw["mu"], lw["w0"], lw["w_up"], lw["a0"], lw["a_up"], lw["g_up"], lw["k_k"], lw["k_a"], lw["r_k"],
              e_mat, _chunk_block_diag(tr, "lower"), _chunk_block_diag(tr, "upper")]
    stream_spec = pl.BlockSpec((N_STREAMS, n_hp, tr, LANES), lambda i: (0, 0, i, 0))
    stream_shape = jax.ShapeDtypeStruct((N_STREAMS, n_hp, rows, LANES), BF16)
    cpt = tr // CHUNK
    kernel = functools.partial(_rwkv_prep_kernel, bw=bw, seg_lat=cfg["seq"] // tr, seg_ctx=cfg["ctx"] // tr,
                               n_lat_tiles=n_lat_tiles)
    return pl.pallas_call(
        kernel,
        grid=(rows // tr,),
        in_specs=[
            pl.BlockSpec((tr, cw), lambda i: (i, 0)),
            pl.BlockSpec((SUBLANES, cw), lambda i: (jnp.maximum(i * hb - 1, 0), 0)),
            pl.BlockSpec((SUBLANES, cw), lambda i: (jnp.minimum((i + 1) * hb, last8), 0)),
        ] + [full(a) for a in consts],
        out_specs=[pl.BlockSpec((n_hp, tr, LANES), lambda i: (0, i, 0)),
                   stream_spec, stream_spec,
                   pl.BlockSpec((2, cpt, n_hp, LANES), lambda i: (0, i, 0, 0)),
                   pl.BlockSpec((tr, bw), lambda i: (i, 0)),
                   pl.BlockSpec((tr, bw), lambda i: (i, 0))],
        out_shape=[jax.ShapeDtypeStruct((n_hp, rows, LANES), BF16),
                   stream_shape, stream_shape,
                   jax.ShapeDtypeStruct((2, rows // CHUNK, n_hp, LANES), F32),
                   jax.ShapeDtypeStruct((rows, bw), F32),
                   jax.ShapeDtypeStruct((rows, bw), F32)],
        compiler_params=_cparams(("parallel",)),
        name="rwkv_prep",
    )(p, p, p, *consts)


def _block_diag(x):
    lo = lax.broadcasted_iota(jnp.int32, x.shape, 1) < HEAD_DIM
    zero = jnp.zeros_like(x)
    return jnp.concatenate([jnp.where(lo, x, zero), jnp.where(lo, zero, x)], axis=0)


def _scan_step(dirs, ones_ref, z_ref, n_hp):
    c, hd = CHUNK, HEAD_DIM
    assert c == hd and LANES == 2 * hd
    row = lax.broadcasted_iota(jnp.int32, (c, LANES), 0)
    lane = lax.broadcasted_iota(jnp.int32, (c, LANES), 1)
    col = lane & (hd - 1)
    lo = lane < hd
    eye = col == row
    inst = [(d, hp) for d in range(len(dirs)) for hp in range(n_hp)]
    pairs = range(len(inst))
    incl = [(col >= row) if dirs[d][4] else (col <= row) for d, _ in inst]
    strict = [(col > row) if dirs[d][4] else (col < row) for d, _ in inst]

    def half(x, h):
        keep = lo if h == 0 else ~lo
        return jnp.where(keep, x, jnp.zeros_like(x))

    qk, rt, bt, kt, bh, kh = ([dirs[d][1][k, hp] for d, hp in inst] for k in range(N_STREAMS))
    vv = [dirs[d][0][hp] for d, hp in inst]
    p1 = [_dot(jnp.concatenate([qk[i], rt[i]], axis=0),
               jnp.concatenate([half(bt[i], 0), half(bt[i], 1), half(kt[i], 0), half(kt[i], 1)], axis=0),
               trans_b=True) for i in pairs]
    same_blk = (row // SUB) == (col // SUB)
    a_ab = [jnp.where(strict[i], p[:c, :LANES], 0.0) for i, p in enumerate(p1)]
    x_pow = [jnp.where(same_blk, -a, 0.0) for a in a_ab]
    u = x_pow
    x_pow = [_dot(x.astype(BF16), _block_diag(x.astype(BF16))) for x in x_pow]
    sq = 2
    while sq < SUB:
        w = [_block_diag(x.astype(BF16)) for x in x_pow]
        if 2 * sq < SUB:
            prod = [_dot(jnp.concatenate([u[i], x_pow[i]], axis=0).astype(BF16), w[i]) for i in pairs]
            u = [u[i] + x_pow[i] + prod[i][:c] for i in pairs]
            x_pow = [pr[c:] for pr in prod]
        else:
            u = [u[i] + x_pow[i] + _dot(u[i].astype(BF16), w[i]) for i in pairs]
        sq *= 2
    u_bf = [x.astype(BF16) for x in u]
    n_off = [jnp.where(same_blk, 0.0, a) for a in a_ab]
    m1_bf = [(-(n_off[i] + _dot(u_bf[i], _block_diag(n_off[i].astype(BF16))))).astype(BF16) for i in pairs]
    m2_bf = [_dot(m1_bf[i], _block_diag(m1_bf[i])).astype(BF16) for i in pairs]
    avy = [_dot(jnp.concatenate([jnp.where(strict[i], p[:c, LANES:], 0.0), jnp.where(incl[i], p[c:, LANES:], 0.0)],
                                axis=0).astype(BF16), _block_diag(vv[i])) for i, p in enumerate(p1)]

    def pair_cols(x):
        return jnp.concatenate([_block_diag(x[:, :LANES].astype(BF16)), _block_diag(x[:, LANES:].astype(BF16))],
                               axis=1)

    r = [jnp.concatenate([qk[i].astype(F32), avy[i][:c]], axis=1) for i in pairs]
    r = [r[i] + _dot(u_bf[i], pair_cols(r[i])) for i in pairs]
    r = [r[i] + _dot(m2_bf[i], pair_cols(r[i])) for i in pairs]
    tu = [(r[i] + _dot(m1_bf[i], pair_cols(r[i]))).astype(BF16) for i in pairs]
    bu = [_dot(jnp.where(incl[i], p[c:, :LANES], 0.0).astype(BF16),
               jnp.concatenate([_block_diag(tu[i][:, :LANES]), _block_diag(tu[i][:, LANES:])], axis=1))
          for i, p in enumerate(p1)]
    rh = [(rt[i].astype(F32) - bu[i][:, :LANES]).astype(BF16) for i in pairs]
    yh = [avy[i][c:] - bu[i][:, LANES:] for i in pairs]
    rp = [_dot(bh[i], tu[i], trans_a=True) for i in pairs]
    kv = [_dot(kh[i], vv[i], trans_a=True) for i in pairs]
    g = [jnp.where(lo, r[:hd, :LANES], r[hd:, :LANES]).astype(BF16) for r in rp]
    hc = [jnp.where(lo, kv[i][:hd], kv[i][hd:]) - jnp.where(lo, rp[i][:hd, LANES:], rp[i][hd:, LANES:])
          for i in pairs]
    ones2 = jnp.concatenate([ones_ref[...], ones_ref[...]], axis=0)
    e_col = []
    for d, hp in inst:
        e_diag = jnp.where(eye, jnp.broadcast_to(dirs[d][2][hp:hp + 1, :], (c, LANES)), 0.0)
        e_col.append(_dot(jnp.concatenate(_split(e_diag), axis=1), ones2))
    z = [z_ref[i] for i in pairs]
    yz = [_dot(jnp.concatenate([rh[i], g[i]], axis=0), _block_diag(z[i].astype(BF16))) for i in pairs]
    for i, (d, hp) in enumerate(inst):
        dirs[d][3][hp] = yh[i] + yz[i][:c]
        z_ref[i] = e_col[i] * z[i] - yz[i][c:] + hc[i]


def _rwkv_scan_kernel(vf_ref, sf_ref, ef_ref, vb_ref, sb_ref, eb_ref, ones_ref, yf_ref, yb_ref, z_ref, *, n_hp):
    @pl.when(pl.program_id(1) == 0)
    def _():
        z_ref[...] = jnp.zeros_like(z_ref)

    _scan_step([(vf_ref, sf_ref, ef_ref, yf_ref, False), (vb_ref, sb_ref, eb_ref, yb_ref, True)],
               ones_ref, z_ref, n_hp)


def _rwkv_scan(s_v, s_f, s_b, e_tot, cfg):
    n_hp, rows, _ = s_v.shape
    b, seq, ctx = cfg["batch"], cfg["seq"], cfg["ctx"]
    c = CHUNK
    ncc, ncl = ctx // c, seq // c
    ctx0 = b * seq // c

    def fwd_blk(bi, j):
        return jnp.where(j < ncc, ctx0 + bi * ncc + j, bi * ncl + (j - ncc))

    def bwd_blk(bi, j):
        return jnp.where(j < ncc, ctx0 + bi * ncc + (ncc - 1 - j), bi * ncl + (ncl - 1 - (j - ncc)))

    def specs(blk, d):
        return [pl.BlockSpec((n_hp, c, LANES), lambda bi, j: (0, blk(bi, j), 0)),
                pl.BlockSpec((N_STREAMS, n_hp, c, LANES), lambda bi, j: (0, 0, blk(bi, j), 0)),
                pl.BlockSpec((None, None, n_hp, LANES), lambda bi, j: (d, blk(bi, j), 0, 0))]

    out_f = pl.BlockSpec((n_hp, c, LANES), lambda bi, j: (0, fwd_blk(bi, j), 0))
    out_b = pl.BlockSpec((n_hp, c, LANES), lambda bi, j: (0, bwd_blk(bi, j), 0))
    y_shape = jax.ShapeDtypeStruct((n_hp, rows, LANES), F32)
    return pl.pallas_call(
        functools.partial(_rwkv_scan_kernel, n_hp=n_hp),
        grid=(b, ncc + ncl),
        in_specs=specs(fwd_blk, 0) + specs(bwd_blk, 1) + [pl.BlockSpec((LANES, LANES), lambda bi, j: (0, 0))],
        out_specs=[out_f, out_b],
        out_shape=[y_shape, y_shape],
        scratch_shapes=[pltpu.VMEM((2 * n_hp, HEAD_DIM, LANES), F32)],
        compiler_params=_cparams(("parallel", "arbitrary")),
        name="rwkv_scan",
    )(s_v, s_f, e_tot, s_v, s_b, e_tot, _block_ones(LANES))


def _outproj_kernel(oa_ref, yf_ref, yb_ref, bonus_ref, gate_ref, gg_ref, gb_ref, e_ref, oc_ref, w_ref, h_ref,
                    g_ref, o_ref):
    n_hp = yf_ref.shape[0]
    y = jnp.concatenate([yf_ref[hp] + yb_ref[hp] for hp in range(n_hp)], axis=1)
    mu = _group_sum(y, e_ref) * (1.0 / HEAD_DIM)
    yc = y - mu
    var = _group_sum(yc * yc, e_ref) * (1.0 / HEAD_DIM)
    yn = yc * lax.rsqrt(var + GN_EPS) * gg_ref[...] + gb_ref[...]
    o_b = ((yn + bonus_ref[...]) * gate_ref[...]).astype(BF16)
    mix = jnp.concatenate([oa_ref[...], o_b, oc_ref[...]], axis=1)
    o_ref[...] = h_ref[...] + g_ref[...] * _dot(mix, w_ref[...])


def _outproj(o_a, y_f, y_b, bonus, gate, gn_g, gn_b, e_mat, o_c, w, h, mods, l, n_rows, cfg):
    d = h.shape[1]
    n_hp = y_f.shape[0]
    bw, tm = cfg["b_width"], cfg["tm"] // 2
    modrow = cfg["modrow"](tm)
    rows_of = lambda width: pl.BlockSpec((tm, width), lambda i: (i, 0))
    y_spec = pl.BlockSpec((n_hp, tm, LANES), lambda i: (0, i, 0))
    vec_spec = pl.BlockSpec((None, 1, bw), lambda i: (l, 0, 0))
    return pl.pallas_call(
        _outproj_kernel,
        grid=(n_rows // tm,),
        in_specs=[rows_of(o_a.shape[1]), y_spec, y_spec, rows_of(bw), rows_of(bw), vec_spec, vec_spec,
                  pl.BlockSpec(e_mat.shape, lambda i: (0, 0)),
                  rows_of(o_c.shape[1]),
                  pl.BlockSpec((None, d, d), lambda i: (l, 0, 0)),
                  rows_of(d),
                  pl.BlockSpec((None, None, None, 1, d), lambda i: (l, modrow(i), 2, 0, 0))],
        out_specs=rows_of(d),
        out_shape=jax.ShapeDtypeStruct((n_rows, d), F32),
        compiler_params=_cparams(("parallel",)),
        name="out_proj",
    )(o_a, y_f, y_b, bonus, gate, gn_g, gn_b, e_mat, o_c, w, h, mods)


def _mlp_kernel(h_ref, g_ref, sc_ref, sh_ref, hcol_ref, gate_ref, w1_ref, w2_ref, o_ref, u_scr, mid_scr, *, n_up):
    s = pl.program_id(1)

    @pl.when(s == 0)
    def _():
        u_scr[...] = _modulated_norm(h_ref[...], g_ref[...], sc_ref[...], sh_ref[...]).astype(BF16)

    @pl.when(s < n_up)
    def _():
        a = jnp.maximum(_dot(u_scr[...], w1_ref[...]), 0.0)
        mid_scr[s] = (a * a).astype(BF16)

    @pl.when(s >= n_up)
    def _():
        mid = jnp.concatenate([mid_scr[k] for k in range(n_up)], axis=1)
        o_ref[...] = hcol_ref[...] + gate_ref[...] * _dot(mid, w2_ref[...])


def _mlp(h, gain, mods, w1, w2, l, cfg):
    rows, d = h.shape
    dff = w1.shape[2]
    tm, tf, tn = cfg["tm"], 1024, 512
    n_up = dff // tf
    modrow = cfg["modrow"](tm)
    mod = lambda k: pl.BlockSpec((None, None, None, 1, d), lambda i, s: (l, modrow(i), k, 0, 0))
    col = lambda s: jnp.maximum(s - n_up, 0)
    return pl.pallas_call(
        functools.partial(_mlp_kernel, n_up=n_up),
        grid=(rows // tm, n_up + d // tn),
        in_specs=[
            pl.BlockSpec((tm, d), lambda i, s: (i, 0)),
            pl.BlockSpec((None, 1, d), lambda i, s: (l, 0, 0)),
            mod(4), mod(3),
            pl.BlockSpec((tm, tn), lambda i, s: (i, col(s))),
            pl.BlockSpec((None, None, None, 1, tn), lambda i, s: (l, modrow(i), 5, 0, col(s))),
            pl.BlockSpec((None, d, tf), lambda i, s: (l, 0, jnp.minimum(s, n_up - 1))),
            pl.BlockSpec((None, dff, tn), lambda i, s: (l, 0, col(s))),
        ],
        out_specs=pl.BlockSpec((tm, tn), lambda i, s: (i, col(s))),
        out_shape=jax.ShapeDtypeStruct((rows, d), F32),
        scratch_shapes=[pltpu.VMEM((tm, d), BF16), pltpu.VMEM((n_up, tm, tf), BF16)],
        compiler_params=_cparams(("parallel", "arbitrary")),
        name="mlp",
    )(h, gain, mods, mods, h, mods, w1, w2)


def _rope_tables(seq, tm):
    rows = seq // GRID_W
    row = jnp.broadcast_to(jnp.arange(rows)[:, None], (rows, GRID_W)).reshape(-1)
    col = jnp.broadcast_to(jnp.arange(GRID_W)[None, :], (rows, GRID_W)).reshape(-1)
    n_freq = HEAD_DIM // 4
    inv = ROPE_THETA ** (-jnp.arange(n_freq, dtype=F32) / n_freq)
    ang = jnp.concatenate([row[:, None].astype(F32) * inv, col[:, None].astype(F32) * inv], -1)
    cos, sin = jnp.cos(ang), jnp.sin(ang)
    reps = LANES // HEAD_DIM
    cos_t = jnp.tile(jnp.concatenate([cos, cos], -1), (1, reps))
    sin_t = jnp.tile(jnp.concatenate([-sin, sin], -1), (1, reps))
    cos_t = jnp.concatenate([cos_t, jnp.ones((tm, LANES), F32)], 0)
    sin_t = jnp.concatenate([sin_t, jnp.zeros((tm, LANES), F32)], 0)
    return cos_t, sin_t


def _block_ones(width):
    g = jnp.arange(width) // HEAD_DIM
    return (g[:, None] == g[None, :]).astype(BF16)


def kernel(x, c, ctx, c_ctx, ada_w, ada_b, norm1_g, norm2_g, w_in, a_q_norm, a_k_norm, a_sink, c_q_norm,
           c_k_norm, shift_mu, decay_w0, decay_up, iclr_a0, iclr_up, gate_up, k_k, k_a, r_k, gn_g, gn_b,
           w_out, mlp_w1, mlp_w2):
    batch, seq, d = x.shape
    n_ctx = ctx.shape[1]
    depth = ada_w.shape[0]
    bw = k_k.shape[1]
    lora_d, lora_i, lora_g = decay_up.shape[2], iclr_up.shape[2], gate_up.shape[1]
    a_heads = a_sink.shape[1]
    a_kv = a_heads // 4
    q_w, kv_w = a_heads * HEAD_DIM, a_kv * HEAD_DIM
    attn_w = q_w + 2 * kv_w
    lora_w = 4 * LORA_PAD + lora_g
    b_in = 3 * bw + 2 * lora_d + 2 * lora_i + lora_g
    assert lora_d <= LORA_PAD and lora_i <= LORA_PAD
    assert w_in.shape[2] == 2 * attn_w + b_in

    tm = batch * n_ctx
    n_lat_tiles = batch * seq // tm
    tiles_per_batch = seq // tm
    cfg = dict(
        batch=batch, seq=seq, ctx=n_ctx, tm=tm, tr=n_ctx, tq=n_ctx, tk=min(seq, 2048),
        b_width=bw, lora_w=lora_w, attn_w=attn_w, attn_col0=3 * bw + lora_w, qk_w=q_w + kv_w,
        attn_heads=a_heads, attn_kv=a_kv,
        modrow=lambda t: (lambda i: jnp.where(i < batch * seq // t, i // (seq // t), batch)),
        rope_blk=lambda i: jnp.where(i < n_lat_tiles, i % tiles_per_batch, tiles_per_batch),
    )
    assert seq % tm == 0 and seq % GRID_W == 0 and n_ctx % CHUNK == 0 and (3 * bw) % lora_w == 0
    assert cfg["attn_col0"] % attn_w == 0 and batch + 1 <= SUBLANES

    def relayout_cols(m):
        a_part, b_part, c_part = m[..., :attn_w], m[..., attn_w:attn_w + b_in], m[..., attn_w + b_in:]
        pad = lambda z, n: jnp.pad(z, [(0, 0)] * (z.ndim - 1) + [(0, n - z.shape[-1])])
        o = 3 * bw
        pieces = [b_part[..., :o]]
        for width in (lora_d, lora_d, lora_i, lora_i):
            pieces.append(pad(b_part[..., o:o + width], LORA_PAD))
            o += width
        pieces.append(b_part[..., o:])
        return jnp.concatenate(pieces + [a_part, c_part], -1)

    w_in_r = relayout_cols(w_in).astype(BF16)
    mu_r = relayout_cols(jnp.pad(shift_mu, ((0, 0), (0, 0), (attn_w, attn_w))))[..., :3 * bw + lora_w]
    pad_rows = lambda z: jnp.pad(z, ((0, 0), (0, 0), (0, LORA_PAD - z.shape[2]), (0, 0)))
    w_up_r, a_up_r = pad_rows(decay_up), pad_rows(iclr_up)
    w_out_b = w_out.astype(BF16)
    w1_b, w2_b = mlp_w1.astype(BF16), mlp_w2.astype(BF16)

    scale = HEAD_DIM ** -0.5 * LOG2E
    tile = lambda g, n: jnp.tile(g, (1, n))

    def gains(qg, kg):
        return jnp.concatenate([tile(qg, a_heads) * scale, tile(kg, a_kv), jnp.ones((depth, kv_w), F32)], -1)

    qk_gains = jnp.stack([gains(a_q_norm, a_k_norm), gains(c_q_norm, c_k_norm)], 1)[:, :, None, :]

    cos_t, sin_t = _rope_tables(seq, tm)
    e_attn = _block_ones(attn_w)
    e_b = _block_ones(2 * LANES)

    cvec = jnp.zeros((SUBLANES, d), F32).at[:batch].set(c).at[batch].set(c_ctx)
    mods = _mods(cvec, ada_w, ada_b).reshape(depth, SUBLANES, 6, 1, d)

    h = jnp.concatenate([x.reshape(batch * seq, d), ctx.reshape(batch * n_ctx, d)], 0)
    n1g, n2g = norm1_g[:, None, :], norm2_g[:, None, :]
    gn_g3, gn_b3 = gn_g[:, None, :], gn_b[:, None, :]
    for l in range(depth):
        p = _inproj(h, n1g, mods, w_in_r, l, cfg)
        qkv = _qkprep(p, qk_gains[l], cos_t, sin_t, e_attn, cfg)
        o_a = _attention(qkv, 0, a_sink[l], cfg, window=True)
        o_c = _attention(qkv, 1, None, cfg, window=False)
        lw = dict(mu=mu_r[l], w0=decay_w0[l], w_up=w_up_r[l], a0=iclr_a0[l], a_up=a_up_r[l], g_up=gate_up[l],
                  k_k=k_k[l][None], k_a=k_a[l][None], r_k=r_k[l][None])
        s_v, s_f, s_b, e_tot, bonus, gate = _rwkv_prep(p, lw, e_b, cfg)
        y_f, y_b = _rwkv_scan(s_v, s_f, s_b, e_tot, cfg)
        n_rows = batch * seq if l == depth - 1 else h.shape[0]
        h = _outproj(o_a, y_f, y_b, bonus, gate, gn_g3, gn_b3, e_b, o_c, w_out_b, h, mods, l, n_rows, cfg)
        h = _mlp(h, n2g, mods, w1_b, w2_b, l, cfg)
    return h.reshape(batch, seq, d)
```

```python
import functools

import jax
import jax.numpy as jnp
from jax import lax
from jax.experimental import pallas as pl
from jax.experimental.pallas import tpu as pltpu

F32 = jnp.float32
BF16 = jnp.bfloat16

HEAD_DIM = 64
GRID_W = 64
WINDOW = 128
ROPE_THETA = 10000.0
NORM_EPS = 1e-6
GN_EPS = 64e-5
NEG_INF = -1e30
LOG2E = 1.4426950408889634
DECAY_SCALE = 0.6065306597126334
LANES = 128
SUBLANES = 8
LORA_PAD = LANES
SUB = 16
CHUNK = 64
VMEM_LIMIT = 52 * 1024 * 1024


def _dot(a, b, trans_a=False, trans_b=False):
    dn = (((0 if trans_a else 1,), (1 if trans_b else 0,)), ((), ()))
    return lax.dot_general(a, b, dn, preferred_element_type=F32)


def _split(x):
    hi = x.astype(BF16)
    lo = (x - hi.astype(F32)).astype(BF16)
    return hi, lo


def _dot3(a, b, trans_a=False, trans_b=False):
    ah, al = _split(a)
    bh, bl = _split(b)
    kw = dict(trans_a=trans_a, trans_b=trans_b)
    return _dot(ah, bh, **kw) + (_dot(al, bh, **kw) + _dot(ah, bl, **kw))


def _group_sum(x, e_ref):
    gw = e_ref.shape[0]
    e = e_ref[...]
    outs = []
    for g in range(x.shape[1] // gw):
        hi, lo = _split(x[:, g * gw:(g + 1) * gw])
        outs.append(_dot(hi, e) + _dot(lo, e))
    return outs[0] if len(outs) == 1 else jnp.concatenate(outs, axis=1)


def _cparams(sem):
    return pltpu.CompilerParams(dimension_semantics=sem, vmem_limit_bytes=VMEM_LIMIT)


def _mods_kernel(c_ref, w_ref, b_ref, o_ref):
    c = c_ref[...]
    s = c * jax.nn.sigmoid(c)
    o_ref[0] = _dot3(s, w_ref[0]) + b_ref[0]


def _mods(cvec, ada_w, ada_b):
    depth, d, n = ada_w.shape
    tn = 512
    return pl.pallas_call(
        _mods_kernel,
        grid=(depth, n // tn),
        in_specs=[
            pl.BlockSpec((SUBLANES, d), lambda l, j: (0, 0)),
            pl.BlockSpec((1, d, tn), lambda l, j: (l, 0, j)),
            pl.BlockSpec((1, 1, tn), lambda l, j: (l, 0, j)),
        ],
        out_specs=pl.BlockSpec((1, SUBLANES, tn), lambda l, j: (l, 0, j)),
        out_shape=jax.ShapeDtypeStruct((depth, SUBLANES, n), F32),
        compiler_params=_cparams(("parallel", "parallel")),
        name="adaln_mods",
    )(cvec, ada_w, ada_b.reshape(depth, 1, n))


def _modulated_norm(x, g, sc, sh):
    ms = jnp.mean(x * x, axis=-1, keepdims=True)
    return (x * lax.rsqrt(ms + NORM_EPS) * g) * (1.0 + sc) + sh


def _inproj_kernel(h_ref, g_ref, sc_ref, sh_ref, w_ref, o_ref, u_scr):
    @pl.when(pl.program_id(1) == 0)
    def _():
        u_scr[...] = _modulated_norm(h_ref[...], g_ref[...], sc_ref[...], sh_ref[...]).astype(BF16)

    o_ref[...] = _dot(u_scr[...], w_ref[...])


def _inproj(h, gain, mods, w, l, cfg):
    rows, d = h.shape
    n = w.shape[2]
    tm, tn = cfg["tm"], 2688
    modrow = cfg["modrow"](tm)
    return pl.pallas_call(
        _inproj_kernel,
        grid=(rows // tm, n // tn),
        in_specs=[
            pl.BlockSpec((tm, d), lambda i, j: (i, 0)),
            pl.BlockSpec((None, 1, d), lambda i, j: (l, 0, 0)),
            pl.BlockSpec((None, None, None, 1, d), lambda i, j: (l, modrow(i), 1, 0, 0)),
            pl.BlockSpec((None, None, None, 1, d), lambda i, j: (l, modrow(i), 0, 0, 0)),
            pl.BlockSpec((None, d, tn), lambda i, j: (l, 0, j)),
        ],
        out_specs=pl.BlockSpec((tm, tn), lambda i, j: (i, j)),
        out_shape=jax.ShapeDtypeStruct((rows, n), F32),
        scratch_shapes=[pltpu.VMEM((tm, d), BF16)],
        compiler_params=_cparams(("parallel", "arbitrary")),
        name="in_proj",
    )(h, gain, mods, mods, w)


def _qkprep_kernel(p_ref, gain_ref, cos_ref, sin_ref, e_ref, o_ref, *, qk_w):
    x = p_ref[...]
    width = x.shape[1]
    ss = _group_sum(x * x, e_ref)
    y = x * lax.rsqrt(ss * (1.0 / HEAD_DIM) + NORM_EPS) * gain_ref[0]
    cos = cos_ref[...]
    sin = sin_ref[...]
    lane = lax.broadcasted_iota(jnp.int32, (x.shape[0], LANES), 1)
    first_half = (lane & (HEAD_DIM // 2)) == 0
    for g in range(width // LANES):
        sl = slice(g * LANES, (g + 1) * LANES)
        if g * LANES < qk_w:
            yg = y[:, sl]
            partner = jnp.where(first_half, pltpu.roll(yg, LANES - HEAD_DIM // 2, 1),
                                pltpu.roll(yg, HEAD_DIM // 2, 1))
            o_ref[0, :, sl] = (yg * cos + partner * sin).astype(BF16)
        else:
            o_ref[0, :, sl] = x[:, sl].astype(BF16)


def _qkprep(p, gains, cos_t, sin_t, e_mat, cfg):
    rows = p.shape[0]
    tm, aw = cfg["tm"], cfg["attn_w"]
    first_blk = cfg["attn_col0"] // aw
    rope_blk = cfg["rope_blk"]
    return pl.pallas_call(
        functools.partial(_qkprep_kernel, qk_w=cfg["qk_w"]),
        grid=(rows // tm, 2),
        in_specs=[
            pl.BlockSpec((tm, aw), lambda i, s: (i, first_blk + s)),
            pl.BlockSpec((1, 1, aw), lambda i, s: (s, 0, 0)),
            pl.BlockSpec((tm, LANES), lambda i, s: (rope_blk(i), 0)),
            pl.BlockSpec((tm, LANES), lambda i, s: (rope_blk(i), 0)),
            pl.BlockSpec(e_mat.shape, lambda i, s: (0, 0)),
        ],
        out_specs=pl.BlockSpec((1, tm, aw), lambda i, s: (s, i, 0)),
        out_shape=jax.ShapeDtypeStruct((2, rows, aw), BF16),
        compiler_params=_cparams(("parallel", "parallel")),
        name="qk_prep",
    )(p, gains, cos_t, sin_t, e_mat)


def _attn_kernel(sink_ref, q_ref, kc_ref, vc_ref, kl_ref, vl_ref, o_ref, *,
                 window, n_heads, group, tq, tk, seq):
    i = pl.program_id(1)
    is_lat = i < seq // tq
    hd = HEAD_DIM
    n_kv = n_heads // group
    rows = group * tq
    den_lane = [((j + 1) % n_kv) * hd for j in range(n_kv)]
    if window:
        span = tq + 2 * WINDOW
        start = pl.multiple_of(jnp.clip(i * tq - WINDOW, 0, seq - span), WINDOW)
        delta = (lax.broadcasted_iota(jnp.int32, (tq, span), 0) - lax.broadcasted_iota(jnp.int32, (tq, span), 1)
                 + (i * tq - start))
        bias = jnp.where(jnp.abs(delta) <= WINDOW, 0.0, NEG_INF)
        n_iter = jnp.where(is_lat, 1, 0)
    else:
        span = tk
        n_iter = jnp.where(is_lat, seq // tk, 0)

    def with_ones(vblk, j):
        lane = lax.broadcasted_iota(jnp.int32, vblk.shape, 1)
        ones_col = jnp.where(lane == den_lane[j], 1.0, 0.0).astype(BF16)
        return jnp.where((lane >= j * hd) & (lane < (j + 1) * hd), vblk, ones_col)

    zeros = jnp.zeros((tq, hd), BF16)
    acc_lane = lax.broadcasted_iota(jnp.int32, (rows, LANES), 1)
    for j in range(n_kv):
        qs = []
        for g in range(group):
            h = j * group + g
            parts = [zeros] * n_kv
            parts[j] = q_ref[0, :, h * hd:(h + 1) * hd]
            qs.append(jnp.concatenate(parts, axis=1))
        q = jnp.concatenate(qs, axis=0)
        if sink_ref is not None:
            m0 = jnp.concatenate(
                [jnp.full((tq, 1), sink_ref[j * group + g] * LOG2E, F32) for g in range(group)], axis=0)
            acc0 = jnp.where(acc_lane == den_lane[j], 1.0, 0.0)
        else:
            m0 = jnp.full((rows, 1), NEG_INF, F32)
            acc0 = jnp.zeros((rows, LANES), F32)

        s = _dot(q, kc_ref[0], trans_b=True)
        m = jnp.maximum(m0, jnp.max(s, axis=-1, keepdims=True))
        p = jnp.exp2(s - m).astype(BF16)
        acc = jnp.exp2(m0 - m) * acc0 + _dot(p, with_ones(vc_ref[0], j))

        def body(kb, carry, q=q, j=j):
            m, acc = carry
            off = start if window else pl.multiple_of(kb * tk, tk)
            s = _dot(q, kl_ref[0, pl.ds(off, span), :], trans_b=True)
            if window:
                s = (s.reshape(group, tq, span) + bias[None]).reshape(rows, span)
            m_new = jnp.maximum(m, jnp.max(s, axis=-1, keepdims=True))
            p = jnp.exp2(s - m_new).astype(BF16)
            acc_new = jnp.exp2(m - m_new) * acc + _dot(p, with_ones(vl_ref[0, pl.ds(off, span), :], j))
            return m_new, acc_new

        m, acc = lax.fori_loop(0, n_iter, body, (m, acc))
        o = acc[:, j * hd:(j + 1) * hd] / acc[:, den_lane[j]:den_lane[j] + 1]
        for g in range(group):
            c0 = (j * group + g) * hd
            o_ref[:, c0:c0 + hd] = o[g * tq:(g + 1) * tq].astype(BF16)


def _window_attn_kernel(sink_ref, q_ref, kc_ref, vc_ref, kl_ref, vl_ref, o_ref, *,
                        window, n_heads, group, tq, tk, seq):
    del window, tk
    i = pl.program_id(1)
    is_lat = i < seq // tq
    hd = HEAD_DIM
    n_kv = n_heads // group
    rows = group * tq
    span = tq + 2 * WINDOW
    den_lane = [((j + 1) % n_kv) * hd for j in range(n_kv)]
    start = pl.multiple_of(jnp.clip(i * tq - WINDOW, 0, seq - span), WINDOW)
    delta = (lax.broadcasted_iota(jnp.int32, (tq, span), 0) - lax.broadcasted_iota(jnp.int32, (tq, span), 1)
             + (i * tq - start))
    bias = jnp.where(jnp.abs(delta) <= jnp.where(is_lat, WINDOW, -1), 0.0, NEG_INF)
    k_ctx, v_ctx = kc_ref[0], vc_ref[0]
    k_win, v_win = kl_ref[0, pl.ds(start, span), :], vl_ref[0, pl.ds(start, span), :]

    def with_ones(vblk, j):
        lane = lax.broadcasted_iota(jnp.int32, vblk.shape, 1)
        ones_col = jnp.where(lane == den_lane[j], 1.0, 0.0).astype(BF16)
        return jnp.where((lane >= j * hd) & (lane < (j + 1) * hd), vblk, ones_col)

    zeros = jnp.zeros((tq, hd), BF16)
    for j in range(n_kv):
        qs = []
        for g in range(group):
            h = j * group + g
            parts = [zeros] * n_kv
            parts[j] = q_ref[0, :, h * hd:(h + 1) * hd]
            qs.append(jnp.concatenate(parts, axis=1))
        q = jnp.concatenate(qs, axis=0)
        m0 = jnp.concatenate(
            [jnp.full((tq, 1), sink_ref[j * group + g] * LOG2E, F32) for g in range(group)], axis=0)
        s_ctx = _dot(q, k_ctx, trans_b=True)
        s_win = (_dot(q, k_win, trans_b=True).reshape(group, tq, span) + bias[None]).reshape(rows, span)
        m = jnp.maximum(m0, jnp.maximum(jnp.max(s_ctx, axis=-1, keepdims=True),
                                        jnp.max(s_win, axis=-1, keepdims=True)))
        acc = (_dot(jnp.exp2(s_ctx - m).astype(BF16), with_ones(v_ctx, j))
               + _dot(jnp.exp2(s_win - m).astype(BF16), with_ones(v_win, j)))
        den = acc[:, den_lane[j]:den_lane[j] + 1] + jnp.exp2(m0 - m)
        o = acc[:, j * hd:(j + 1) * hd] / den
        for g in range(group):
            c0 = (j * group + g) * hd
            o_ref[:, c0:c0 + hd] = o[g * tq:(g + 1) * tq].astype(BF16)


def _attention(qkv, sec, sink, cfg, window):
    rows = qkv.shape[1]
    b, seq, ctx, tq = cfg["batch"], cfg["seq"], cfg["ctx"], cfg["tq"]
    n_heads, n_kv = cfg["attn_heads"], cfg["attn_kv"]
    q_w = n_heads * HEAD_DIM
    kv_w = n_kv * HEAD_DIM
    assert kv_w == LANES and q_w % kv_w == 0 and ctx == tq
    n_lat_tiles = seq // tq
    k_blk = q_w // kv_w
    ctx_blk0 = b * seq // ctx

    def q_map(bi, i, *_):
        return (sec, jnp.where(i < n_lat_tiles, bi * n_lat_tiles + i, ctx_blk0 + bi), 0)

    def o_map(bi, i, *_):
        return (jnp.where(i < n_lat_tiles, bi * n_lat_tiles + i, ctx_blk0 + bi), 0)

    tk = cfg["tk"]
    assert seq % tk == 0 and tq % WINDOW == 0 and tq + 2 * WINDOW <= seq
    assert (sink is not None) == window
    kernel = functools.partial(_window_attn_kernel if window else _attn_kernel, window=window, n_heads=n_heads,
                               group=n_heads // n_kv, tq=tq, tk=tk, seq=seq)
    in_specs = [
        pl.BlockSpec((1, tq, q_w), q_map),
        pl.BlockSpec((1, ctx, kv_w), lambda bi, i, *_: (sec, ctx_blk0 + bi, k_blk)),
        pl.BlockSpec((1, ctx, kv_w), lambda bi, i, *_: (sec, ctx_blk0 + bi, k_blk + 1)),
        pl.BlockSpec((1, seq, kv_w), lambda bi, i, *_: (sec, bi, k_blk)),
        pl.BlockSpec((1, seq, kv_w), lambda bi, i, *_: (sec, bi, k_blk + 1)),
    ]
    args = [qkv, qkv, qkv, qkv, qkv]
    if sink is not None:
        in_specs = [pl.BlockSpec(memory_space=pltpu.SMEM)] + in_specs
        args = [sink] + args
    else:
        kernel = functools.partial(kernel, None)
    return pl.pallas_call(
        kernel,
        grid=(b, n_lat_tiles + 1),
        in_specs=in_specs,
        out_specs=pl.BlockSpec((tq, q_w), o_map),
        out_shape=jax.ShapeDtypeStruct((rows, q_w), BF16),
        compiler_params=_cparams(("parallel", "arbitrary")),
        name="window_attn" if window else "global_attn",
    )(*args)


def _tri3(tri, x):
    h1 = x.astype(BF16)
    r1 = x - h1.astype(F32)
    h2 = r1.astype(BF16)
    h3 = (r1 - h2.astype(F32)).astype(BF16)
    return _dot(tri, h1) + (_dot(tri, h2) + _dot(tri, h3))


def _rwkv_prep_kernel(cur_ref, prev_ref, next_ref, mu_ref, w0_ref, wup_ref, a0_ref, aup_ref, gup_ref,
                      kk_ref, ka_ref, rk_ref, e_ref, tril_ref, triu_ref,
                      sv_ref, sf_ref, sb_ref, etot_ref, bonus_ref, gate_ref, *,
                      bw, seg_lat, seg_ctx, n_lat_tiles):
    i = pl.program_id(0)
    p = cur_ref[...]
    tr = p.shape[0]
    lat = i < n_lat_tiles
    seg = jnp.where(lat, seg_lat, seg_ctx)
    pos = jnp.where(lat, i, i - n_lat_tiles) % seg
    row = lax.broadcasted_iota(jnp.int32, (tr, 1), 0)
    prev_row = jnp.where(pos != 0, prev_ref[SUBLANES - 1:SUBLANES, :], 0.0)
    next_row = jnp.where(pos != seg - 1, next_ref[0:1, :], 0.0)
    prv = jnp.where(row == 0, prev_row, pltpu.roll(p, 1, 0))
    nxt = jnp.where(row == tr - 1, next_row, pltpu.roll(p, tr - 1, 0))
    xs = p + mu_ref[0:1, :] * (prv - p) + mu_ref[1:2, :] * (nxt - p)

    r = xs[:, 0:bw]
    k = xs[:, bw:2 * bw]
    v = xs[:, 2 * bw:3 * bw]
    lora = xs[:, 3 * bw:]
    n_hp = bw // LANES

    kk = k * kk_ref[...]
    kk = kk * lax.rsqrt(jnp.maximum(_group_sum(kk * kk, e_ref), 1e-24))
    ksum = jnp.zeros_like(k)
    for d, (s_ref, tri_ref) in enumerate(((sf_ref, tril_ref), (sb_ref, triu_ref))):
        wd = lora[:, d * LORA_PAD:(d + 1) * LORA_PAD]
        ad = lora[:, (2 + d) * LORA_PAD:(3 + d) * LORA_PAD]
        log_decay = -DECAY_SCALE * jax.nn.sigmoid(w0_ref[d:d + 1, :] + _dot3(jnp.tanh(wd), wup_ref[d]))
        a = jax.nn.sigmoid(a0_ref[d:d + 1, :] + _dot3(ad, aup_ref[d]))
        key = k * (1.0 + (a - 1.0) * ka_ref[...])
        ksum = ksum + key
        kka = kk * a
        cum = _tri3(tri_ref[...], log_decay)
        last = CHUNK - 1 if d == 0 else 0
        tot_rows = [cum[c * CHUNK + last:c * CHUNK + last + 1] for c in range(tr // CHUNK)]
        tot = jnp.concatenate([jnp.broadcast_to(t, (CHUNK, bw)) for t in tot_rows], axis=0)
        e_inv = jnp.exp(-cum)
        e_rel = jnp.exp(tot - cum)
        streams = (kk * jnp.exp(cum - log_decay), r * jnp.exp(cum), kka * e_inv, key * e_inv,
                   kka * e_rel, key * e_rel)
        for n, st in enumerate(streams):
            st = st.astype(BF16)
            for hp in range(n_hp):
                s_ref[n, hp] = st[:, hp * LANES:(hp + 1) * LANES]
        for cidx in range(tr // CHUNK):
            e_tot = jnp.exp(tot_rows[cidx])
            etot_ref[d, cidx] = jnp.concatenate(
                [e_tot[:, hp * LANES:(hp + 1) * LANES] for hp in range(n_hp)], axis=0)
    vb = v.astype(BF16)
    for hp in range(n_hp):
        sv_ref[hp] = vb[:, hp * LANES:(hp + 1) * LANES]
    bonus_ref[...] = _group_sum(r * ksum * rk_ref[...], e_ref) * v
    gate_ref[...] = _dot3(jax.nn.sigmoid(lora[:, 4 * LORA_PAD:]), gup_ref[...])


def _chunk_block_diag(tr, kind):
    t = jnp.arange(tr)
    same = (t[:, None] // CHUNK) == (t[None, :] // CHUNK)
    if kind == "lower":
        same = same & (t[None, :] <= t[:, None])
    elif kind == "upper":
        same = same & (t[None, :] >= t[:, None])
    return same.astype(BF16)


N_STREAMS = 6


def _rwkv_prep(p, lw, e_mat, cfg):
    rows = p.shape[0]
    bw, tr = cfg["b_width"], cfg["tr"]
    cw = 3 * bw + cfg["lora_w"]
    n_hp = bw // LANES
    n_lat_tiles = cfg["batch"] * cfg["seq"] // tr
    hb = tr // SUBLANES
    last8 = rows // SUBLANES - 1
    full = lambda a: pl.BlockSpec(a.shape, lambda i: (0,) * a.ndim)
    consts = [lw["mu"], lw["w0"], lw["w_up"], lw["a0"], lw["a_up"], lw["g_up"], lw["k_k"], lw["k_a"], lw["r_k"],
              e_mat, _chunk_block_diag(tr, "lower"), _chunk_block_diag(tr, "upper")]
    stream_spec = pl.BlockSpec((N_STREAMS, n_hp, tr, LANES), lambda i: (0, 0, i, 0))
    stream_shape = jax.ShapeDtypeStruct((N_STREAMS, n_hp, rows, LANES), BF16)
    cpt = tr // CHUNK
    kernel = functools.partial(_rwkv_prep_kernel, bw=bw, seg_lat=cfg["seq"] // tr, seg_ctx=cfg["ctx"] // tr,
                               n_lat_tiles=n_lat_tiles)
    return pl.pallas_call(
        kernel,
        grid=(rows // tr,),
        in_specs=[
            pl.BlockSpec((tr, cw), lambda i: (i, 0)),
            pl.BlockSpec((SUBLANES, cw), lambda i: (jnp.maximum(i * hb - 1, 0), 0)),
            pl.BlockSpec((SUBLANES, cw), lambda i: (jnp.minimum((i + 1) * hb, last8), 0)),
        ] + [full(a) for a in consts],
        out_specs=[pl.BlockSpec((n_hp, tr, LANES), lambda i: (0, i, 0)),
                   stream_spec, stream_spec,
                   pl.BlockSpec((2, cpt, n_hp, LANES), lambda i: (0, i, 0, 0)),
                   pl.BlockSpec((tr, bw), lambda i: (i, 0)),
                   pl.BlockSpec((tr, bw), lambda i: (i, 0))],
        out_shape=[jax.ShapeDtypeStruct((n_hp, rows, LANES), BF16),
                   stream_shape, stream_shape,
                   jax.ShapeDtypeStruct((2, rows // CHUNK, n_hp, LANES), F32),
                   jax.ShapeDtypeStruct((rows, bw), F32),
                   jax.ShapeDtypeStruct((rows, bw), F32)],
        compiler_params=_cparams(("parallel",)),
        name="rwkv_prep",
    )(p, p, p, *consts)


def _block_diag(x):
    lo = lax.broadcasted_iota(jnp.int32, x.shape, 1) < HEAD_DIM
    zero = jnp.zeros_like(x)
    return jnp.concatenate([jnp.where(lo, x, zero), jnp.where(lo, zero, x)], axis=0)


def _scan_step(dirs, ones_ref, z_ref, n_hp):
    c, hd = CHUNK, HEAD_DIM
    assert c == hd and LANES == 2 * hd
    row = lax.broadcasted_iota(jnp.int32, (c, LANES), 0)
    lane = lax.broadcasted_iota(jnp.int32, (c, LANES), 1)
    col = lane & (hd - 1)
    lo = lane < hd
    eye = col == row
    inst = [(d, hp) for d in range(len(dirs)) for hp in range(n_hp)]
    pairs = range(len(inst))
    incl = [(col >= row) if dirs[d][4] else (col <= row) for d, _ in inst]
    strict = [(col > row) if dirs[d][4] else (col < row) for d, _ in inst]

    def half(x, h):
        keep = lo if h == 0 else ~lo
        return jnp.where(keep, x, jnp.zeros_like(x))

    qk, rt, bt, kt, bh, kh = ([dirs[d][1][k, hp] for d, hp in inst] for k in range(N_STREAMS))
    vv = [dirs[d][0][hp] for d, hp in inst]
    p1 = [_dot(jnp.concatenate([qk[i], rt[i]], axis=0),
               jnp.concatenate([half(bt[i], 0), half(bt[i], 1), half(kt[i], 0), half(kt[i], 1)], axis=0),
               trans_b=True) for i in pairs]
    same_blk = (row // SUB) == (col // SUB)
    a_ab = [jnp.where(strict[i], p[:c, :LANES], 0.0) for i, p in enumerate(p1)]
    x_pow = [jnp.where(same_blk, -a, 0.0) for a in a_ab]
    u = x_pow
    x_pow = [_dot(x.astype(BF16), _block_diag(x.astype(BF16))) for x in x_pow]
    sq = 2
    while sq < SUB:
        w = [_block_diag(x.astype(BF16)) for x in x_pow]
        if 2 * sq < SUB:
            prod = [_dot(jnp.concatenate([u[i], x_pow[i]], axis=0).astype(BF16), w[i]) for i in pairs]
            u = [u[i] + x_pow[i] + prod[i][:c] for i in pairs]
            x_pow = [pr[c:] for pr in prod]
        else:
            u = [u[i] + x_pow[i] + _dot(u[i].astype(BF16), w[i]) for i in pairs]
        sq *= 2
    u_bf = [x.astype(BF16) for x in u]
    n_off = [jnp.where(same_blk, 0.0, a) for a in a_ab]
    m1_bf = [(-(n_off[i] + _dot(u_bf[i], _block_diag(n_off[i].astype(BF16))))).astype(BF16) for i in pairs]
    m2_bf = [_dot(m1_bf[i], _block_diag(m1_bf[i])).astype(BF16) for i in pairs]
    avy = [_dot(jnp.concatenate([jnp.where(strict[i], p[:c, LANES:], 0.0), jnp.where(incl[i], p[c:, LANES:], 0.0)],
                                axis=0).astype(BF16), _block_diag(vv[i])) for i, p in enumerate(p1)]

    def pair_cols(x):
        return jnp.concatenate([_block_diag(x[:, :LANES].astype(BF16)), _block_diag(x[:, LANES:].astype(BF16))],
                               axis=1)

    r = [jnp.concatenate([qk[i].astype(F32), avy[i][:c]], axis=1) for i in pairs]
    r = [r[i] + _dot(u_bf[i], pair_cols(r[i])) for i in pairs]
    r = [r[i] + _dot(m2_bf[i], pair_cols(r[i])) for i in pairs]
    tu = [(r[i] + _dot(m1_bf[i], pair_cols(r[i]))).astype(BF16) for i in pairs]
    bu = [_dot(jnp.where(incl[i], p[c:, :LANES], 0.0).astype(BF16),
               jnp.concatenate([_block_diag(tu[i][:, :LANES]), _block_diag(tu[i][:, LANES:])], axis=1))
          for i, p in enumerate(p1)]
    rh = [(rt[i].astype(F32) - bu[i][:, :LANES]).astype(BF16) for i in pairs]
    yh = [avy[i][c:] - bu[i][:, LANES:] for i in pairs]
    rp = [_dot(bh[i], tu[i], trans_a=True) for i in pairs]
    kv = [_dot(kh[i], vv[i], trans_a=True) for i in pairs]
    g = [jnp.where(lo, r[:hd, :LANES], r[hd:, :LANES]).astype(BF16) for r in rp]
    hc = [jnp.where(lo, kv[i][:hd], kv[i][hd:]) - jnp.where(lo, rp[i][:hd, LANES:], rp[i][hd:, LANES:])
          for i in pairs]
    ones2 = jnp.concatenate([ones_ref[...], ones_ref[...]], axis=0)
    e_col = []
    for d, hp in inst:
        e_diag = jnp.where(eye, jnp.broadcast_to(dirs[d][2][hp:hp + 1, :], (c, LANES)), 0.0)
        e_col.append(_dot(jnp.concatenate(_split(e_diag), axis=1), ones2))
    z = [z_ref[i] for i in pairs]
    yz = [_dot(jnp.concatenate([rh[i], g[i]], axis=0), _block_diag(z[i].astype(BF16))) for i in pairs]
    for i, (d, hp) in enumerate(inst):
        dirs[d][3][hp] = yh[i] + yz[i][:c]
        z_ref[i] = e_col[i] * z[i] - yz[i][c:] + hc[i]


def _rwkv_scan_kernel(vf_ref, sf_ref, ef_ref, vb_ref, sb_ref, eb_ref, ones_ref, yf_ref, yb_ref, z_ref, *, n_hp):
    @pl.when(pl.program_id(1) == 0)
    def _():
        z_ref[...] = jnp.zeros_like(z_ref)

    _scan_step([(vf_ref, sf_ref, ef_ref, yf_ref, False), (vb_ref, sb_ref, eb_ref, yb_ref, True)],
               ones_ref, z_ref, n_hp)


def _rwkv_scan(s_v, s_f, s_b, e_tot, cfg):
    n_hp, rows, _ = s_v.shape
    b, seq, ctx = cfg["batch"], cfg["seq"], cfg["ctx"]
    c = CHUNK
    ncc, ncl = ctx // c, seq // c
    ctx0 = b * seq // c

    def fwd_blk(bi, j):
        return jnp.where(j < ncc, ctx0 + bi * ncc + j, bi * ncl + (j - ncc))

    def bwd_blk(bi, j):
        return jnp.where(j < ncc, ctx0 + bi * ncc + (ncc - 1 - j), bi * ncl + (ncl - 1 - (j - ncc)))

    def specs(blk, d):
        return [pl.BlockSpec((n_hp, c, LANES), lambda bi, j: (0, blk(bi, j), 0)),
                pl.BlockSpec((N_STREAMS, n_hp, c, LANES), lambda bi, j: (0, 0, blk(bi, j), 0)),
                pl.BlockSpec((None, None, n_hp, LANES), lambda bi, j: (d, blk(bi, j), 0, 0))]

    out_f = pl.BlockSpec((n_hp, c, LANES), lambda bi, j: (0, fwd_blk(bi, j), 0))
    out_b = pl.BlockSpec((n_hp, c, LANES), lambda bi, j: (0, bwd_blk(bi, j), 0))
    y_shape = jax.ShapeDtypeStruct((n_hp, rows, LANES), F32)
    return pl.pallas_call(
        functools.partial(_rwkv_scan_kernel, n_hp=n_hp),
        grid=(b, ncc + ncl),
        in_specs=specs(fwd_blk, 0) + specs(bwd_blk, 1) + [pl.BlockSpec((LANES, LANES), lambda bi, j: (0, 0))],
        out_specs=[out_f, out_b],
        out_shape=[y_shape, y_shape],
        scratch_shapes=[pltpu.VMEM((2 * n_hp, HEAD_DIM, LANES), F32)],
        compiler_params=_cparams(("parallel", "arbitrary")),
        name="rwkv_scan",
    )(s_v, s_f, e_tot, s_v, s_b, e_tot, _block_ones(LANES))


def _outproj_kernel(oa_ref, yf_ref, yb_ref, bonus_ref, gate_ref, gg_ref, gb_ref, e_ref, oc_ref, w_ref, h_ref,
                    g_ref, n2_ref, sc_ref, sh_ref, o_ref, u_ref):
    n_hp = yf_ref.shape[0]
    y = jnp.concatenate([yf_ref[hp] + yb_ref[hp] for hp in range(n_hp)], axis=1)
    mu = _group_sum(y, e_ref) * (1.0 / HEAD_DIM)
    yc = y - mu
    var = _group_sum(yc * yc, e_ref) * (1.0 / HEAD_DIM)
    yn = yc * lax.rsqrt(var + GN_EPS) * gg_ref[...] + gb_ref[...]
    o_b = ((yn + bonus_ref[...]) * gate_ref[...]).astype(BF16)
    mix = jnp.concatenate([oa_ref[...], o_b, oc_ref[...]], axis=1)
    h_new = h_ref[...] + g_ref[...] * _dot(mix, w_ref[...])
    o_ref[...] = h_new
    u_ref[...] = _modulated_norm(h_new, n2_ref[...], sc_ref[...], sh_ref[...]).astype(BF16)


def _outproj(o_a, y_f, y_b, bonus, gate, gn_g, gn_b, e_mat, o_c, w, h, mods, norm2_g, l, n_rows, cfg):
    d = h.shape[1]
    n_hp = y_f.shape[0]
    bw, tm = cfg["b_width"], cfg["tm"] // 2
    modrow = cfg["modrow"](tm)
    rows_of = lambda width: pl.BlockSpec((tm, width), lambda i: (i, 0))
    y_spec = pl.BlockSpec((n_hp, tm, LANES), lambda i: (0, i, 0))
    vec_spec = pl.BlockSpec((None, 1, bw), lambda i: (l, 0, 0))
    mod = lambda k: pl.BlockSpec((None, None, None, 1, d), lambda i: (l, modrow(i), k, 0, 0))
    return pl.pallas_call(
        _outproj_kernel,
        grid=(n_rows // tm,),
        in_specs=[rows_of(o_a.shape[1]), y_spec, y_spec, rows_of(bw), rows_of(bw), vec_spec, vec_spec,
                  pl.BlockSpec(e_mat.shape, lambda i: (0, 0)),
                  rows_of(o_c.shape[1]),
                  pl.BlockSpec((None, d, d), lambda i: (l, 0, 0)),
                  rows_of(d),
                  mod(2),
                  pl.BlockSpec((None, 1, d), lambda i: (l, 0, 0)),
                  mod(4), mod(3)],
        out_specs=[rows_of(d), rows_of(d)],
        out_shape=[jax.ShapeDtypeStruct((n_rows, d), F32), jax.ShapeDtypeStruct((n_rows, d), BF16)],
        compiler_params=_cparams(("parallel",)),
        name="out_proj",
    )(o_a, y_f, y_b, bonus, gate, gn_g, gn_b, e_mat, o_c, w, h, mods, norm2_g, mods, mods)


def _mlp_kernel(u_ref, hcol_ref, gate_ref, w1_ref, w2_ref, o_ref, mid_scr, *, n_up):
    s = pl.program_id(1)

    @pl.when(s < n_up)
    def _():
        a = jnp.maximum(_dot(u_ref[...], w1_ref[...]), 0.0)
        mid_scr[s] = (a * a).astype(BF16)

    @pl.when(s >= n_up)
    def _():
        mid = jnp.concatenate([mid_scr[k] for k in range(n_up)], axis=1)
        o_ref[...] = hcol_ref[...] + gate_ref[...] * _dot(mid, w2_ref[...])


def _mlp(h, u, mods, w1, w2, l, cfg):
    rows, d = h.shape
    dff = w1.shape[2]
    tm, tf, tn = cfg["tm"], 2048, 512
    n_up = dff // tf
    modrow = cfg["modrow"](tm)
    col = lambda s: jnp.maximum(s - n_up, 0)
    return pl.pallas_call(
        functools.partial(_mlp_kernel, n_up=n_up),
        grid=(rows // tm, n_up + d // tn),
        in_specs=[
            pl.BlockSpec((tm, d), lambda i, s: (i, 0)),
            pl.BlockSpec((tm, tn), lambda i, s: (i, col(s))),
            pl.BlockSpec((None, None, None, 1, tn), lambda i, s: (l, modrow(i), 5, 0, col(s))),
            pl.BlockSpec((None, d, tf), lambda i, s: (l, 0, jnp.minimum(s, n_up - 1))),
            pl.BlockSpec((None, dff, tn), lambda i, s: (l, 0, col(s))),
        ],
        out_specs=pl.BlockSpec((tm, tn), lambda i, s: (i, col(s))),
        out_shape=jax.ShapeDtypeStruct((rows, d), F32),
        scratch_shapes=[pltpu.VMEM((n_up, tm, tf), BF16)],
        compiler_params=_cparams(("parallel", "arbitrary")),
        name="mlp",
    )(u, h, mods, w1, w2)


def _rope_tables(seq, tm):
    rows = seq // GRID_W
    row = jnp.broadcast_to(jnp.arange(rows)[:, None], (rows, GRID_W)).reshape(-1)
    col = jnp.broadcast_to(jnp.arange(GRID_W)[None, :], (rows, GRID_W)).reshape(-1)
    n_freq = HEAD_DIM // 4
    inv = ROPE_THETA ** (-jnp.arange(n_freq, dtype=F32) / n_freq)
    ang = jnp.concatenate([row[:, None].astype(F32) * inv, col[:, None].astype(F32) * inv], -1)
    cos, sin = jnp.cos(ang), jnp.sin(ang)
    reps = LANES // HEAD_DIM
    cos_t = jnp.tile(jnp.concatenate([cos, cos], -1), (1, reps))
    sin_t = jnp.tile(jnp.concatenate([-sin, sin], -1), (1, reps))
    cos_t = jnp.concatenate([cos_t, jnp.ones((tm, LANES), F32)], 0)
    sin_t = jnp.concatenate([sin_t, jnp.zeros((tm, LANES), F32)], 0)
    return cos_t, sin_t


def _block_ones(width):
    g = jnp.arange(width) // HEAD_DIM
    return (g[:, None] == g[None, :]).astype(BF16)


def kernel(x, c, ctx, c_ctx, ada_w, ada_b, norm1_g, norm2_g, w_in, a_q_norm, a_k_norm, a_sink, c_q_norm,
           c_k_norm, shift_mu, decay_w0, decay_up, iclr_a0, iclr_up, gate_up, k_k, k_a, r_k, gn_g, gn_b,
           w_out, mlp_w1, mlp_w2):
    batch, seq, d = x.shape
    n_ctx = ctx.shape[1]
    depth = ada_w.shape[0]
    bw = k_k.shape[1]
    lora_d, lora_i, lora_g = decay_up.shape[2], iclr_up.shape[2], gate_up.shape[1]
    a_heads = a_sink.shape[1]
    a_kv = a_heads // 4
    q_w, kv_w = a_heads * HEAD_DIM, a_kv * HEAD_DIM
    attn_w = q_w + 2 * kv_w
    lora_w = 4 * LORA_PAD + lora_g
    b_in = 3 * bw + 2 * lora_d + 2 * lora_i + lora_g
    assert lora_d <= LORA_PAD and lora_i <= LORA_PAD
    assert w_in.shape[2] == 2 * attn_w + b_in

    tm = batch * n_ctx
    n_lat_tiles = batch * seq // tm
    tiles_per_batch = seq // tm
    cfg = dict(
        batch=batch, seq=seq, ctx=n_ctx, tm=tm, tr=n_ctx, tq=n_ctx, tk=min(seq, 2048),
        b_width=bw, lora_w=lora_w, attn_w=attn_w, attn_col0=3 * bw + lora_w, qk_w=q_w + kv_w,
        attn_heads=a_heads, attn_kv=a_kv,
        modrow=lambda t: (lambda i: jnp.where(i < batch * seq // t, i // (seq // t), batch)),
        rope_blk=lambda i: jnp.where(i < n_lat_tiles, i % tiles_per_batch, tiles_per_batch),
    )
    assert seq % tm == 0 and seq % GRID_W == 0 and n_ctx % CHUNK == 0 and (3 * bw) % lora_w == 0
    assert cfg["attn_col0"] % attn_w == 0 and batch + 1 <= SUBLANES

    def relayout_cols(m):
        a_part, b_part, c_part = m[..., :attn_w], m[..., attn_w:attn_w + b_in], m[..., attn_w + b_in:]
        pad = lambda z, n: jnp.pad(z, [(0, 0)] * (z.ndim - 1) + [(0, n - z.shape[-1])])
        o = 3 * bw
        pieces = [b_part[..., :o]]
        for width in (lora_d, lora_d, lora_i, lora_i):
            pieces.append(pad(b_part[..., o:o + width], LORA_PAD))
            o += width
        pieces.append(b_part[..., o:])
        return jnp.concatenate(pieces + [a_part, c_part], -1)

    w_in_r = relayout_cols(w_in).astype(BF16)
    mu_r = relayout_cols(jnp.pad(shift_mu, ((0, 0), (0, 0), (attn_w, attn_w))))[..., :3 * bw + lora_w]
    pad_rows = lambda z: jnp.pad(z, ((0, 0), (0, 0), (0, LORA_PAD - z.shape[2]), (0, 0)))
    w_up_r, a_up_r = pad_rows(decay_up), pad_rows(iclr_up)
    w_out_b = w_out.astype(BF16)
    w1_b, w2_b = mlp_w1.astype(BF16), mlp_w2.astype(BF16)

    scale = HEAD_DIM ** -0.5 * LOG2E
    tile = lambda g, n: jnp.tile(g, (1, n))

    def gains(qg, kg):
        return jnp.concatenate([tile(qg, a_heads) * scale, tile(kg, a_kv), jnp.ones((depth, kv_w), F32)], -1)

    qk_gains = jnp.stack([gains(a_q_norm, a_k_norm), gains(c_q_norm, c_k_norm)], 1)[:, :, None, :]

    cos_t, sin_t = _rope_tables(seq, tm)
    e_attn = _block_ones(attn_w)
    e_b = _block_ones(2 * LANES)

    cvec = jnp.zeros((SUBLANES, d), F32).at[:batch].set(c).at[batch].set(c_ctx)
    mods = _mods(cvec, ada_w, ada_b).reshape(depth, SUBLANES, 6, 1, d)

    h = jnp.concatenate([x.reshape(batch * seq, d), ctx.reshape(batch * n_ctx, d)], 0)
    n1g, n2g = norm1_g[:, None, :], norm2_g[:, None, :]
    gn_g3, gn_b3 = gn_g[:, None, :], gn_b[:, None, :]
    for l in range(depth):
        p = _inproj(h, n1g, mods, w_in_r, l, cfg)
        qkv = _qkprep(p, qk_gains[l], cos_t, sin_t, e_attn, cfg)
        o_a = _attention(qkv, 0, a_sink[l], cfg, window=True)
        o_c = _attention(qkv, 1, None, cfg, window=False)
        lw = dict(mu=mu_r[l], w0=decay_w0[l], w_up=w_up_r[l], a0=iclr_a0[l], a_up=a_up_r[l], g_up=gate_up[l],
                  k_k=k_k[l][None], k_a=k_a[l][None], r_k=r_k[l][None])
        s_v, s_f, s_b, e_tot, bonus, gate = _rwkv_prep(p, lw, e_b, cfg)
        y_f, y_b = _rwkv_scan(s_v, s_f, s_b, e_tot, cfg)
        n_rows = batch * seq if l == depth - 1 else h.shape[0]
        h, u2 = _outproj(o_a, y_f, y_b, bonus, gate, gn_g3, gn_b3, e_b, o_c, w_out_b, h, mods, n2g, l, n_rows, cfg)
        h = _mlp(h, u2, mods, w1_b, w2_b, l, cfg)
    return h.reshape(batch, seq, d)
```

```python
import functools

import jax
import jax.numpy as jnp
from jax import lax
from jax.experimental import pallas as pl
from jax.experimental.pallas import tpu as pltpu

F32 = jnp.float32
BF16 = jnp.bfloat16

HEAD_DIM = 64
GRID_W = 64
WINDOW = 128
ROPE_THETA = 10000.0
NORM_EPS = 1e-6
GN_EPS = 64e-5
NEG_INF = -1e30
LOG2E = 1.4426950408889634
DECAY_SCALE = 0.6065306597126334
LANES = 128
SUBLANES = 8
LORA_PAD = LANES
SUB = 16
CHUNK = 64
VMEM_LIMIT = 52 * 1024 * 1024


def _dot(a, b, trans_a=False, trans_b=False):
    dn = (((0 if trans_a else 1,), (1 if trans_b else 0,)), ((), ()))
    return lax.dot_general(a, b, dn, preferred_element_type=F32)


def _split(x):
    hi = x.astype(BF16)
    lo = (x - hi.astype(F32)).astype(BF16)
    return hi, lo


def _dot3(a, b, trans_a=False, trans_b=False):
    ah, al = _split(a)
    bh, bl = _split(b)
    kw = dict(trans_a=trans_a, trans_b=trans_b)
    return _dot(ah, bh, **kw) + (_dot(al, bh, **kw) + _dot(ah, bl, **kw))


def _group_sum(x, e_ref):
    gw = e_ref.shape[0]
    e = e_ref[...]
    outs = []
    for g in range(x.shape[1] // gw):
        hi, lo = _split(x[:, g * gw:(g + 1) * gw])
        outs.append(_dot(hi, e) + _dot(lo, e))
    return outs[0] if len(outs) == 1 else jnp.concatenate(outs, axis=1)


def _cparams(sem):
    return pltpu.CompilerParams(dimension_semantics=sem, vmem_limit_bytes=VMEM_LIMIT)


def _mods_kernel(c_ref, w_ref, b_ref, o_ref):
    c = c_ref[...]
    s = c * jax.nn.sigmoid(c)
    o_ref[0] = _dot3(s, w_ref[0]) + b_ref[0]


def _mods(cvec, ada_w, ada_b):
    depth, d, n = ada_w.shape
    tn = 512
    return pl.pallas_call(
        _mods_kernel,
        grid=(depth, n // tn),
        in_specs=[
            pl.BlockSpec((SUBLANES, d), lambda l, j: (0, 0)),
            pl.BlockSpec((1, d, tn), lambda l, j: (l, 0, j)),
            pl.BlockSpec((1, 1, tn), lambda l, j: (l, 0, j)),
        ],
        out_specs=pl.BlockSpec((1, SUBLANES, tn), lambda l, j: (l, 0, j)),
        out_shape=jax.ShapeDtypeStruct((depth, SUBLANES, n), F32),
        compiler_params=_cparams(("parallel", "parallel")),
        name="adaln_mods",
    )(cvec, ada_w, ada_b.reshape(depth, 1, n))


def _modulated_norm(x, g, sc, sh):
    ms = jnp.mean(x * x, axis=-1, keepdims=True)
    return (x * lax.rsqrt(ms + NORM_EPS) * g) * (1.0 + sc) + sh


def _inproj_kernel(h_ref, g_ref, sc_ref, sh_ref, w_ref, o_ref, u_scr):
    @pl.when(pl.program_id(1) == 0)
    def _():
        u_scr[...] = _modulated_norm(h_ref[...], g_ref[...], sc_ref[...], sh_ref[...]).astype(BF16)

    o_ref[...] = _dot(u_scr[...], w_ref[...])


def _inproj(h, gain, mods, w, l, cfg):
    rows, d = h.shape
    n = w.shape[2]
    tm, tn = cfg["tm"], 2688
    modrow = cfg["modrow"](tm)
    return pl.pallas_call(
        _inproj_kernel,
        grid=(rows // tm, n // tn),
        in_specs=[
            pl.BlockSpec((tm, d), lambda i, j: (i, 0)),
            pl.BlockSpec((None, 1, d), lambda i, j: (l, 0, 0)),
            pl.BlockSpec((None, None, None, 1, d), lambda i, j: (l, modrow(i), 1, 0, 0)),
            pl.BlockSpec((None, None, None, 1, d), lambda i, j: (l, modrow(i), 0, 0, 0)),
            pl.BlockSpec((None, d, tn), lambda i, j: (l, 0, j)),
        ],
        out_specs=pl.BlockSpec((tm, tn), lambda i, j: (i, j)),
        out_shape=jax.ShapeDtypeStruct((rows, n), F32),
        scratch_shapes=[pltpu.VMEM((tm, d), BF16)],
        compiler_params=_cparams(("parallel", "arbitrary")),
        name="in_proj",
    )(h, gain, mods, mods, w)


def _qkprep_kernel(p_ref, gain_ref, cos_ref, sin_ref, e_ref, o_ref, *, qk_w):
    x = p_ref[...]
    width = x.shape[1]
    ss = _group_sum(x * x, e_ref)
    y = x * lax.rsqrt(ss * (1.0 / HEAD_DIM) + NORM_EPS) * gain_ref[0]
    cos = cos_ref[...]
    sin = sin_ref[...]
    lane = lax.broadcasted_iota(jnp.int32, (x.shape[0], LANES), 1)
    first_half = (lane & (HEAD_DIM // 2)) == 0
    for g in range(width // LANES):
        sl = slice(g * LANES, (g + 1) * LANES)
        if g * LANES < qk_w:
            yg = y[:, sl]
            partner = jnp.where(first_half, pltpu.roll(yg, LANES - HEAD_DIM // 2, 1),
                                pltpu.roll(yg, HEAD_DIM // 2, 1))
            o_ref[0, :, sl] = (yg * cos + partner * sin).astype(BF16)
        else:
            o_ref[0, :, sl] = x[:, sl].astype(BF16)


def _qkprep(p, gains, cos_t, sin_t, e_mat, cfg):
    rows = p.shape[0]
    tm, aw = cfg["tm"], cfg["attn_w"]
    first_blk = cfg["attn_col0"] // aw
    rope_blk = cfg["rope_blk"]
    return pl.pallas_call(
        functools.partial(_qkprep_kernel, qk_w=cfg["qk_w"]),
        grid=(rows // tm, 2),
        in_specs=[
            pl.BlockSpec((tm, aw), lambda i, s: (i, first_blk + s)),
            pl.BlockSpec((1, 1, aw), lambda i, s: (s, 0, 0)),
            pl.BlockSpec((tm, LANES), lambda i, s: (rope_blk(i), 0)),
            pl.BlockSpec((tm, LANES), lambda i, s: (rope_blk(i), 0)),
            pl.BlockSpec(e_mat.shape, lambda i, s: (0, 0)),
        ],
        out_specs=pl.BlockSpec((1, tm, aw), lambda i, s: (s, i, 0)),
        out_shape=jax.ShapeDtypeStruct((2, rows, aw), BF16),
        compiler_params=_cparams(("parallel", "parallel")),
        name="qk_prep",
    )(p, gains, cos_t, sin_t, e_mat)


def _attn_kernel(sink_ref, q_ref, kc_ref, vc_ref, kl_ref, vl_ref, o_ref, *,
                 window, n_heads, group, tq, tk, seq):
    i = pl.program_id(1)
    is_lat = i < seq // tq
    hd = HEAD_DIM
    n_kv = n_heads // group
    rows = group * tq
    den_lane = [((j + 1) % n_kv) * hd for j in range(n_kv)]
    if window:
        span = tq + 2 * WINDOW
        start = pl.multiple_of(jnp.clip(i * tq - WINDOW, 0, seq - span), WINDOW)
        delta = (lax.broadcasted_iota(jnp.int32, (tq, span), 0) - lax.broadcasted_iota(jnp.int32, (tq, span), 1)
                 + (i * tq - start))
        bias = jnp.where(jnp.abs(delta) <= WINDOW, 0.0, NEG_INF)
        n_iter = jnp.where(is_lat, 1, 0)
    else:
        span = tk
        n_iter = jnp.where(is_lat, seq // tk, 0)

    def with_ones(vblk, j):
        lane = lax.broadcasted_iota(jnp.int32, vblk.shape, 1)
        ones_col = jnp.where(lane == den_lane[j], 1.0, 0.0).astype(BF16)
        return jnp.where((lane >= j * hd) & (lane < (j + 1) * hd), vblk, ones_col)

    zeros = jnp.zeros((tq, hd), BF16)
    acc_lane = lax.broadcasted_iota(jnp.int32, (rows, LANES), 1)
    for j in range(n_kv):
        qs = []
        for g in range(group):
            h = j * group + g
            parts = [zeros] * n_kv
            parts[j] = q_ref[0, :, h * hd:(h + 1) * hd]
            qs.append(jnp.concatenate(parts, axis=1))
        q = jnp.concatenate(qs, axis=0)
        if sink_ref is not None:
            m0 = jnp.concatenate(
                [jnp.full((tq, 1), sink_ref[j * group + g] * LOG2E, F32) for g in range(group)], axis=0)
            acc0 = jnp.where(acc_lane == den_lane[j], 1.0, 0.0)
        else:
            m0 = jnp.full((rows, 1), NEG_INF, F32)
            acc0 = jnp.zeros((rows, LANES), F32)

        s = _dot(q, kc_ref[0], trans_b=True)
        m = jnp.maximum(m0, jnp.max(s, axis=-1, keepdims=True))
        p = jnp.exp2(s - m).astype(BF16)
        acc = jnp.exp2(m0 - m) * acc0 + _dot(p, with_ones(vc_ref[0], j))

        def body(kb, carry, q=q, j=j):
            m, acc = carry
            off = start if window else pl.multiple_of(kb * tk, tk)
            s = _dot(q, kl_ref[0, pl.ds(off, span), :], trans_b=True)
            if window:
                s = (s.reshape(group, tq, span) + bias[None]).reshape(rows, span)
            m_new = jnp.maximum(m, jnp.max(s, axis=-1, keepdims=True))
            p = jnp.exp2(s - m_new).astype(BF16)
            acc_new = jnp.exp2(m - m_new) * acc + _dot(p, with_ones(vl_ref[0, pl.ds(off, span), :], j))
            return m_new, acc_new

        m, acc = lax.fori_loop(0, n_iter, body, (m, acc))
        o = acc[:, j * hd:(j + 1) * hd] / acc[:, den_lane[j]:den_lane[j] + 1]
        for g in range(group):
            c0 = (j * group + g) * hd
            o_ref[:, c0:c0 + hd] = o[g * tq:(g + 1) * tq].astype(BF16)


def _window_attn_kernel(sink_ref, q_ref, kc_ref, vc_ref, kl_ref, vl_ref, o_ref, *,
                        window, n_heads, group, tq, tk, seq):
    del window, tk
    i = pl.program_id(1)
    is_lat = i < seq // tq
    hd = HEAD_DIM
    n_kv = n_heads // group
    rows = group * tq
    span = tq + 2 * WINDOW
    den_lane = [((j + 1) % n_kv) * hd for j in range(n_kv)]
    start = pl.multiple_of(jnp.clip(i * tq - WINDOW, 0, seq - span), WINDOW)
    delta = (lax.broadcasted_iota(jnp.int32, (tq, span), 0) - lax.broadcasted_iota(jnp.int32, (tq, span), 1)
             + (i * tq - start))
    bias = jnp.where(jnp.abs(delta) <= jnp.where(is_lat, WINDOW, -1), 0.0, NEG_INF)
    k_ctx, v_ctx = kc_ref[0], vc_ref[0]
    k_win, v_win = kl_ref[0, pl.ds(start, span), :], vl_ref[0, pl.ds(start, span), :]

    def with_ones(vblk, j):
        lane = lax.broadcasted_iota(jnp.int32, vblk.shape, 1)
        ones_col = jnp.where(lane == den_lane[j], 1.0, 0.0).astype(BF16)
        return jnp.where((lane >= j * hd) & (lane < (j + 1) * hd), vblk, ones_col)

    zeros = jnp.zeros((tq, hd), BF16)
    for j in range(n_kv):
        qs = []
        for g in range(group):
            h = j * group + g
            parts = [zeros] * n_kv
            parts[j] = q_ref[0, :, h * hd:(h + 1) * hd]
            qs.append(jnp.concatenate(parts, axis=1))
        q = jnp.concatenate(qs, axis=0)
        m0 = jnp.concatenate(
            [jnp.full((tq, 1), sink_ref[j * group + g] * LOG2E, F32) for g in range(group)], axis=0)
        s_ctx = _dot(q, k_ctx, trans_b=True)
        s_win = (_dot(q, k_win, trans_b=True).reshape(group, tq, span) + bias[None]).reshape(rows, span)
        m = jnp.maximum(m0, jnp.maximum(jnp.max(s_ctx, axis=-1, keepdims=True),
                                        jnp.max(s_win, axis=-1, keepdims=True)))
        acc = (_dot(jnp.exp2(s_ctx - m).astype(BF16), with_ones(v_ctx, j))
               + _dot(jnp.exp2(s_win - m).astype(BF16), with_ones(v_win, j)))
        den = acc[:, den_lane[j]:den_lane[j] + 1] + jnp.exp2(m0 - m)
        o = acc[:, j * hd:(j + 1) * hd] / den
        for g in range(group):
            c0 = (j * group + g) * hd
            o_ref[:, c0:c0 + hd] = o[g * tq:(g + 1) * tq].astype(BF16)


def _attention(qkv, sec, sink, cfg, window):
    rows = qkv.shape[1]
    b, seq, ctx, tq = cfg["batch"], cfg["seq"], cfg["ctx"], cfg["tq"]
    n_heads, n_kv = cfg["attn_heads"], cfg["attn_kv"]
    q_w = n_heads * HEAD_DIM
    kv_w = n_kv * HEAD_DIM
    assert kv_w == LANES and q_w % kv_w == 0 and ctx == tq
    n_lat_tiles = seq // tq
    k_blk = q_w // kv_w
    ctx_blk0 = b * seq // ctx

    def q_map(bi, i, *_):
        return (sec, jnp.where(i < n_lat_tiles, bi * n_lat_tiles + i, ctx_blk0 + bi), 0)

    def o_map(bi, i, *_):
        return (jnp.where(i < n_lat_tiles, bi * n_lat_tiles + i, ctx_blk0 + bi), 0)

    tk = cfg["tk"]
    assert seq % tk == 0 and tq % WINDOW == 0 and tq + 2 * WINDOW <= seq
    assert (sink is not None) == window
    kernel = functools.partial(_window_attn_kernel if window else _attn_kernel, window=window, n_heads=n_heads,
                               group=n_heads // n_kv, tq=tq, tk=tk, seq=seq)
    in_specs = [
        pl.BlockSpec((1, tq, q_w), q_map),
        pl.BlockSpec((1, ctx, kv_w), lambda bi, i, *_: (sec, ctx_blk0 + bi, k_blk)),
        pl.BlockSpec((1, ctx, kv_w), lambda bi, i, *_: (sec, ctx_blk0 + bi, k_blk + 1)),
        pl.BlockSpec((1, seq, kv_w), lambda bi, i, *_: (sec, bi, k_blk)),
        pl.BlockSpec((1, seq, kv_w), lambda bi, i, *_: (sec, bi, k_blk + 1)),
    ]
    args = [qkv, qkv, qkv, qkv, qkv]
    if sink is not None:
        in_specs = [pl.BlockSpec(memory_space=pltpu.SMEM)] + in_specs
        args = [sink] + args
    else:
        kernel = functools.partial(kernel, None)
    return pl.pallas_call(
        kernel,
        grid=(b, n_lat_tiles + 1),
        in_specs=in_specs,
        out_specs=pl.BlockSpec((tq, q_w), o_map),
        out_shape=jax.ShapeDtypeStruct((rows, q_w), BF16),
        compiler_params=_cparams(("parallel", "arbitrary")),
        name="window_attn" if window else "global_attn",
    )(*args)


def _tri3(tri, x):
    h1 = x.astype(BF16)
    r1 = x - h1.astype(F32)
    h2 = r1.astype(BF16)
    h3 = (r1 - h2.astype(F32)).astype(BF16)
    return _dot(tri, h1) + (_dot(tri, h2) + _dot(tri, h3))


def _rwkv_prep_kernel(cur_ref, prev_ref, next_ref, mu_ref, w0_ref, wup_ref, a0_ref, aup_ref, gup_ref,
                      kk_ref, ka_ref, rk_ref, e_ref, tril_ref, triu_ref,
                      sv_ref, sf_ref, sb_ref, etot_ref, bonus_ref, gate_ref, *,
                      bw, seg_lat, seg_ctx, n_lat_tiles):
    i = pl.program_id(0)
    p = cur_ref[...]
    tr = p.shape[0]
    lat = i < n_lat_tiles
    seg = jnp.where(lat, seg_lat, seg_ctx)
    pos = jnp.where(lat, i, i - n_lat_tiles) % seg
    row = lax.broadcasted_iota(jnp.int32, (tr, 1), 0)
    prev_row = jnp.where(pos != 0, prev_ref[SUBLANES - 1:SUBLANES, :], 0.0)
    next_row = jnp.where(pos != seg - 1, next_ref[0:1, :], 0.0)
    prv = jnp.where(row == 0, prev_row, pltpu.roll(p, 1, 0))
    nxt = jnp.where(row == tr - 1, next_row, pltpu.roll(p, tr - 1, 0))
    xs = p + mu_ref[0:1, :] * (prv - p) + mu_ref[1:2, :] * (nxt - p)

    r = xs[:, 0:bw]
    k = xs[:, bw:2 * bw]
    v = xs[:, 2 * bw:3 * bw]
    lora = xs[:, 3 * bw:]
    n_hp = bw // LANES

    kk = k * kk_ref[...]
    kk = kk * lax.rsqrt(jnp.maximum(_group_sum(kk * kk, e_ref), 1e-24))
    ksum = jnp.zeros_like(k)
    for d, (s_ref, tri_ref) in enumerate(((sf_ref, tril_ref), (sb_ref, triu_ref))):
        wd = lora[:, d * LORA_PAD:(d + 1) * LORA_PAD]
        ad = lora[:, (2 + d) * LORA_PAD:(3 + d) * LORA_PAD]
        log_decay = -DECAY_SCALE * jax.nn.sigmoid(w0_ref[d:d + 1, :] + _dot3(jnp.tanh(wd), wup_ref[d]))
        a = jax.nn.sigmoid(a0_ref[d:d + 1, :] + _dot3(ad, aup_ref[d]))
        key = k * (1.0 + (a - 1.0) * ka_ref[...])
        ksum = ksum + key
        kka = kk * a
        cum = _tri3(tri_ref[...], log_decay)
        last = CHUNK - 1 if d == 0 else 0
        tot_rows = [cum[c * CHUNK + last:c * CHUNK + last + 1] for c in range(tr // CHUNK)]
        e_inv = jnp.exp(-cum)
        streams = (kk * jnp.exp(cum - log_decay), r * jnp.exp(cum), kka * e_inv, key * e_inv)
        for n, st in enumerate(streams):
            st = st.astype(BF16)
            for hp in range(n_hp):
                s_ref[n, hp] = st[:, hp * LANES:(hp + 1) * LANES]
        for cidx in range(tr // CHUNK):
            e_tot = jnp.exp(tot_rows[cidx])
            etot_ref[d, cidx] = jnp.concatenate(
                [e_tot[:, hp * LANES:(hp + 1) * LANES] for hp in range(n_hp)], axis=0)
    vb = v.astype(BF16)
    for hp in range(n_hp):
        sv_ref[hp] = vb[:, hp * LANES:(hp + 1) * LANES]
    bonus_ref[...] = _group_sum(r * ksum * rk_ref[...], e_ref) * v
    gate_ref[...] = _dot3(jax.nn.sigmoid(lora[:, 4 * LORA_PAD:]), gup_ref[...])


def _chunk_block_diag(tr, kind):
    t = jnp.arange(tr)
    same = (t[:, None] // CHUNK) == (t[None, :] // CHUNK)
    if kind == "lower":
        same = same & (t[None, :] <= t[:, None])
    elif kind == "upper":
        same = same & (t[None, :] >= t[:, None])
    return same.astype(BF16)


N_STREAMS = 4


def _rwkv_prep(p, lw, e_mat, cfg):
    rows = p.shape[0]
    bw, tr = cfg["b_width"], cfg["tr"]
    cw = 3 * bw + cfg["lora_w"]
    n_hp = bw // LANES
    n_lat_tiles = cfg["batch"] * cfg["seq"] // tr
    hb = tr // SUBLANES
    last8 = rows // SUBLANES - 1
    full = lambda a: pl.BlockSpec(a.shape, lambda i: (0,) * a.ndim)
    consts = [lw["mu"], lw["w0"], lw["w_up"], lw["a0"], lw["a_up"], lw["g_up"], lw["k_k"], lw["k_a"], lw["r_k"],
              e_mat, _chunk_block_diag(tr, "lower"), _chunk_block_diag(tr, "upper")]
    stream_spec = pl.BlockSpec((N_STREAMS, n_hp, tr, LANES), lambda i: (0, 0, i, 0))
    stream_shape = jax.ShapeDtypeStruct((N_STREAMS, n_hp, rows, LANES), BF16)
    cpt = tr // CHUNK
    kernel = functools.partial(_rwkv_prep_kernel, bw=bw, seg_lat=cfg["seq"] // tr, seg_ctx=cfg["ctx"] // tr,
                               n_lat_tiles=n_lat_tiles)
    return pl.pallas_call(
        kernel,
        grid=(rows // tr,),
        in_specs=[
            pl.BlockSpec((tr, cw), lambda i: (i, 0)),
            pl.BlockSpec((SUBLANES, cw), lambda i: (jnp.maximum(i * hb - 1, 0), 0)),
            pl.BlockSpec((SUBLANES, cw), lambda i: (jnp.minimum((i + 1) * hb, last8), 0)),
        ] + [full(a) for a in consts],
        out_specs=[pl.BlockSpec((n_hp, tr, LANES), lambda i: (0, i, 0)),
                   stream_spec, stream_spec,
                   pl.BlockSpec((2, cpt, n_hp, LANES), lambda i: (0, i, 0, 0)),
                   pl.BlockSpec((tr, bw), lambda i: (i, 0)),
                   pl.BlockSpec((tr, bw), lambda i: (i, 0))],
        out_shape=[jax.ShapeDtypeStruct((n_hp, rows, LANES), BF16),
                   stream_shape, stream_shape,
                   jax.ShapeDtypeStruct((2, rows // CHUNK, n_hp, LANES), F32),
                   jax.ShapeDtypeStruct((rows, bw), F32),
                   jax.ShapeDtypeStruct((rows, bw), F32)],
        compiler_params=_cparams(("parallel",)),
        name="rwkv_prep",
    )(p, p, p, *consts)


def _block_diag(x):
    lo = lax.broadcasted_iota(jnp.int32, x.shape, 1) < HEAD_DIM
    zero = jnp.zeros_like(x)
    return jnp.concatenate([jnp.where(lo, x, zero), jnp.where(lo, zero, x)], axis=0)


def _scan_step(dirs, ones_ref, z_ref, n_hp):
    c, hd = CHUNK, HEAD_DIM
    assert c == hd and LANES == 2 * hd
    row = lax.broadcasted_iota(jnp.int32, (c, LANES), 0)
    lane = lax.broadcasted_iota(jnp.int32, (c, LANES), 1)
    col = lane & (hd - 1)
    lo = lane < hd
    eye = col == row
    inst = [(d, hp) for d in range(len(dirs)) for hp in range(n_hp)]
    pairs = range(len(inst))
    incl = [(col >= row) if dirs[d][4] else (col <= row) for d, _ in inst]
    strict = [(col > row) if dirs[d][4] else (col < row) for d, _ in inst]

    def half(x, h):
        keep = lo if h == 0 else ~lo
        return jnp.where(keep, x, jnp.zeros_like(x))

    qk, rt, bt, kt = ([dirs[d][1][k, hp] for d, hp in inst] for k in range(N_STREAMS))
    e_row = [dirs[d][2][hp:hp + 1, :] for d, hp in inst]
    bh = [(bt[i].astype(F32) * e_row[i]).astype(BF16) for i in pairs]
    kh = [(kt[i].astype(F32) * e_row[i]).astype(BF16) for i in pairs]
    vv = [dirs[d][0][hp] for d, hp in inst]
    p1 = [_dot(jnp.concatenate([qk[i], rt[i]], axis=0),
               jnp.concatenate([half(bt[i], 0), half(bt[i], 1), half(kt[i], 0), half(kt[i], 1)], axis=0),
               trans_b=True) for i in pairs]
    same_blk = (row // SUB) == (col // SUB)
    a_ab = [jnp.where(strict[i], p[:c, :LANES], 0.0) for i, p in enumerate(p1)]
    x_pow = [jnp.where(same_blk, -a, 0.0) for a in a_ab]
    u = x_pow
    x_pow = [_dot(x.astype(BF16), _block_diag(x.astype(BF16))) for x in x_pow]
    sq = 2
    while sq < SUB:
        w = [_block_diag(x.astype(BF16)) for x in x_pow]
        if 2 * sq < SUB:
            prod = [_dot(jnp.concatenate([u[i], x_pow[i]], axis=0).astype(BF16), w[i]) for i in pairs]
            u = [u[i] + x_pow[i] + prod[i][:c] for i in pairs]
            x_pow = [pr[c:] for pr in prod]
        else:
            u = [u[i] + x_pow[i] + _dot(u[i].astype(BF16), w[i]) for i in pairs]
        sq *= 2
    u_bf = [x.astype(BF16) for x in u]
    n_off = [jnp.where(same_blk, 0.0, a) for a in a_ab]
    m1_bf = [(-(n_off[i] + _dot(u_bf[i], _block_diag(n_off[i].astype(BF16))))).astype(BF16) for i in pairs]
    m2_bf = [_dot(m1_bf[i], _block_diag(m1_bf[i])).astype(BF16) for i in pairs]
    avy = [_dot(jnp.concatenate([jnp.where(strict[i], p[:c, LANES:], 0.0), jnp.where(incl[i], p[c:, LANES:], 0.0)],
                                axis=0).astype(BF16), _block_diag(vv[i])) for i, p in enumerate(p1)]

    def pair_cols(x):
        return jnp.concatenate([_block_diag(x[:, :LANES].astype(BF16)), _block_diag(x[:, LANES:].astype(BF16))],
                               axis=1)

    r = [jnp.concatenate([qk[i].astype(F32), avy[i][:c]], axis=1) for i in pairs]
    r = [r[i] + _dot(u_bf[i], pair_cols(r[i])) for i in pairs]
    r = [r[i] + _dot(m2_bf[i], pair_cols(r[i])) for i in pairs]
    tu = [(r[i] + _dot(m1_bf[i], pair_cols(r[i]))).astype(BF16) for i in pairs]
    bu = [_dot(jnp.where(incl[i], p[c:, :LANES], 0.0).astype(BF16),
               jnp.concatenate([_block_diag(tu[i][:, :LANES]), _block_diag(tu[i][:, LANES:])], axis=1))
          for i, p in enumerate(p1)]
    rh = [(rt[i].astype(F32) - bu[i][:, :LANES]).astype(BF16) for i in pairs]
    yh = [avy[i][c:] - bu[i][:, LANES:] for i in pairs]
    rp = [_dot(bh[i], tu[i], trans_a=True) for i in pairs]
    kv = [_dot(kh[i], vv[i], trans_a=True) for i in pairs]
    g = [jnp.where(lo, r[:hd, :LANES], r[hd:, :LANES]).astype(BF16) for r in rp]
    hc = [jnp.where(lo, kv[i][:hd], kv[i][hd:]) - jnp.where(lo, rp[i][:hd, LANES:], rp[i][hd:, LANES:])
          for i in pairs]
    ones2 = jnp.concatenate([ones_ref[...], ones_ref[...]], axis=0)
    e_col = []
    for i in pairs:
        e_diag = jnp.where(eye, jnp.broadcast_to(e_row[i], (c, LANES)), 0.0)
        e_col.append(_dot(jnp.concatenate(_split(e_diag), axis=1), ones2))
    z = [z_ref[i] for i in pairs]
    yz = [_dot(jnp.concatenate([rh[i], g[i]], axis=0), _block_diag(z[i].astype(BF16))) for i in pairs]
    for i, (d, hp) in enumerate(inst):
        dirs[d][3][hp] = yh[i] + yz[i][:c]
        z_ref[i] = e_col[i] * z[i] - yz[i][c:] + hc[i]


def _rwkv_scan_kernel(vf_ref, sf_ref, ef_ref, vb_ref, sb_ref, eb_ref, ones_ref, yf_ref, yb_ref, z_ref, *, n_hp):
    @pl.when(pl.program_id(1) == 0)
    def _():
        z_ref[...] = jnp.zeros_like(z_ref)

    _scan_step([(vf_ref, sf_ref, ef_ref, yf_ref, False), (vb_ref, sb_ref, eb_ref, yb_ref, True)],
               ones_ref, z_ref, n_hp)


def _rwkv_scan(s_v, s_f, s_b, e_tot, cfg):
    n_hp, rows, _ = s_v.shape
    b, seq, ctx = cfg["batch"], cfg["seq"], cfg["ctx"]
    c = CHUNK
    ncc, ncl = ctx // c, seq // c
    ctx0 = b * seq // c

    def fwd_blk(bi, j):
        return jnp.where(j < ncc, ctx0 + bi * ncc + j, bi * ncl + (j - ncc))

    def bwd_blk(bi, j):
        return jnp.where(j < ncc, ctx0 + bi * ncc + (ncc - 1 - j), bi * ncl + (ncl - 1 - (j - ncc)))

    def specs(blk, d):
        return [pl.BlockSpec((n_hp, c, LANES), lambda bi, j: (0, blk(bi, j), 0)),
                pl.BlockSpec((N_STREAMS, n_hp, c, LANES), lambda bi, j: (0, 0, blk(bi, j), 0)),
                pl.BlockSpec((None, None, n_hp, LANES), lambda bi, j: (d, blk(bi, j), 0, 0))]

    out_f = pl.BlockSpec((n_hp, c, LANES), lambda bi, j: (0, fwd_blk(bi, j), 0))
    out_b = pl.BlockSpec((n_hp, c, LANES), lambda bi, j: (0, bwd_blk(bi, j), 0))
    y_shape = jax.ShapeDtypeStruct((n_hp, rows, LANES), F32)
    return pl.pallas_call(
        functools.partial(_rwkv_scan_kernel, n_hp=n_hp),
        grid=(b, ncc + ncl),
        in_specs=specs(fwd_blk, 0) + specs(bwd_blk, 1) + [pl.BlockSpec((LANES, LANES), lambda bi, j: (0, 0))],
        out_specs=[out_f, out_b],
        out_shape=[y_shape, y_shape],
        scratch_shapes=[pltpu.VMEM((2 * n_hp, HEAD_DIM, LANES), F32)],
        compiler_params=_cparams(("parallel", "arbitrary")),
        name="rwkv_scan",
    )(s_v, s_f, e_tot, s_v, s_b, e_tot, _block_ones(LANES))


def _outproj_kernel(oa_ref, yf_ref, yb_ref, bonus_ref, gate_ref, gg_ref, gb_ref, e_ref, oc_ref, w_ref, h_ref,
                    g_ref, n2_ref, sc_ref, sh_ref, o_ref, u_ref):
    n_hp = yf_ref.shape[0]
    y = jnp.concatenate([yf_ref[hp] + yb_ref[hp] for hp in range(n_hp)], axis=1)
    mu = _group_sum(y, e_ref) * (1.0 / HEAD_DIM)
    yc = y - mu
    var = _group_sum(yc * yc, e_ref) * (1.0 / HEAD_DIM)
    yn = yc * lax.rsqrt(var + GN_EPS) * gg_ref[...] + gb_ref[...]
    o_b = ((yn + bonus_ref[...]) * gate_ref[...]).astype(BF16)
    mix = jnp.concatenate([oa_ref[...], o_b, oc_ref[...]], axis=1)
    h_new = h_ref[...] + g_ref[...] * _dot(mix, w_ref[...])
    o_ref[...] = h_new
    u_ref[...] = _modulated_norm(h_new, n2_ref[...], sc_ref[...], sh_ref[...]).astype(BF16)


def _outproj(o_a, y_f, y_b, bonus, gate, gn_g, gn_b, e_mat, o_c, w, h, mods, norm2_g, l, n_rows, cfg):
    d = h.shape[1]
    n_hp = y_f.shape[0]
    bw, tm = cfg["b_width"], cfg["tm"] // 2
    modrow = cfg["modrow"](tm)
    rows_of = lambda width: pl.BlockSpec((tm, width), lambda i: (i, 0))
    y_spec = pl.BlockSpec((n_hp, tm, LANES), lambda i: (0, i, 0))
    vec_spec = pl.BlockSpec((None, 1, bw), lambda i: (l, 0, 0))
    mod = lambda k: pl.BlockSpec((None, None, None, 1, d), lambda i: (l, modrow(i), k, 0, 0))
    return pl.pallas_call(
        _outproj_kernel,
        grid=(n_rows // tm,),
        in_specs=[rows_of(o_a.shape[1]), y_spec, y_spec, rows_of(bw), rows_of(bw), vec_spec, vec_spec,
                  pl.BlockSpec(e_mat.shape, lambda i: (0, 0)),
                  rows_of(o_c.shape[1]),
                  pl.BlockSpec((None, d, d), lambda i: (l, 0, 0)),
                  rows_of(d),
                  mod(2),
                  pl.BlockSpec((None, 1, d), lambda i: (l, 0, 0)),
                  mod(4), mod(3)],
        out_specs=[rows_of(d), rows_of(d)],
        out_shape=[jax.ShapeDtypeStruct((n_rows, d), F32), jax.ShapeDtypeStruct((n_rows, d), BF16)],
        compiler_params=_cparams(("parallel",)),
        name="out_proj",
    )(o_a, y_f, y_b, bonus, gate, gn_g, gn_b, e_mat, o_c, w, h, mods, norm2_g, mods, mods)


def _mlp_kernel(u_ref, hcol_ref, gate_ref, w1_ref, w2_ref, o_ref, mid_scr, *, n_up):
    s = pl.program_id(1)

    @pl.when(s < n_up)
    def _():
        a = jnp.maximum(_dot(u_ref[...], w1_ref[...]), 0.0)
        mid_scr[s] = (a * a).astype(BF16)

    @pl.when(s >= n_up)
    def _():
        mid = jnp.concatenate([mid_scr[k] for k in range(n_up)], axis=1)
        o_ref[...] = hcol_ref[...] + gate_ref[...] * _dot(mid, w2_ref[...])


def _mlp(h, u, mods, w1, w2, l, cfg):
    rows, d = h.shape
    dff = w1.shape[2]
    tm, tf, tn = cfg["tm"], 2048, 512
    n_up = dff // tf
    modrow = cfg["modrow"](tm)
    col = lambda s: jnp.maximum(s - n_up, 0)
    return pl.pallas_call(
        functools.partial(_mlp_kernel, n_up=n_up),
        grid=(rows // tm, n_up + d // tn),
        in_specs=[
            pl.BlockSpec((tm, d), lambda i, s: (i, 0)),
            pl.BlockSpec((tm, tn), lambda i, s: (i, col(s))),
            pl.BlockSpec((None, None, None, 1, tn), lambda i, s: (l, modrow(i), 5, 0, col(s))),
            pl.BlockSpec((None, d, tf), lambda i, s: (l, 0, jnp.minimum(s, n_up - 1))),
            pl.BlockSpec((None, dff, tn), lambda i, s: (l, 0, col(s))),
        ],
        out_specs=pl.BlockSpec((tm, tn), lambda i, s: (i, col(s))),
        out_shape=jax.ShapeDtypeStruct((rows, d), F32),
        scratch_shapes=[pltpu.VMEM((n_up, tm, tf), BF16)],
        compiler_params=_cparams(("parallel", "arbitrary")),
        name="mlp",
    )(u, h, mods, w1, w2)


def _rope_tables(seq, tm):
    rows = seq // GRID_W
    row = jnp.broadcast_to(jnp.arange(rows)[:, None], (rows, GRID_W)).reshape(-1)
    col = jnp.broadcast_to(jnp.arange(GRID_W)[None, :], (rows, GRID_W)).reshape(-1)
    n_freq = HEAD_DIM // 4
    inv = ROPE_THETA ** (-jnp.arange(n_freq, dtype=F32) / n_freq)
    ang = jnp.concatenate([row[:, None].astype(F32) * inv, col[:, None].astype(F32) * inv], -1)
    cos, sin = jnp.cos(ang), jnp.sin(ang)
    reps = LANES // HEAD_DIM
    cos_t = jnp.tile(jnp.concatenate([cos, cos], -1), (1, reps))
    sin_t = jnp.tile(jnp.concatenate([-sin, sin], -1), (1, reps))
    cos_t = jnp.concatenate([cos_t, jnp.ones((tm, LANES), F32)], 0)
    sin_t = jnp.concatenate([sin_t, jnp.zeros((tm, LANES), F32)], 0)
    return cos_t, sin_t


def _block_ones(width):
    g = jnp.arange(width) // HEAD_DIM
    return (g[:, None] == g[None, :]).astype(BF16)


def kernel(x, c, ctx, c_ctx, ada_w, ada_b, norm1_g, norm2_g, w_in, a_q_norm, a_k_norm, a_sink, c_q_norm,
           c_k_norm, shift_mu, decay_w0, decay_up, iclr_a0, iclr_up, gate_up, k_k, k_a, r_k, gn_g, gn_b,
           w_out, mlp_w1, mlp_w2):
    batch, seq, d = x.shape
    n_ctx = ctx.shape[1]
    depth = ada_w.shape[0]
    bw = k_k.shape[1]
    lora_d, lora_i, lora_g = decay_up.shape[2], iclr_up.shape[2], gate_up.shape[1]
    a_heads = a_sink.shape[1]
    a_kv = a_heads // 4
    q_w, kv_w = a_heads * HEAD_DIM, a_kv * HEAD_DIM
    attn_w = q_w + 2 * kv_w
    lora_w = 4 * LORA_PAD + lora_g
    b_in = 3 * bw + 2 * lora_d + 2 * lora_i + lora_g
    assert lora_d <= LORA_PAD and lora_i <= LORA_PAD
    assert w_in.shape[2] == 2 * attn_w + b_in

    tm = batch * n_ctx
    n_lat_tiles = batch * seq // tm
    tiles_per_batch = seq // tm
    cfg = dict(
        batch=batch, seq=seq, ctx=n_ctx, tm=tm, tr=n_ctx, tq=n_ctx, tk=min(seq, 2048),
        b_width=bw, lora_w=lora_w, attn_w=attn_w, attn_col0=3 * bw + lora_w, qk_w=q_w + kv_w,
        attn_heads=a_heads, attn_kv=a_kv,
        modrow=lambda t: (lambda i: jnp.where(i < batch * seq // t, i // (seq // t), batch)),
        rope_blk=lambda i: jnp.where(i < n_lat_tiles, i % tiles_per_batch, tiles_per_batch),
    )
    assert seq % tm == 0 and seq % GRID_W == 0 and n_ctx % CHUNK == 0 and (3 * bw) % lora_w == 0
    assert cfg["attn_col0"] % attn_w == 0 and batch + 1 <= SUBLANES

    def relayout_cols(m):
        a_part, b_part, c_part = m[..., :attn_w], m[..., attn_w:attn_w + b_in], m[..., attn_w + b_in:]
        pad = lambda z, n: jnp.pad(z, [(0, 0)] * (z.ndim - 1) + [(0, n - z.shape[-1])])
        o = 3 * bw
        pieces = [b_part[..., :o]]
        for width in (lora_d, lora_d, lora_i, lora_i):
            pieces.append(pad(b_part[..., o:o + width], LORA_PAD))
            o += width
        pieces.append(b_part[..., o:])
        return jnp.concatenate(pieces + [a_part, c_part], -1)

    w_in_r = relayout_cols(w_in).astype(BF16)
    mu_r = relayout_cols(jnp.pad(shift_mu, ((0, 0), (0, 0), (attn_w, attn_w))))[..., :3 * bw + lora_w]
    pad_rows = lambda z: jnp.pad(z, ((0, 0), (0, 0), (0, LORA_PAD - z.shape[2]), (0, 0)))
    w_up_r, a_up_r = pad_rows(decay_up), pad_rows(iclr_up)
    w_out_b = w_out.astype(BF16)
    w1_b, w2_b = mlp_w1.astype(BF16), mlp_w2.astype(BF16)

    scale = HEAD_DIM ** -0.5 * LOG2E
    tile = lambda g, n: jnp.tile(g, (1, n))

    def gains(qg, kg):
        return jnp.concatenate([tile(qg, a_heads) * scale, tile(kg, a_kv), jnp.ones((depth, kv_w), F32)], -1)

    qk_gains = jnp.stack([gains(a_q_norm, a_k_norm), gains(c_q_norm, c_k_norm)], 1)[:, :, None, :]

    cos_t, sin_t = _rope_tables(seq, tm)
    e_attn = _block_ones(attn_w)
    e_b = _block_ones(2 * LANES)

    cvec = jnp.zeros((SUBLANES, d), F32).at[:batch].set(c).at[batch].set(c_ctx)
    mods = _mods(cvec, ada_w, ada_b).reshape(depth, SUBLANES, 6, 1, d)

    h = jnp.concatenate([x.reshape(batch * seq, d), ctx.reshape(batch * n_ctx, d)], 0)
    n1g, n2g = norm1_g[:, None, :], norm2_g[:, None, :]
    gn_g3, gn_b3 = gn_g[:, None, :], gn_b[:, None, :]
    for l in range(depth):
        p = _inproj(h, n1g, mods, w_in_r, l, cfg)
        qkv = _qkprep(p, qk_gains[l], cos_t, sin_t, e_attn, cfg)
        o_a = _attention(qkv, 0, a_sink[l], cfg, window=True)
        o_c = _attention(qkv, 1, None, cfg, window=False)
        lw = dict(mu=mu_r[l], w0=decay_w0[l], w_up=w_up_r[l], a0=iclr_a0[l], a_up=a_up_r[l], g_up=gate_up[l],
                  k_k=k_k[l][None], k_a=k_a[l][None], r_k=r_k[l][None])
        s_v, s_f, s_b, e_tot, bonus, gate = _rwkv_prep(p, lw, e_b, cfg)
        y_f, y_b = _rwkv_scan(s_v, s_f, s_b, e_tot, cfg)
        n_rows = batch * seq if l == depth - 1 else h.shape[0]
        h, u2 = _outproj(o_a, y_f, y_b, bonus, gate, gn_g3, gn_b3, e_b, o_c, w_out_b, h, mods, n2g, l, n_rows, cfg)
        h = _mlp(h, u2, mods, w1_b, w2_b, l, cfg)
    return h.reshape(batch, seq, d)
```

```python
import functools

import jax
import jax.numpy as jnp
from jax import lax
from jax.experimental import pallas as pl
from jax.experimental.pallas import tpu as pltpu

F32 = jnp.float32
BF16 = jnp.bfloat16

HEAD_DIM = 64
GRID_W = 64
WINDOW = 128
ROPE_THETA = 10000.0
NORM_EPS = 1e-6
GN_EPS = 64e-5
NEG_INF = -1e30
LOG2E = 1.4426950408889634
DECAY_SCALE = 0.6065306597126334
LANES = 128
SUBLANES = 8
LORA_PAD = LANES
SUB = 16
CHUNK = 64
VMEM_LIMIT = 52 * 1024 * 1024


def _dot(a, b, trans_a=False, trans_b=False):
    dn = (((0 if trans_a else 1,), (1 if trans_b else 0,)), ((), ()))
    return lax.dot_general(a, b, dn, preferred_element_type=F32)


def _split(x):
    hi = x.astype(BF16)
    lo = (x - hi.astype(F32)).astype(BF16)
    return hi, lo


def _dot3(a, b, trans_a=False, trans_b=False):
    ah, al = _split(a)
    bh, bl = _split(b)
    kw = dict(trans_a=trans_a, trans_b=trans_b)
    return _dot(ah, bh, **kw) + (_dot(al, bh, **kw) + _dot(ah, bl, **kw))


def _group_sum(x, e_ref):
    gw = e_ref.shape[0]
    e = e_ref[...]
    outs = []
    for g in range(x.shape[1] // gw):
        hi, lo = _split(x[:, g * gw:(g + 1) * gw])
        outs.append(_dot(hi, e) + _dot(lo, e))
    return outs[0] if len(outs) == 1 else jnp.concatenate(outs, axis=1)


def _cparams(sem):
    return pltpu.CompilerParams(dimension_semantics=sem, vmem_limit_bytes=VMEM_LIMIT)


def _mods_kernel(c_ref, w_ref, b_ref, o_ref):
    c = c_ref[...]
    s = c * jax.nn.sigmoid(c)
    o_ref[0] = _dot3(s, w_ref[0]) + b_ref[0]


def _mods(cvec, ada_w, ada_b):
    depth, d, n = ada_w.shape
    tn = 512
    return pl.pallas_call(
        _mods_kernel,
        grid=(depth, n // tn),
        in_specs=[
            pl.BlockSpec((SUBLANES, d), lambda l, j: (0, 0)),
            pl.BlockSpec((1, d, tn), lambda l, j: (l, 0, j)),
            pl.BlockSpec((1, 1, tn), lambda l, j: (l, 0, j)),
        ],
        out_specs=pl.BlockSpec((1, SUBLANES, tn), lambda l, j: (l, 0, j)),
        out_shape=jax.ShapeDtypeStruct((depth, SUBLANES, n), F32),
        compiler_params=_cparams(("parallel", "parallel")),
        name="adaln_mods",
    )(cvec, ada_w, ada_b.reshape(depth, 1, n))


def _modulated_norm(x, g, sc, sh):
    ms = jnp.mean(x * x, axis=-1, keepdims=True)
    return (x * lax.rsqrt(ms + NORM_EPS) * g) * (1.0 + sc) + sh


def _inproj_kernel(h_ref, g_ref, sc_ref, sh_ref, w_ref, o_ref, u_scr):
    @pl.when(pl.program_id(1) == 0)
    def _():
        u_scr[...] = _modulated_norm(h_ref[...], g_ref[...], sc_ref[...], sh_ref[...]).astype(BF16)

    o_ref[...] = _dot(u_scr[...], w_ref[...])


def _inproj(h, gain, mods, w, l, cfg):
    rows, d = h.shape
    n = w.shape[2]
    tm, tn = cfg["tm"], 2688
    modrow = cfg["modrow"](tm)
    return pl.pallas_call(
        _inproj_kernel,
        grid=(rows // tm, n // tn),
        in_specs=[
            pl.BlockSpec((tm, d), lambda i, j: (i, 0)),
            pl.BlockSpec((None, 1, d), lambda i, j: (l, 0, 0)),
            pl.BlockSpec((None, None, None, 1, d), lambda i, j: (l, modrow(i), 1, 0, 0)),
            pl.BlockSpec((None, None, None, 1, d), lambda i, j: (l, modrow(i), 0, 0, 0)),
            pl.BlockSpec((None, d, tn), lambda i, j: (l, 0, j)),
        ],
        out_specs=pl.BlockSpec((tm, tn), lambda i, j: (i, j)),
        out_shape=jax.ShapeDtypeStruct((rows, n), F32),
        scratch_shapes=[pltpu.VMEM((tm, d), BF16)],
        compiler_params=_cparams(("parallel", "arbitrary")),
        name="in_proj",
    )(h, gain, mods, mods, w)


def _qkprep_kernel(p_ref, gain_ref, cos_ref, sin_ref, e_ref, o_ref, *, qk_w):
    x = p_ref[...]
    width = x.shape[1]
    ss = _group_sum(x * x, e_ref)
    y = x * lax.rsqrt(ss * (1.0 / HEAD_DIM) + NORM_EPS) * gain_ref[0]
    cos = cos_ref[...]
    sin = sin_ref[...]
    lane = lax.broadcasted_iota(jnp.int32, (x.shape[0], LANES), 1)
    first_half = (lane & (HEAD_DIM // 2)) == 0
    for g in range(width // LANES):
        sl = slice(g * LANES, (g + 1) * LANES)
        if g * LANES < qk_w:
            yg = y[:, sl]
            partner = jnp.where(first_half, pltpu.roll(yg, LANES - HEAD_DIM // 2, 1),
                                pltpu.roll(yg, HEAD_DIM // 2, 1))
            o_ref[0, :, sl] = (yg * cos + partner * sin).astype(BF16)
        else:
            o_ref[0, :, sl] = x[:, sl].astype(BF16)


def _qkprep(p, gains, cos_t, sin_t, e_mat, cfg):
    rows = p.shape[0]
    tm, aw = cfg["tm"], cfg["attn_w"]
    first_blk = cfg["attn_col0"] // aw
    rope_blk = cfg["rope_blk"]
    return pl.pallas_call(
        functools.partial(_qkprep_kernel, qk_w=cfg["qk_w"]),
        grid=(rows // tm, 2),
        in_specs=[
            pl.BlockSpec((tm, aw), lambda i, s: (i, first_blk + s)),
            pl.BlockSpec((1, 1, aw), lambda i, s: (s, 0, 0)),
            pl.BlockSpec((tm, LANES), lambda i, s: (rope_blk(i), 0)),
            pl.BlockSpec((tm, LANES), lambda i, s: (rope_blk(i), 0)),
            pl.BlockSpec(e_mat.shape, lambda i, s: (0, 0)),
        ],
        out_specs=pl.BlockSpec((1, tm, aw), lambda i, s: (s, i, 0)),
        out_shape=jax.ShapeDtypeStruct((2, rows, aw), BF16),
        compiler_params=_cparams(("parallel", "parallel")),
        name="qk_prep",
    )(p, gains, cos_t, sin_t, e_mat)


def _attn_kernel(sink_ref, q_ref, kc_ref, vc_ref, kl_ref, vl_ref, o_ref, *,
                 window, n_heads, group, tq, tk, seq):
    assert sink_ref is None and not window
    i = pl.program_id(1)
    is_lat = i < seq // tq
    hd = HEAD_DIM
    n_kv = n_heads // group
    den_lane = [((j + 1) % n_kv) * hd for j in range(n_kv)]

    def with_ones(vblk, j):
        lane = lax.broadcasted_iota(jnp.int32, vblk.shape, 1)
        ones_col = jnp.where(lane == den_lane[j], 1.0, 0.0).astype(BF16)
        return jnp.where((lane >= j * hd) & (lane < (j + 1) * hd), vblk, ones_col)

    def padded_q(j):
        zeros = jnp.zeros((tq, hd), BF16)
        qs = []
        for g in range(group):
            h = j * group + g
            parts = [zeros] * n_kv
            parts[j] = q_ref[0, :, h * hd:(h + 1) * hd]
            qs.append(jnp.concatenate(parts, axis=1))
        return jnp.concatenate(qs, axis=0)

    def finish(acc, j):
        o = acc[:, j * hd:(j + 1) * hd] / acc[:, den_lane[j]:den_lane[j] + 1]
        for g in range(group):
            c0 = (j * group + g) * hd
            o_ref[:, c0:c0 + hd] = o[g * tq:(g + 1) * tq].astype(BF16)

    @pl.when(is_lat)
    def _():
        for j in range(n_kv):
            q = padded_q(j)
            s_ctx = _dot(q, kc_ref[0], trans_b=True)
            s_lat = _dot(q, kl_ref[0, 0:tk, :], trans_b=True)
            m = jnp.maximum(jnp.max(s_ctx, axis=-1, keepdims=True), jnp.max(s_lat, axis=-1, keepdims=True))
            acc = (_dot(jnp.exp2(s_ctx - m).astype(BF16), with_ones(vc_ref[0], j))
                   + _dot(jnp.exp2(s_lat - m).astype(BF16), with_ones(vl_ref[0, 0:tk, :], j)))

            def body(kb, carry, q=q, j=j):
                m, acc = carry
                off = pl.multiple_of(kb * tk, tk)
                s = _dot(q, kl_ref[0, pl.ds(off, tk), :], trans_b=True)
                m_new = jnp.maximum(m, jnp.max(s, axis=-1, keepdims=True))
                p = jnp.exp2(s - m_new).astype(BF16)
                acc_new = jnp.exp2(m - m_new) * acc + _dot(p, with_ones(vl_ref[0, pl.ds(off, tk), :], j))
                return m_new, acc_new

            m, acc = lax.fori_loop(1, seq // tk, body, (m, acc))
            finish(acc, j)

    @pl.when(jnp.logical_not(is_lat))
    def _():
        for j in range(n_kv):
            s_ctx = _dot(padded_q(j), kc_ref[0], trans_b=True)
            p = jnp.exp2(s_ctx - jnp.max(s_ctx, axis=-1, keepdims=True)).astype(BF16)
            finish(_dot(p, with_ones(vc_ref[0], j)), j)


def _window_attn_kernel(sink_ref, q_ref, kc_ref, vc_ref, kl_ref, vl_ref, o_ref, *,
                        window, n_heads, group, tq, tk, seq):
    del window, tk
    i = pl.program_id(1)
    is_lat = i < seq // tq
    hd = HEAD_DIM
    n_kv = n_heads // group
    rows = group * tq
    span = tq + 2 * WINDOW
    den_lane = [((j + 1) % n_kv) * hd for j in range(n_kv)]
    start = pl.multiple_of(jnp.clip(i * tq - WINDOW, 0, seq - span), WINDOW)
    delta = (lax.broadcasted_iota(jnp.int32, (tq, span), 0) - lax.broadcasted_iota(jnp.int32, (tq, span), 1)
             + (i * tq - start))
    bias = jnp.where(jnp.abs(delta) <= jnp.where(is_lat, WINDOW, -1), 0.0, NEG_INF)
    k_ctx, v_ctx = kc_ref[0], vc_ref[0]
    k_win, v_win = kl_ref[0, pl.ds(start, span), :], vl_ref[0, pl.ds(start, span), :]

    def with_ones(vblk, j):
        lane = lax.broadcasted_iota(jnp.int32, vblk.shape, 1)
        ones_col = jnp.where(lane == den_lane[j], 1.0, 0.0).astype(BF16)
        return jnp.where((lane >= j * hd) & (lane < (j + 1) * hd), vblk, ones_col)

    zeros = jnp.zeros((tq, hd), BF16)
    for j in range(n_kv):
        qs = []
        for g in range(group):
            h = j * group + g
            parts = [zeros] * n_kv
            parts[j] = q_ref[0, :, h * hd:(h + 1) * hd]
            qs.append(jnp.concatenate(parts, axis=1))
        q = jnp.concatenate(qs, axis=0)
        m0 = jnp.concatenate(
            [jnp.full((tq, 1), sink_ref[j * group + g] * LOG2E, F32) for g in range(group)], axis=0)
        s_ctx = _dot(q, k_ctx, trans_b=True)
        s_win = (_dot(q, k_win, trans_b=True).reshape(group, tq, span) + bias[None]).reshape(rows, span)
        m = jnp.maximum(m0, jnp.maximum(jnp.max(s_ctx, axis=-1, keepdims=True),
                                        jnp.max(s_win, axis=-1, keepdims=True)))
        acc = (_dot(jnp.exp2(s_ctx - m).astype(BF16), with_ones(v_ctx, j))
               + _dot(jnp.exp2(s_win - m).astype(BF16), with_ones(v_win, j)))
        den = acc[:, den_lane[j]:den_lane[j] + 1] + jnp.exp2(m0 - m)
        o = acc[:, j * hd:(j + 1) * hd] / den
        for g in range(group):
            c0 = (j * group + g) * hd
            o_ref[:, c0:c0 + hd] = o[g * tq:(g + 1) * tq].astype(BF16)


def _attention(qkv, sec, sink, cfg, window):
    rows = qkv.shape[1]
    b, seq, ctx, tq = cfg["batch"], cfg["seq"], cfg["ctx"], cfg["tq"]
    n_heads, n_kv = cfg["attn_heads"], cfg["attn_kv"]
    q_w = n_heads * HEAD_DIM
    kv_w = n_kv * HEAD_DIM
    assert kv_w == LANES and q_w % kv_w == 0 and ctx == tq
    n_lat_tiles = seq // tq
    k_blk = q_w // kv_w
    ctx_blk0 = b * seq // ctx

    def q_map(bi, i, *_):
        return (sec, jnp.where(i < n_lat_tiles, bi * n_lat_tiles + i, ctx_blk0 + bi), 0)

    def o_map(bi, i, *_):
        return (jnp.where(i < n_lat_tiles, bi * n_lat_tiles + i, ctx_blk0 + bi), 0)

    tk = cfg["tk"]
    assert seq % tk == 0 and tq % WINDOW == 0 and tq + 2 * WINDOW <= seq
    assert (sink is not None) == window
    kernel = functools.partial(_window_attn_kernel if window else _attn_kernel, window=window, n_heads=n_heads,
                               group=n_heads // n_kv, tq=tq, tk=tk, seq=seq)
    in_specs = [
        pl.BlockSpec((1, tq, q_w), q_map),
        pl.BlockSpec((1, ctx, kv_w), lambda bi, i, *_: (sec, ctx_blk0 + bi, k_blk)),
        pl.BlockSpec((1, ctx, kv_w), lambda bi, i, *_: (sec, ctx_blk0 + bi, k_blk + 1)),
        pl.BlockSpec((1, seq, kv_w), lambda bi, i, *_: (sec, bi, k_blk)),
        pl.BlockSpec((1, seq, kv_w), lambda bi, i, *_: (sec, bi, k_blk + 1)),
    ]
    args = [qkv, qkv, qkv, qkv, qkv]
    if sink is not None:
        in_specs = [pl.BlockSpec(memory_space=pltpu.SMEM)] + in_specs
        args = [sink] + args
    else:
        kernel = functools.partial(kernel, None)
    return pl.pallas_call(
        kernel,
        grid=(b, n_lat_tiles + 1),
        in_specs=in_specs,
        out_specs=pl.BlockSpec((tq, q_w), o_map),
        out_shape=jax.ShapeDtypeStruct((rows, q_w), BF16),
        compiler_params=_cparams(("parallel", "arbitrary")),
        name="window_attn" if window else "global_attn",
    )(*args)


def _tri3(tri, x):
    h1 = x.astype(BF16)
    r1 = x - h1.astype(F32)
    h2 = r1.astype(BF16)
    h3 = (r1 - h2.astype(F32)).astype(BF16)
    return _dot(tri, h1) + (_dot(tri, h2) + _dot(tri, h3))


def _rwkv_prep_kernel(cur_ref, prev_ref, next_ref, mu_ref, w0_ref, wup_ref, a0_ref, aup_ref, gup_ref,
                      kk_ref, ka_ref, rk_ref, e_ref, tril_ref, triu_ref,
                      sv_ref, sf_ref, sb_ref, etot_ref, bonus_ref, gate_ref, *,
                      bw, seg_lat, seg_ctx, n_lat_tiles):
    i = pl.program_id(0)
    p = cur_ref[...]
    tr = p.shape[0]
    lat = i < n_lat_tiles
    seg = jnp.where(lat, seg_lat, seg_ctx)
    pos = jnp.where(lat, i, i - n_lat_tiles) % seg
    row = lax.broadcasted_iota(jnp.int32, (tr, 1), 0)
    prev_row = jnp.where(pos != 0, prev_ref[SUBLANES - 1:SUBLANES, :], 0.0)
    next_row = jnp.where(pos != seg - 1, next_ref[0:1, :], 0.0)
    prv = jnp.where(row == 0, prev_row, pltpu.roll(p, 1, 0))
    nxt = jnp.where(row == tr - 1, next_row, pltpu.roll(p, tr - 1, 0))
    xs = p + mu_ref[0:1, :] * (prv - p) + mu_ref[1:2, :] * (nxt - p)

    r = xs[:, 0:bw]
    k = xs[:, bw:2 * bw]
    v = xs[:, 2 * bw:3 * bw]
    lora = xs[:, 3 * bw:]
    n_hp = bw // LANES

    kk = k * kk_ref[...]
    kk = kk * lax.rsqrt(jnp.maximum(_group_sum(kk * kk, e_ref), 1e-24))
    ksum = jnp.zeros_like(k)
    for d, (s_ref, tri_ref) in enumerate(((sf_ref, tril_ref), (sb_ref, triu_ref))):
        wd = lora[:, d * LORA_PAD:(d + 1) * LORA_PAD]
        ad = lora[:, (2 + d) * LORA_PAD:(3 + d) * LORA_PAD]
        log_decay = -DECAY_SCALE * jax.nn.sigmoid(w0_ref[d:d + 1, :] + _dot3(jnp.tanh(wd), wup_ref[d]))
        a = jax.nn.sigmoid(a0_ref[d:d + 1, :] + _dot3(ad, aup_ref[d]))
        key = k * (1.0 + (a - 1.0) * ka_ref[...])
        ksum = ksum + key
        kka = kk * a
        cum = _tri3(tri_ref[...], log_decay)
        last = CHUNK - 1 if d == 0 else 0
        tot_rows = [cum[c * CHUNK + last:c * CHUNK + last + 1] for c in range(tr // CHUNK)]
        e_inv = jnp.exp(-cum)
        streams = (kk * jnp.exp(cum - log_decay), r * jnp.exp(cum), kka * e_inv, key * e_inv)
        for n, st in enumerate(streams):
            st = st.astype(BF16)
            for hp in range(n_hp):
                s_ref[n, hp] = st[:, hp * LANES:(hp + 1) * LANES]
        for cidx in range(tr // CHUNK):
            e_tot = jnp.exp(tot_rows[cidx])
            etot_ref[d, cidx] = jnp.concatenate(
                [e_tot[:, hp * LANES:(hp + 1) * LANES] for hp in range(n_hp)], axis=0)
    vb = v.astype(BF16)
    for hp in range(n_hp):
        sv_ref[hp] = vb[:, hp * LANES:(hp + 1) * LANES]
    bonus_ref[...] = _group_sum(r * ksum * rk_ref[...], e_ref) * v
    gate_ref[...] = _dot3(jax.nn.sigmoid(lora[:, 4 * LORA_PAD:]), gup_ref[...])


def _chunk_block_diag(tr, kind):
    t = jnp.arange(tr)
    same = (t[:, None] // CHUNK) == (t[None, :] // CHUNK)
    if kind == "lower":
        same = same & (t[None, :] <= t[:, None])
    elif kind == "upper":
        same = same & (t[None, :] >= t[:, None])
    return same.astype(BF16)


N_STREAMS = 4


def _rwkv_prep(p, lw, e_mat, cfg):
    rows = p.shape[0]
    bw, tr = cfg["b_width"], cfg["tr"]
    cw = 3 * bw + cfg["lora_w"]
    n_hp = bw // LANES
    n_lat_tiles = cfg["batch"] * cfg["seq"] // tr
    hb = tr // SUBLANES
    last8 = rows // SUBLANES - 1
    full = lambda a: pl.BlockSpec(a.shape, lambda i: (0,) * a.ndim)
    consts = [lw["mu"], lw["w0"], lw["w_up"], lw["a0"], lw["a_up"], lw["g_up"], lw["k_k"], lw["k_a"], lw["r_k"],
              e_mat, _chunk_block_diag(tr, "lower"), _chunk_block_diag(tr, "upper")]
    stream_spec = pl.BlockSpec((N_STREAMS, n_hp, tr, LANES), lambda i: (0, 0, i, 0))
    stream_shape = jax.ShapeDtypeStruct((N_STREAMS, n_hp, rows, LANES), BF16)
    cpt = tr // CHUNK
    kernel = functools.partial(_rwkv_prep_kernel, bw=bw, seg_lat=cfg["seq"] // tr, seg_ctx=cfg["ctx"] // tr,
                               n_lat_tiles=n_lat_tiles)
    return pl.pallas_call(
        kernel,
        grid=(rows // tr,),
        in_specs=[
            pl.BlockSpec((tr, cw), lambda i: (i, 0)),
            pl.BlockSpec((SUBLANES, cw), lambda i: (jnp.maximum(i * hb - 1, 0), 0)),
            pl.BlockSpec((SUBLANES, cw), lambda i: (jnp.minimum((i + 1) * hb, last8), 0)),
        ] + [full(a) for a in consts],
        out_specs=[pl.BlockSpec((n_hp, tr, LANES), lambda i: (0, i, 0)),
                   stream_spec, stream_spec,
                   pl.BlockSpec((2, cpt, n_hp, LANES), lambda i: (0, i, 0, 0)),
                   pl.BlockSpec((tr, bw), lambda i: (i, 0)),
                   pl.BlockSpec((tr, bw), lambda i: (i, 0))],
        out_shape=[jax.ShapeDtypeStruct((n_hp, rows, LANES), BF16),
                   stream_shape, stream_shape,
                   jax.ShapeDtypeStruct((2, rows // CHUNK, n_hp, LANES), F32),
                   jax.ShapeDtypeStruct((rows, bw), F32),
                   jax.ShapeDtypeStruct((rows, bw), F32)],
        compiler_params=_cparams(("parallel",)),
        name="rwkv_prep",
    )(p, p, p, *consts)


def _block_diag(x):
    lo = lax.broadcasted_iota(jnp.int32, x.shape, 1) < HEAD_DIM
    zero = jnp.zeros_like(x)
    return jnp.concatenate([jnp.where(lo, x, zero), jnp.where(lo, zero, x)], axis=0)


def _scan_step(dirs, ones_ref, z_ref, n_hp):
    c, hd = CHUNK, HEAD_DIM
    assert c == hd and LANES == 2 * hd
    row = lax.broadcasted_iota(jnp.int32, (c, LANES), 0)
    lane = lax.broadcasted_iota(jnp.int32, (c, LANES), 1)
    col = lane & (hd - 1)
    lo = lane < hd
    eye = col == row
    inst = [(d, hp) for d in range(len(dirs)) for hp in range(n_hp)]
    pairs = range(len(inst))
    incl = [(col >= row) if dirs[d][4] else (col <= row) for d, _ in inst]
    strict = [(col > row) if dirs[d][4] else (col < row) for d, _ in inst]

    def half(x, h):
        keep = lo if h == 0 else ~lo
        return jnp.where(keep, x, jnp.zeros_like(x))

    qk, rt, bt, kt = ([dirs[d][1][k, hp] for d, hp in inst] for k in range(N_STREAMS))
    e_row = [dirs[d][2][hp:hp + 1, :] for d, hp in inst]
    bh = [(bt[i].astype(F32) * e_row[i]).astype(BF16) for i in pairs]
    kh = [(kt[i].astype(F32) * e_row[i]).astype(BF16) for i in pairs]
    vv = [dirs[d][0][hp] for d, hp in inst]
    p1 = [_dot(jnp.concatenate([qk[i], rt[i]], axis=0),
               jnp.concatenate([half(bt[i], 0), half(bt[i], 1), half(kt[i], 0), half(kt[i], 1)], axis=0),
               trans_b=True) for i in pairs]
    same_blk = (row // SUB) == (col // SUB)
    a_ab = [jnp.where(strict[i], p[:c, :LANES], 0.0) for i, p in enumerate(p1)]
    x_pow = [jnp.where(same_blk, -a, 0.0) for a in a_ab]
    u = x_pow
    x_pow = [_dot(x.astype(BF16), _block_diag(x.astype(BF16))) for x in x_pow]
    sq = 2
    while sq < SUB:
        w = [_block_diag(x.astype(BF16)) for x in x_pow]
        if 2 * sq < SUB:
            prod = [_dot(jnp.concatenate([u[i], x_pow[i]], axis=0).astype(BF16), w[i]) for i in pairs]
            u = [u[i] + x_pow[i] + prod[i][:c] for i in pairs]
            x_pow = [pr[c:] for pr in prod]
        else:
            u = [u[i] + x_pow[i] + _dot(u[i].astype(BF16), w[i]) for i in pairs]
        sq *= 2
    u_bf = [x.astype(BF16) for x in u]
    n_off = [jnp.where(same_blk, 0.0, a) for a in a_ab]
    m1_bf = [(-(n_off[i] + _dot(u_bf[i], _block_diag(n_off[i].astype(BF16))))).astype(BF16) for i in pairs]
    m2_bf = [_dot(m1_bf[i], _block_diag(m1_bf[i])).astype(BF16) for i in pairs]
    avy = [_dot(jnp.concatenate([jnp.where(strict[i], p[:c, LANES:], 0.0), jnp.where(incl[i], p[c:, LANES:], 0.0)],
                                axis=0).astype(BF16), _block_diag(vv[i])) for i, p in enumerate(p1)]

    def pair_cols(x):
        return jnp.concatenate([_block_diag(x[:, :LANES].astype(BF16)), _block_diag(x[:, LANES:].astype(BF16))],
                               axis=1)

    r = [jnp.concatenate([qk[i].astype(F32), avy[i][:c]], axis=1) for i in pairs]
    r = [r[i] + _dot(u_bf[i], pair_cols(r[i])) for i in pairs]
    r = [r[i] + _dot(m2_bf[i], pair_cols(r[i])) for i in pairs]
    tu = [(r[i] + _dot(m1_bf[i], pair_cols(r[i]))).astype(BF16) for i in pairs]
    bu = [_dot(jnp.where(incl[i], p[c:, :LANES], 0.0).astype(BF16),
               jnp.concatenate([_block_diag(tu[i][:, :LANES]), _block_diag(tu[i][:, LANES:])], axis=1))
          for i, p in enumerate(p1)]
    rh = [(rt[i].astype(F32) - bu[i][:, :LANES]).astype(BF16) for i in pairs]
    yh = [avy[i][c:] - bu[i][:, LANES:] for i in pairs]
    rp = [_dot(bh[i], tu[i], trans_a=True) for i in pairs]
    kv = [_dot(kh[i], vv[i], trans_a=True) for i in pairs]
    g = [jnp.where(lo, r[:hd, :LANES], r[hd:, :LANES]).astype(BF16) for r in rp]
    hc = [jnp.where(lo, kv[i][:hd], kv[i][hd:]) - jnp.where(lo, rp[i][:hd, LANES:], rp[i][hd:, LANES:])
          for i in pairs]
    ones2 = jnp.concatenate([ones_ref[...], ones_ref[...]], axis=0)
    e_col = []
    for i in pairs:
        e_diag = jnp.where(eye, jnp.broadcast_to(e_row[i], (c, LANES)), 0.0)
        e_col.append(_dot(jnp.concatenate(_split(e_diag), axis=1), ones2))
    z = [z_ref[i] for i in pairs]
    yz = [_dot(jnp.concatenate([rh[i], g[i]], axis=0), _block_diag(z[i].astype(BF16))) for i in pairs]
    for i, (d, hp) in enumerate(inst):
        dirs[d][3][hp] = yh[i] + yz[i][:c]
        z_ref[i] = e_col[i] * z[i] - yz[i][c:] + hc[i]


def _rwkv_scan_kernel(vf_ref, sf_ref, ef_ref, vb_ref, sb_ref, eb_ref, ones_ref, yf_ref, yb_ref, z_ref, *, n_hp):
    @pl.when(pl.program_id(1) == 0)
    def _():
        z_ref[...] = jnp.zeros_like(z_ref)

    _scan_step([(vf_ref, sf_ref, ef_ref, yf_ref, False), (vb_ref, sb_ref, eb_ref, yb_ref, True)],
               ones_ref, z_ref, n_hp)


def _rwkv_scan(s_v, s_f, s_b, e_tot, cfg):
    n_hp, rows, _ = s_v.shape
    b, seq, ctx = cfg["batch"], cfg["seq"], cfg["ctx"]
    c = CHUNK
    ncc, ncl = ctx // c, seq // c
    ctx0 = b * seq // c

    def fwd_blk(bi, j):
        return jnp.where(j < ncc, ctx0 + bi * ncc + j, bi * ncl + (j - ncc))

    def bwd_blk(bi, j):
        return jnp.where(j < ncc, ctx0 + bi * ncc + (ncc - 1 - j), bi * ncl + (ncl - 1 - (j - ncc)))

    def specs(blk, d):
        return [pl.BlockSpec((n_hp, c, LANES), lambda bi, j: (0, blk(bi, j), 0)),
                pl.BlockSpec((N_STREAMS, n_hp, c, LANES), lambda bi, j: (0, 0, blk(bi, j), 0)),
                pl.BlockSpec((None, None, n_hp, LANES), lambda bi, j: (d, blk(bi, j), 0, 0))]

    out_f = pl.BlockSpec((n_hp, c, LANES), lambda bi, j: (0, fwd_blk(bi, j), 0))
    out_b = pl.BlockSpec((n_hp, c, LANES), lambda bi, j: (0, bwd_blk(bi, j), 0))
    y_shape = jax.ShapeDtypeStruct((n_hp, rows, LANES), F32)
    return pl.pallas_call(
        functools.partial(_rwkv_scan_kernel, n_hp=n_hp),
        grid=(b, ncc + ncl),
        in_specs=specs(fwd_blk, 0) + specs(bwd_blk, 1) + [pl.BlockSpec((LANES, LANES), lambda bi, j: (0, 0))],
        out_specs=[out_f, out_b],
        out_shape=[y_shape, y_shape],
        scratch_shapes=[pltpu.VMEM((2 * n_hp, HEAD_DIM, LANES), F32)],
        compiler_params=_cparams(("parallel", "arbitrary")),
        name="rwkv_scan",
    )(s_v, s_f, e_tot, s_v, s_b, e_tot, _block_ones(LANES))


def _outproj_kernel(oa_ref, yf_ref, yb_ref, bonus_ref, gate_ref, gg_ref, gb_ref, e_ref, oc_ref, w_ref, h_ref,
                    g_ref, n2_ref, sc_ref, sh_ref, o_ref, u_ref):
    n_hp = yf_ref.shape[0]
    y = jnp.concatenate([yf_ref[hp] + yb_ref[hp] for hp in range(n_hp)], axis=1)
    mu = _group_sum(y, e_ref) * (1.0 / HEAD_DIM)
    yc = y - mu
    var = _group_sum(yc * yc, e_ref) * (1.0 / HEAD_DIM)
    yn = yc * lax.rsqrt(var + GN_EPS) * gg_ref[...] + gb_ref[...]
    o_b = ((yn + bonus_ref[...]) * gate_ref[...]).astype(BF16)
    mix = jnp.concatenate([oa_ref[...], o_b, oc_ref[...]], axis=1)
    h_new = h_ref[...] + g_ref[...] * _dot(mix, w_ref[...])
    o_ref[...] = h_new
    u_ref[...] = _modulated_norm(h_new, n2_ref[...], sc_ref[...], sh_ref[...]).astype(BF16)


def _outproj(o_a, y_f, y_b, bonus, gate, gn_g, gn_b, e_mat, o_c, w, h, mods, norm2_g, l, n_rows, cfg):
    d = h.shape[1]
    n_hp = y_f.shape[0]
    bw, tm = cfg["b_width"], cfg["tm"] // 2
    modrow = cfg["modrow"](tm)
    rows_of = lambda width: pl.BlockSpec((tm, width), lambda i: (i, 0))
    y_spec = pl.BlockSpec((n_hp, tm, LANES), lambda i: (0, i, 0))
    vec_spec = pl.BlockSpec((None, 1, bw), lambda i: (l, 0, 0))
    mod = lambda k: pl.BlockSpec((None, None, None, 1, d), lambda i: (l, modrow(i), k, 0, 0))
    return pl.pallas_call(
        _outproj_kernel,
        grid=(n_rows // tm,),
        in_specs=[rows_of(o_a.shape[1]), y_spec, y_spec, rows_of(bw), rows_of(bw), vec_spec, vec_spec,
                  pl.BlockSpec(e_mat.shape, lambda i: (0, 0)),
                  rows_of(o_c.shape[1]),
                  pl.BlockSpec((None, d, d), lambda i: (l, 0, 0)),
                  rows_of(d),
                  mod(2),
                  pl.BlockSpec((None, 1, d), lambda i: (l, 0, 0)),
                  mod(4), mod(3)],
        out_specs=[rows_of(d), rows_of(d)],
        out_shape=[jax.ShapeDtypeStruct((n_rows, d), F32), jax.ShapeDtypeStruct((n_rows, d), BF16)],
        compiler_params=_cparams(("parallel",)),
        name="out_proj",
    )(o_a, y_f, y_b, bonus, gate, gn_g, gn_b, e_mat, o_c, w, h, mods, norm2_g, mods, mods)


def _mlp_kernel(u_ref, hcol_ref, gate_ref, w1_ref, w2_ref, o_ref, mid_scr, *, n_up):
    s = pl.program_id(1)

    @pl.when(s < n_up)
    def _():
        a = jnp.maximum(_dot(u_ref[...], w1_ref[...]), 0.0)
        mid_scr[s] = (a * a).astype(BF16)

    @pl.when(s >= n_up)
    def _():
        mid = jnp.concatenate([mid_scr[k] for k in range(n_up)], axis=1)
        o_ref[...] = hcol_ref[...] + gate_ref[...] * _dot(mid, w2_ref[...])


def _mlp(h, u, mods, w1, w2, l, cfg):
    rows, d = h.shape
    dff = w1.shape[2]
    tm, tf, tn = cfg["tm"], 2048, 512
    n_up = dff // tf
    modrow = cfg["modrow"](tm)
    col = lambda s: jnp.maximum(s - n_up, 0)
    return pl.pallas_call(
        functools.partial(_mlp_kernel, n_up=n_up),
        grid=(rows // tm, n_up + d // tn),
        in_specs=[
            pl.BlockSpec((tm, d), lambda i, s: (i, 0)),
            pl.BlockSpec((tm, tn), lambda i, s: (i, col(s))),
            pl.BlockSpec((None, None, None, 1, tn), lambda i, s: (l, modrow(i), 5, 0, col(s))),
            pl.BlockSpec((None, d, tf), lambda i, s: (l, 0, jnp.minimum(s, n_up - 1))),
            pl.BlockSpec((None, dff, tn), lambda i, s: (l, 0, col(s))),
        ],
        out_specs=pl.BlockSpec((tm, tn), lambda i, s: (i, col(s))),
        out_shape=jax.ShapeDtypeStruct((rows, d), F32),
        scratch_shapes=[pltpu.VMEM((n_up, tm, tf), BF16)],
        compiler_params=_cparams(("parallel", "arbitrary")),
        name="mlp",
    )(u, h, mods, w1, w2)


def _rope_tables(seq, tm):
    rows = seq // GRID_W
    row = jnp.broadcast_to(jnp.arange(rows)[:, None], (rows, GRID_W)).reshape(-1)
    col = jnp.broadcast_to(jnp.arange(GRID_W)[None, :], (rows, GRID_W)).reshape(-1)
    n_freq = HEAD_DIM // 4
    inv = ROPE_THETA ** (-jnp.arange(n_freq, dtype=F32) / n_freq)
    ang = jnp.concatenate([row[:, None].astype(F32) * inv, col[:, None].astype(F32) * inv], -1)
    cos, sin = jnp.cos(ang), jnp.sin(ang)
    reps = LANES // HEAD_DIM
    cos_t = jnp.tile(jnp.concatenate([cos, cos], -1), (1, reps))
    sin_t = jnp.tile(jnp.concatenate([-sin, sin], -1), (1, reps))
    cos_t = jnp.concatenate([cos_t, jnp.ones((tm, LANES), F32)], 0)
    sin_t = jnp.concatenate([sin_t, jnp.zeros((tm, LANES), F32)], 0)
    return cos_t, sin_t


def _block_ones(width):
    g = jnp.arange(width) // HEAD_DIM
    return (g[:, None] == g[None, :]).astype(BF16)


def kernel(x, c, ctx, c_ctx, ada_w, ada_b, norm1_g, norm2_g, w_in, a_q_norm, a_k_norm, a_sink, c_q_norm,
           c_k_norm, shift_mu, decay_w0, decay_up, iclr_a0, iclr_up, gate_up, k_k, k_a, r_k, gn_g, gn_b,
           w_out, mlp_w1, mlp_w2):
    batch, seq, d = x.shape
    n_ctx = ctx.shape[1]
    depth = ada_w.shape[0]
    bw = k_k.shape[1]
    lora_d, lora_i, lora_g = decay_up.shape[2], iclr_up.shape[2], gate_up.shape[1]
    a_heads = a_sink.shape[1]
    a_kv = a_heads // 4
    q_w, kv_w = a_heads * HEAD_DIM, a_kv * HEAD_DIM
    attn_w = q_w + 2 * kv_w
    lora_w = 4 * LORA_PAD + lora_g
    b_in = 3 * bw + 2 * lora_d + 2 * lora_i + lora_g
    assert lora_d <= LORA_PAD and lora_i <= LORA_PAD
    assert w_in.shape[2] == 2 * attn_w + b_in

    tm = batch * n_ctx
    n_lat_tiles = batch * seq // tm
    tiles_per_batch = seq // tm
    cfg = dict(
        batch=batch, seq=seq, ctx=n_ctx, tm=tm, tr=n_ctx, tq=n_ctx, tk=min(seq, 2048),
        b_width=bw, lora_w=lora_w, attn_w=attn_w, attn_col0=3 * bw + lora_w, qk_w=q_w + kv_w,
        attn_heads=a_heads, attn_kv=a_kv,
        modrow=lambda t: (lambda i: jnp.where(i < batch * seq // t, i // (seq // t), batch)),
        rope_blk=lambda i: jnp.where(i < n_lat_tiles, i % tiles_per_batch, tiles_per_batch),
    )
    assert seq % tm == 0 and seq % GRID_W == 0 and n_ctx % CHUNK == 0 and (3 * bw) % lora_w == 0
    assert cfg["attn_col0"] % attn_w == 0 and batch + 1 <= SUBLANES

    def relayout_cols(m):
        a_part, b_part, c_part = m[..., :attn_w], m[..., attn_w:attn_w + b_in], m[..., attn_w + b_in:]
        pad = lambda z, n: jnp.pad(z, [(0, 0)] * (z.ndim - 1) + [(0, n - z.shape[-1])])
        o = 3 * bw
        pieces = [b_part[..., :o]]
        for width in (lora_d, lora_d, lora_i, lora_i):
            pieces.append(pad(b_part[..., o:o + width], LORA_PAD))
            o += width
        pieces.append(b_part[..., o:])
        return jnp.concatenate(pieces + [a_part, c_part], -1)

    w_in_r = relayout_cols(w_in).astype(BF16)
    mu_r = relayout_cols(jnp.pad(shift_mu, ((0, 0), (0, 0), (attn_w, attn_w))))[..., :3 * bw + lora_w]
    pad_rows = lambda z: jnp.pad(z, ((0, 0), (0, 0), (0, LORA_PAD - z.shape[2]), (0, 0)))
    w_up_r, a_up_r = pad_rows(decay_up), pad_rows(iclr_up)
    w_out_b = w_out.astype(BF16)
    w1_b, w2_b = mlp_w1.astype(BF16), mlp_w2.astype(BF16)

    scale = HEAD_DIM ** -0.5 * LOG2E
    tile = lambda g, n: jnp.tile(g, (1, n))

    def gains(qg, kg):
        return jnp.concatenate([tile(qg, a_heads) * scale, tile(kg, a_kv), jnp.ones((depth, kv_w), F32)], -1)

    qk_gains = jnp.stack([gains(a_q_norm, a_k_norm), gains(c_q_norm, c_k_norm)], 1)[:, :, None, :]

    cos_t, sin_t = _rope_tables(seq, tm)
    e_attn = _block_ones(attn_w)
    e_b = _block_ones(2 * LANES)

    cvec = jnp.zeros((SUBLANES, d), F32).at[:batch].set(c).at[batch].set(c_ctx)
    mods = _mods(cvec, ada_w, ada_b).reshape(depth, SUBLANES, 6, 1, d)

    h = jnp.concatenate([x.reshape(batch * seq, d), ctx.reshape(batch * n_ctx, d)], 0)
    n1g, n2g = norm1_g[:, None, :], norm2_g[:, None, :]
    gn_g3, gn_b3 = gn_g[:, None, :], gn_b[:, None, :]
    for l in range(depth):
        p = _inproj(h, n1g, mods, w_in_r, l, cfg)
        qkv = _qkprep(p, qk_gains[l], cos_t, sin_t, e_attn, cfg)
        o_a = _attention(qkv, 0, a_sink[l], cfg, window=True)
        o_c = _attention(qkv, 1, None, cfg, window=False)
        lw = dict(mu=mu_r[l], w0=decay_w0[l], w_up=w_up_r[l], a0=iclr_a0[l], a_up=a_up_r[l], g_up=gate_up[l],
                  k_k=k_k[l][None], k_a=k_a[l][None], r_k=r_k[l][None])
        s_v, s_f, s_b, e_tot, bonus, gate = _rwkv_prep(p, lw, e_b, cfg)
        y_f, y_b = _rwkv_scan(s_v, s_f, s_b, e_tot, cfg)
        n_rows = batch * seq if l == depth - 1 else h.shape[0]
        h, u2 = _outproj(o_a, y_f, y_b, bonus, gate, gn_g3, gn_b3, e_b, o_c, w_out_b, h, mods, n2g, l, n_rows, cfg)
        h = _mlp(h, u2, mods, w1_b, w2_b, l, cfg)
    return h.reshape(batch, seq, d)
```

```python
import functools

import jax
import jax.numpy as jnp
from jax import lax
from jax.experimental import pallas as pl
from jax.experimental.pallas import tpu as pltpu

F32 = jnp.float32
BF16 = jnp.bfloat16

HEAD_DIM = 64
GRID_W = 64
WINDOW = 128
ROPE_THETA = 10000.0
NORM_EPS = 1e-6
GN_EPS = 64e-5
NEG_INF = -1e30
LOG2E = 1.4426950408889634
DECAY_SCALE = 0.6065306597126334
LANES = 128
SUBLANES = 8
LORA_PAD = LANES
SUB = 16
CHUNK = 64
VMEM_LIMIT = 52 * 1024 * 1024


def _dot(a, b, trans_a=False, trans_b=False):
    dn = (((0 if trans_a else 1,), (1 if trans_b else 0,)), ((), ()))
    return lax.dot_general(a, b, dn, preferred_element_type=F32)


def _split(x):
    hi = x.astype(BF16)
    lo = (x - hi.astype(F32)).astype(BF16)
    return hi, lo


def _dot3(a, b, trans_a=False, trans_b=False):
    ah, al = _split(a)
    bh, bl = _split(b)
    kw = dict(trans_a=trans_a, trans_b=trans_b)
    return _dot(ah, bh, **kw) + (_dot(al, bh, **kw) + _dot(ah, bl, **kw))


def _group_sum(x, e_ref):
    gw = e_ref.shape[0]
    e = e_ref[...]
    outs = []
    for g in range(x.shape[1] // gw):
        hi, lo = _split(x[:, g * gw:(g + 1) * gw])
        outs.append(_dot(hi, e) + _dot(lo, e))
    return outs[0] if len(outs) == 1 else jnp.concatenate(outs, axis=1)


def _cparams(sem):
    return pltpu.CompilerParams(dimension_semantics=sem, vmem_limit_bytes=VMEM_LIMIT)


def _mods_kernel(c_ref, w_ref, b_ref, o_ref):
    c = c_ref[...]
    s = c * jax.nn.sigmoid(c)
    o_ref[0] = _dot3(s, w_ref[0]) + b_ref[0]


def _mods(cvec, ada_w, ada_b):
    depth, d, n = ada_w.shape
    tn = 512
    return pl.pallas_call(
        _mods_kernel,
        grid=(depth, n // tn),
        in_specs=[
            pl.BlockSpec((SUBLANES, d), lambda l, j: (0, 0)),
            pl.BlockSpec((1, d, tn), lambda l, j: (l, 0, j)),
            pl.BlockSpec((1, 1, tn), lambda l, j: (l, 0, j)),
        ],
        out_specs=pl.BlockSpec((1, SUBLANES, tn), lambda l, j: (l, 0, j)),
        out_shape=jax.ShapeDtypeStruct((depth, SUBLANES, n), F32),
        compiler_params=_cparams(("parallel", "parallel")),
        name="adaln_mods",
    )(cvec, ada_w, ada_b.reshape(depth, 1, n))


def _modulated_norm(x, g, sc, sh):
    ms = jnp.mean(x * x, axis=-1, keepdims=True)
    return (x * lax.rsqrt(ms + NORM_EPS) * g) * (1.0 + sc) + sh


def _inproj_kernel(h_ref, g_ref, sc_ref, sh_ref, w_ref, o_ref):
    u = _modulated_norm(h_ref[...], g_ref[...], sc_ref[...], sh_ref[...]).astype(BF16)
    o_ref[...] = _dot(u, w_ref[...])


def _inproj(h, gain, mods, w, l, cfg):
    rows, d = h.shape
    n = w.shape[2]
    tm = cfg["tm"] // 2
    modrow = cfg["modrow"](tm)
    return pl.pallas_call(
        _inproj_kernel,
        grid=(rows // tm,),
        in_specs=[
            pl.BlockSpec((tm, d), lambda i: (i, 0)),
            pl.BlockSpec((None, 1, d), lambda i: (l, 0, 0)),
            pl.BlockSpec((None, None, None, 1, d), lambda i: (l, modrow(i), 1, 0, 0)),
            pl.BlockSpec((None, None, None, 1, d), lambda i: (l, modrow(i), 0, 0, 0)),
            pl.BlockSpec((None, d, n), lambda i: (l, 0, 0), pipeline_mode=pl.Buffered(1)),
        ],
        out_specs=pl.BlockSpec((tm, n), lambda i: (i, 0)),
        out_shape=jax.ShapeDtypeStruct((rows, n), F32),
        compiler_params=_cparams(("parallel",)),
        name="in_proj",
    )(h, gain, mods, mods, w)


def _qkprep_kernel(p_ref, gain_ref, cos_ref, sin_ref, e_ref, o_ref, *, qk_w):
    x = p_ref[...]
    width = x.shape[1]
    ss = _group_sum(x * x, e_ref)
    y = x * lax.rsqrt(ss * (1.0 / HEAD_DIM) + NORM_EPS) * gain_ref[0]
    cos = cos_ref[...]
    sin = sin_ref[...]
    lane = lax.broadcasted_iota(jnp.int32, (x.shape[0], LANES), 1)
    first_half = (lane & (HEAD_DIM // 2)) == 0
    for g in range(width // LANES):
        sl = slice(g * LANES, (g + 1) * LANES)
        if g * LANES < qk_w:
            yg = y[:, sl]
            partner = jnp.where(first_half, pltpu.roll(yg, LANES - HEAD_DIM // 2, 1),
                                pltpu.roll(yg, HEAD_DIM // 2, 1))
            o_ref[0, :, sl] = (yg * cos + partner * sin).astype(BF16)
        else:
            o_ref[0, :, sl] = x[:, sl].astype(BF16)


def _qkprep(p, gains, cos_t, sin_t, e_mat, cfg):
    rows = p.shape[0]
    tm, aw = cfg["tm"], cfg["attn_w"]
    first_blk = cfg["attn_col0"] // aw
    rope_blk = cfg["rope_blk"]
    return pl.pallas_call(
        functools.partial(_qkprep_kernel, qk_w=cfg["qk_w"]),
        grid=(rows // tm, 2),
        in_specs=[
            pl.BlockSpec((tm, aw), lambda i, s: (i, first_blk + s)),
            pl.BlockSpec((1, 1, aw), lambda i, s: (s, 0, 0)),
            pl.BlockSpec((tm, LANES), lambda i, s: (rope_blk(i), 0)),
            pl.BlockSpec((tm, LANES), lambda i, s: (rope_blk(i), 0)),
            pl.BlockSpec(e_mat.shape, lambda i, s: (0, 0)),
        ],
        out_specs=pl.BlockSpec((1, tm, aw), lambda i, s: (s, i, 0)),
        out_shape=jax.ShapeDtypeStruct((2, rows, aw), BF16),
        compiler_params=_cparams(("parallel", "parallel")),
        name="qk_prep",
    )(p, gains, cos_t, sin_t, e_mat)


def _attn_kernel(sink_ref, q_ref, kc_ref, vc_ref, kl_ref, vl_ref, o_ref, *,
                 window, n_heads, group, tq, tk, seq):
    assert sink_ref is None and not window
    i = pl.program_id(1)
    is_lat = i < seq // tq
    hd = HEAD_DIM
    n_kv = n_heads // group
    den_lane = [((j + 1) % n_kv) * hd for j in range(n_kv)]

    def with_ones(vblk, j):
        lane = lax.broadcasted_iota(jnp.int32, vblk.shape, 1)
        ones_col = jnp.where(lane == den_lane[j], 1.0, 0.0).astype(BF16)
        return jnp.where((lane >= j * hd) & (lane < (j + 1) * hd), vblk, ones_col)

    def padded_q(j):
        zeros = jnp.zeros((tq, hd), BF16)
        qs = []
        for g in range(group):
            h = j * group + g
            parts = [zeros] * n_kv
            parts[j] = q_ref[0, :, h * hd:(h + 1) * hd]
            qs.append(jnp.concatenate(parts, axis=1))
        return jnp.concatenate(qs, axis=0)

    def finish(acc, j):
        o = acc[:, j * hd:(j + 1) * hd] / acc[:, den_lane[j]:den_lane[j] + 1]
        for g in range(group):
            c0 = (j * group + g) * hd
            o_ref[:, c0:c0 + hd] = o[g * tq:(g + 1) * tq].astype(BF16)

    @pl.when(is_lat)
    def _():
        for j in range(n_kv):
            q = padded_q(j)
            s_ctx = _dot(q, kc_ref[0], trans_b=True)
            s_lat = _dot(q, kl_ref[0, 0:tk, :], trans_b=True)
            m = jnp.maximum(jnp.max(s_ctx, axis=-1, keepdims=True), jnp.max(s_lat, axis=-1, keepdims=True))
            acc = (_dot(jnp.exp2(s_ctx - m).astype(BF16), with_ones(vc_ref[0], j))
                   + _dot(jnp.exp2(s_lat - m).astype(BF16), with_ones(vl_ref[0, 0:tk, :], j)))

            def body(kb, carry, q=q, j=j):
                m, acc = carry
                off = pl.multiple_of(kb * tk, tk)
                s = _dot(q, kl_ref[0, pl.ds(off, tk), :], trans_b=True)
                m_new = jnp.maximum(m, jnp.max(s, axis=-1, keepdims=True))
                p = jnp.exp2(s - m_new).astype(BF16)
                acc_new = jnp.exp2(m - m_new) * acc + _dot(p, with_ones(vl_ref[0, pl.ds(off, tk), :], j))
                return m_new, acc_new

            m, acc = lax.fori_loop(1, seq // tk, body, (m, acc))
            finish(acc, j)

    @pl.when(jnp.logical_not(is_lat))
    def _():
        for j in range(n_kv):
            s_ctx = _dot(padded_q(j), kc_ref[0], trans_b=True)
            p = jnp.exp2(s_ctx - jnp.max(s_ctx, axis=-1, keepdims=True)).astype(BF16)
            finish(_dot(p, with_ones(vc_ref[0], j)), j)


def _window_attn_kernel(sink_ref, q_ref, kc_ref, vc_ref, kl_ref, vl_ref, o_ref, *,
                        window, n_heads, group, tq, tk, seq):
    del window, tk
    i = pl.program_id(1)
    is_lat = i < seq // tq
    hd = HEAD_DIM
    n_kv = n_heads // group
    rows = group * tq
    span = tq + 2 * WINDOW
    den_lane = [((j + 1) % n_kv) * hd for j in range(n_kv)]
    start = pl.multiple_of(jnp.clip(i * tq - WINDOW, 0, seq - span), WINDOW)
    delta = (lax.broadcasted_iota(jnp.int32, (tq, span), 0) - lax.broadcasted_iota(jnp.int32, (tq, span), 1)
             + (i * tq - start))
    bias = jnp.where(jnp.abs(delta) <= jnp.where(is_lat, WINDOW, -1), 0.0, NEG_INF)
    k_ctx, v_ctx = kc_ref[0], vc_ref[0]
    k_win, v_win = kl_ref[0, pl.ds(start, span), :], vl_ref[0, pl.ds(start, span), :]

    def with_ones(vblk, j):
        lane = lax.broadcasted_iota(jnp.int32, vblk.shape, 1)
        ones_col = jnp.where(lane == den_lane[j], 1.0, 0.0).astype(BF16)
        return jnp.where((lane >= j * hd) & (lane < (j + 1) * hd), vblk, ones_col)

    zeros = jnp.zeros((tq, hd), BF16)
    for j in range(n_kv):
        qs = []
        for g in range(group):
            h = j * group + g
            parts = [zeros] * n_kv
            parts[j] = q_ref[0, :, h * hd:(h + 1) * hd]
            qs.append(jnp.concatenate(parts, axis=1))
        q = jnp.concatenate(qs, axis=0)
        m0 = jnp.concatenate(
            [jnp.full((tq, 1), sink_ref[j * group + g] * LOG2E, F32) for g in range(group)], axis=0)
        s_ctx = _dot(q, k_ctx, trans_b=True)
        s_win = (_dot(q, k_win, trans_b=True).reshape(group, tq, span) + bias[None]).reshape(rows, span)
        m = jnp.maximum(m0, jnp.maximum(jnp.max(s_ctx, axis=-1, keepdims=True),
                                        jnp.max(s_win, axis=-1, keepdims=True)))
        acc = (_dot(jnp.exp2(s_ctx - m).astype(BF16), with_ones(v_ctx, j))
               + _dot(jnp.exp2(s_win - m).astype(BF16), with_ones(v_win, j)))
        den = acc[:, den_lane[j]:den_lane[j] + 1] + jnp.exp2(m0 - m)
        o = acc[:, j * hd:(j + 1) * hd] / den
        for g in range(group):
            c0 = (j * group + g) * hd
            o_ref[:, c0:c0 + hd] = o[g * tq:(g + 1) * tq].astype(BF16)


def _attention(qkv, sec, sink, cfg, window):
    rows = qkv.shape[1]
    b, seq, ctx, tq = cfg["batch"], cfg["seq"], cfg["ctx"], cfg["tq"]
    n_heads, n_kv = cfg["attn_heads"], cfg["attn_kv"]
    q_w = n_heads * HEAD_DIM
    kv_w = n_kv * HEAD_DIM
    assert kv_w == LANES and q_w % kv_w == 0 and ctx == tq
    n_lat_tiles = seq // tq
    k_blk = q_w // kv_w
    ctx_blk0 = b * seq // ctx

    def q_map(bi, i, *_):
        return (sec, jnp.where(i < n_lat_tiles, bi * n_lat_tiles + i, ctx_blk0 + bi), 0)

    def o_map(bi, i, *_):
        return (jnp.where(i < n_lat_tiles, bi * n_lat_tiles + i, ctx_blk0 + bi), 0)

    tk = cfg["tk"]
    assert seq % tk == 0 and tq % WINDOW == 0 and tq + 2 * WINDOW <= seq
    assert (sink is not None) == window
    kernel = functools.partial(_window_attn_kernel if window else _attn_kernel, window=window, n_heads=n_heads,
                               group=n_heads // n_kv, tq=tq, tk=tk, seq=seq)
    in_specs = [
        pl.BlockSpec((1, tq, q_w), q_map),
        pl.BlockSpec((1, ctx, kv_w), lambda bi, i, *_: (sec, ctx_blk0 + bi, k_blk)),
        pl.BlockSpec((1, ctx, kv_w), lambda bi, i, *_: (sec, ctx_blk0 + bi, k_blk + 1)),
        pl.BlockSpec((1, seq, kv_w), lambda bi, i, *_: (sec, bi, k_blk)),
        pl.BlockSpec((1, seq, kv_w), lambda bi, i, *_: (sec, bi, k_blk + 1)),
    ]
    args = [qkv, qkv, qkv, qkv, qkv]
    if sink is not None:
        in_specs = [pl.BlockSpec(memory_space=pltpu.SMEM)] + in_specs
        args = [sink] + args
    else:
        kernel = functools.partial(kernel, None)
    return pl.pallas_call(
        kernel,
        grid=(b, n_lat_tiles + 1),
        in_specs=in_specs,
        out_specs=pl.BlockSpec((tq, q_w), o_map),
        out_shape=jax.ShapeDtypeStruct((rows, q_w), BF16),
        compiler_params=_cparams(("parallel", "arbitrary")),
        name="window_attn" if window else "global_attn",
    )(*args)


def _tri3(tri, x):
    h1 = x.astype(BF16)
    r1 = x - h1.astype(F32)
    h2 = r1.astype(BF16)
    h3 = (r1 - h2.astype(F32)).astype(BF16)
    return _dot(tri, h1) + (_dot(tri, h2) + _dot(tri, h3))


def _rwkv_prep_kernel(cur_ref, prev_ref, next_ref, mu_ref, w0_ref, wup_ref, a0_ref, aup_ref, gup_ref,
                      kk_ref, ka_ref, rk_ref, e_ref, tril_ref, triu_ref,
                      sv_ref, sf_ref, sb_ref, etot_ref, bonus_ref, gate_ref, *,
                      bw, seg_lat, seg_ctx, n_lat_tiles):
    i = pl.program_id(0)
    p = cur_ref[...]
    tr = p.shape[0]
    lat = i < n_lat_tiles
    seg = jnp.where(lat, seg_lat, seg_ctx)
    pos = jnp.where(lat, i, i - n_lat_tiles) % seg
    row = lax.broadcasted_iota(jnp.int32, (tr, 1), 0)
    prev_row = jnp.where(pos != 0, prev_ref[SUBLANES - 1:SUBLANES, :], 0.0)
    next_row = jnp.where(pos != seg - 1, next_ref[0:1, :], 0.0)
    prv = jnp.where(row == 0, prev_row, pltpu.roll(p, 1, 0))
    nxt = jnp.where(row == tr - 1, next_row, pltpu.roll(p, tr - 1, 0))
    xs = p + mu_ref[0:1, :] * (prv - p) + mu_ref[1:2, :] * (nxt - p)

    r = xs[:, 0:bw]
    k = xs[:, bw:2 * bw]
    v = xs[:, 2 * bw:3 * bw]
    lora = xs[:, 3 * bw:]
    n_hp = bw // LANES

    kk = k * kk_ref[...]
    kk = kk * lax.rsqrt(jnp.maximum(_group_sum(kk * kk, e_ref), 1e-24))
    ksum = jnp.zeros_like(k)
    for d, (s_ref, tri_ref) in enumerate(((sf_ref, tril_ref), (sb_ref, triu_ref))):
        wd = lora[:, d * LORA_PAD:(d + 1) * LORA_PAD]
        ad = lora[:, (2 + d) * LORA_PAD:(3 + d) * LORA_PAD]
        log_decay = -DECAY_SCALE * jax.nn.sigmoid(w0_ref[d:d + 1, :] + _dot3(jnp.tanh(wd), wup_ref[d]))
        a = jax.nn.sigmoid(a0_ref[d:d + 1, :] + _dot3(ad, aup_ref[d]))
        key = k * (1.0 + (a - 1.0) * ka_ref[...])
        ksum = ksum + key
        kka = kk * a
        cum = _tri3(tri_ref[...], log_decay)
        last = CHUNK - 1 if d == 0 else 0
        tot_rows = [cum[c * CHUNK + last:c * CHUNK + last + 1] for c in range(tr // CHUNK)]
        e_inv = jnp.exp(-cum)
        streams = (kk * jnp.exp(cum - log_decay), r * jnp.exp(cum), kka * e_inv, key * e_inv)
        for n, st in enumerate(streams):
            st = st.astype(BF16)
            for hp in range(n_hp):
                s_ref[n, hp] = st[:, hp * LANES:(hp + 1) * LANES]
        for cidx in range(tr // CHUNK):
            e_tot = jnp.exp(tot_rows[cidx])
            etot_ref[d, cidx] = jnp.concatenate(
                [e_tot[:, hp * LANES:(hp + 1) * LANES] for hp in range(n_hp)], axis=0)
    vb = v.astype(BF16)
    for hp in range(n_hp):
        sv_ref[hp] = vb[:, hp * LANES:(hp + 1) * LANES]
    bonus_ref[...] = _group_sum(r * ksum * rk_ref[...], e_ref) * v
    gate_ref[...] = _dot3(jax.nn.sigmoid(lora[:, 4 * LORA_PAD:]), gup_ref[...])


def _chunk_block_diag(tr, kind):
    t = jnp.arange(tr)
    same = (t[:, None] // CHUNK) == (t[None, :] // CHUNK)
    if kind == "lower":
        same = same & (t[None, :] <= t[:, None])
    elif kind == "upper":
        same = same & (t[None, :] >= t[:, None])
    return same.astype(BF16)


N_STREAMS = 4


def _rwkv_prep(p, lw, e_mat, cfg):
    rows = p.shape[0]
    bw, tr = cfg["b_width"], cfg["tr"]
    cw = 3 * bw + cfg["lora_w"]
    n_hp = bw // LANES
    n_lat_tiles = cfg["batch"] * cfg["seq"] // tr
    hb = tr // SUBLANES
    last8 = rows // SUBLANES - 1
    full = lambda a: pl.BlockSpec(a.shape, lambda i: (0,) * a.ndim)
    consts = [lw["mu"], lw["w0"], lw["w_up"], lw["a0"], lw["a_up"], lw["g_up"], lw["k_k"], lw["k_a"], lw["r_k"],
              e_mat, _chunk_block_diag(tr, "lower"), _chunk_block_diag(tr, "upper")]
    stream_spec = pl.BlockSpec((N_STREAMS, n_hp, tr, LANES), lambda i: (0, 0, i, 0))
    stream_shape = jax.ShapeDtypeStruct((N_STREAMS, n_hp, rows, LANES), BF16)
    cpt = tr // CHUNK
    kernel = functools.partial(_rwkv_prep_kernel, bw=bw, seg_lat=cfg["seq"] // tr, seg_ctx=cfg["ctx"] // tr,
                               n_lat_tiles=n_lat_tiles)
    return pl.pallas_call(
        kernel,
        grid=(rows // tr,),
        in_specs=[
            pl.BlockSpec((tr, cw), lambda i: (i, 0)),
            pl.BlockSpec((SUBLANES, cw), lambda i: (jnp.maximum(i * hb - 1, 0), 0)),
            pl.BlockSpec((SUBLANES, cw), lambda i: (jnp.minimum((i + 1) * hb, last8), 0)),
        ] + [full(a) for a in consts],
        out_specs=[pl.BlockSpec((n_hp, tr, LANES), lambda i: (0, i, 0)),
                   stream_spec, stream_spec,
                   pl.BlockSpec((2, cpt, n_hp, LANES), lambda i: (0, i, 0, 0)),
                   pl.BlockSpec((tr, bw), lambda i: (i, 0)),
                   pl.BlockSpec((tr, bw), lambda i: (i, 0))],
        out_shape=[jax.ShapeDtypeStruct((n_hp, rows, LANES), BF16),
                   stream_shape, stream_shape,
                   jax.ShapeDtypeStruct((2, rows // CHUNK, n_hp, LANES), F32),
                   jax.ShapeDtypeStruct((rows, bw), F32),
                   jax.ShapeDtypeStruct((rows, bw), F32)],
        compiler_params=_cparams(("parallel",)),
        name="rwkv_prep",
    )(p, p, p, *consts)


def _block_diag(x):
    lo = lax.broadcasted_iota(jnp.int32, x.shape, 1) < HEAD_DIM
    zero = jnp.zeros_like(x)
    return jnp.concatenate([jnp.where(lo, x, zero), jnp.where(lo, zero, x)], axis=0)


def _scan_step(dirs, ones_ref, z_ref, n_hp):
    c, hd = CHUNK, HEAD_DIM
    assert c == hd and LANES == 2 * hd
    row = lax.broadcasted_iota(jnp.int32, (c, LANES), 0)
    lane = lax.broadcasted_iota(jnp.int32, (c, LANES), 1)
    col = lane & (hd - 1)
    lo = lane < hd
    eye = col == row
    inst = [(d, hp) for d in range(len(dirs)) for hp in range(n_hp)]
    pairs = range(len(inst))
    incl = [(col >= row) if dirs[d][4] else (col <= row) for d, _ in inst]
    strict = [(col > row) if dirs[d][4] else (col < row) for d, _ in inst]

    def half(x, h):
        keep = lo if h == 0 else ~lo
        return jnp.where(keep, x, jnp.zeros_like(x))

    qk, rt, bt, kt = ([dirs[d][1][k, hp] for d, hp in inst] for k in range(N_STREAMS))
    e_row = [dirs[d][2][hp:hp + 1, :] for d, hp in inst]
    bh = [(bt[i].astype(F32) * e_row[i]).astype(BF16) for i in pairs]
    kh = [(kt[i].astype(F32) * e_row[i]).astype(BF16) for i in pairs]
    vv = [dirs[d][0][hp] for d, hp in inst]
    p1 = [_dot(jnp.concatenate([qk[i], rt[i]], axis=0),
               jnp.concatenate([half(bt[i], 0), half(bt[i], 1), half(kt[i], 0), half(kt[i], 1)], axis=0),
               trans_b=True) for i in pairs]
    same_blk = (row // SUB) == (col // SUB)
    a_ab = [jnp.where(strict[i], p[:c, :LANES], 0.0) for i, p in enumerate(p1)]
    x_pow = [jnp.where(same_blk, -a, 0.0) for a in a_ab]
    u = x_pow
    x_pow = [_dot(x.astype(BF16), _block_diag(x.astype(BF16))) for x in x_pow]
    sq = 2
    while sq < SUB:
        w = [_block_diag(x.astype(BF16)) for x in x_pow]
        if 2 * sq < SUB:
            prod = [_dot(jnp.concatenate([u[i], x_pow[i]], axis=0).astype(BF16), w[i]) for i in pairs]
            u = [u[i] + x_pow[i] + prod[i][:c] for i in pairs]
            x_pow = [pr[c:] for pr in prod]
        else:
            u = [u[i] + x_pow[i] + _dot(u[i].astype(BF16), w[i]) for i in pairs]
        sq *= 2
    u_bf = [x.astype(BF16) for x in u]
    n_off = [jnp.where(same_blk, 0.0, a) for a in a_ab]
    m1_bf = [(-(n_off[i] + _dot(u_bf[i], _block_diag(n_off[i].astype(BF16))))).astype(BF16) for i in pairs]
    m2_bf = [_dot(m1_bf[i], _block_diag(m1_bf[i])).astype(BF16) for i in pairs]
    avy = [_dot(jnp.concatenate([jnp.where(strict[i], p[:c, LANES:], 0.0), jnp.where(incl[i], p[c:, LANES:], 0.0)],
                                axis=0).astype(BF16), _block_diag(vv[i])) for i, p in enumerate(p1)]

    def pair_cols(x):
        return jnp.concatenate([_block_diag(x[:, :LANES].astype(BF16)), _block_diag(x[:, LANES:].astype(BF16))],
                               axis=1)

    r = [jnp.concatenate([qk[i].astype(F32), avy[i][:c]], axis=1) for i in pairs]
    r = [r[i] + _dot(u_bf[i], pair_cols(r[i])) for i in pairs]
    r = [r[i] + _dot(m2_bf[i], pair_cols(r[i])) for i in pairs]
    tu = [(r[i] + _dot(m1_bf[i], pair_cols(r[i]))).astype(BF16) for i in pairs]
    bu = [_dot(jnp.where(incl[i], p[c:, :LANES], 0.0).astype(BF16),
               jnp.concatenate([_block_diag(tu[i][:, :LANES]), _block_diag(tu[i][:, LANES:])], axis=1))
          for i, p in enumerate(p1)]
    rh = [(rt[i].astype(F32) - bu[i][:, :LANES]).astype(BF16) for i in pairs]
    yh = [avy[i][c:] - bu[i][:, LANES:] for i in pairs]
    rp = [_dot(bh[i], tu[i], trans_a=True) for i in pairs]
    kv = [_dot(kh[i], vv[i], trans_a=True) for i in pairs]
    g = [jnp.where(lo, r[:hd, :LANES], r[hd:, :LANES]).astype(BF16) for r in rp]
    hc = [jnp.where(lo, kv[i][:hd], kv[i][hd:]) - jnp.where(lo, rp[i][:hd, LANES:], rp[i][hd:, LANES:])
          for i in pairs]
    ones2 = jnp.concatenate([ones_ref[...], ones_ref[...]], axis=0)
    e_col = []
    for i in pairs:
        e_diag = jnp.where(eye, jnp.broadcast_to(e_row[i], (c, LANES)), 0.0)
        e_col.append(_dot(jnp.concatenate(_split(e_diag), axis=1), ones2))
    z = [z_ref[i] for i in pairs]
    yz = [_dot(jnp.concatenate([rh[i], g[i]], axis=0), _block_diag(z[i].astype(BF16))) for i in pairs]
    for i, (d, hp) in enumerate(inst):
        dirs[d][3][hp] = yh[i] + yz[i][:c]
        z_ref[i] = e_col[i] * z[i] - yz[i][c:] + hc[i]


def _rwkv_scan_kernel(vf_ref, sf_ref, ef_ref, vb_ref, sb_ref, eb_ref, ones_ref, yf_ref, yb_ref, z_ref, *, n_hp):
    @pl.when(pl.program_id(1) == 0)
    def _():
        z_ref[...] = jnp.zeros_like(z_ref)

    _scan_step([(vf_ref, sf_ref, ef_ref, yf_ref, False), (vb_ref, sb_ref, eb_ref, yb_ref, True)],
               ones_ref, z_ref, n_hp)


def _rwkv_scan(s_v, s_f, s_b, e_tot, cfg):
    n_hp, rows, _ = s_v.shape
    b, seq, ctx = cfg["batch"], cfg["seq"], cfg["ctx"]
    c = CHUNK
    ncc, ncl = ctx // c, seq // c
    ctx0 = b * seq // c

    def fwd_blk(bi, j):
        return jnp.where(j < ncc, ctx0 + bi * ncc + j, bi * ncl + (j - ncc))

    def bwd_blk(bi, j):
        return jnp.where(j < ncc, ctx0 + bi * ncc + (ncc - 1 - j), bi * ncl + (ncl - 1 - (j - ncc)))

    def specs(blk, d):
        return [pl.BlockSpec((n_hp, c, LANES), lambda bi, j: (0, blk(bi, j), 0)),
                pl.BlockSpec((N_STREAMS, n_hp, c, LANES), lambda bi, j: (0, 0, blk(bi, j), 0)),
                pl.BlockSpec((None, None, n_hp, LANES), lambda bi, j: (d, blk(bi, j), 0, 0))]

    out_f = pl.BlockSpec((n_hp, c, LANES), lambda bi, j: (0, fwd_blk(bi, j), 0))
    out_b = pl.BlockSpec((n_hp, c, LANES), lambda bi, j: (0, bwd_blk(bi, j), 0))
    y_shape = jax.ShapeDtypeStruct((n_hp, rows, LANES), F32)
    return pl.pallas_call(
        functools.partial(_rwkv_scan_kernel, n_hp=n_hp),
        grid=(b, ncc + ncl),
        in_specs=specs(fwd_blk, 0) + specs(bwd_blk, 1) + [pl.BlockSpec((LANES, LANES), lambda bi, j: (0, 0))],
        out_specs=[out_f, out_b],
        out_shape=[y_shape, y_shape],
        scratch_shapes=[pltpu.VMEM((2 * n_hp, HEAD_DIM, LANES), F32)],
        compiler_params=_cparams(("parallel", "arbitrary")),
        name="rwkv_scan",
    )(s_v, s_f, e_tot, s_v, s_b, e_tot, _block_ones(LANES))


def _outproj_kernel(oa_ref, yf_ref, yb_ref, bonus_ref, gate_ref, gg_ref, gb_ref, e_ref, oc_ref, w_ref, h_ref,
                    g_ref, n2_ref, sc_ref, sh_ref, o_ref, u_ref):
    n_hp = yf_ref.shape[0]
    y = jnp.concatenate([yf_ref[hp] + yb_ref[hp] for hp in range(n_hp)], axis=1)
    mu = _group_sum(y, e_ref) * (1.0 / HEAD_DIM)
    yc = y - mu
    var = _group_sum(yc * yc, e_ref) * (1.0 / HEAD_DIM)
    yn = yc * lax.rsqrt(var + GN_EPS) * gg_ref[...] + gb_ref[...]
    o_b = ((yn + bonus_ref[...]) * gate_ref[...]).astype(BF16)
    mix = jnp.concatenate([oa_ref[...], o_b, oc_ref[...]], axis=1)
    h_new = h_ref[...] + g_ref[...] * _dot(mix, w_ref[...])
    o_ref[...] = h_new
    u_ref[...] = _modulated_norm(h_new, n2_ref[...], sc_ref[...], sh_ref[...]).astype(BF16)


def _outproj(o_a, y_f, y_b, bonus, gate, gn_g, gn_b, e_mat, o_c, w, h, mods, norm2_g, l, n_rows, cfg):
    d = h.shape[1]
    n_hp = y_f.shape[0]
    bw, tm = cfg["b_width"], cfg["tm"] // 2
    modrow = cfg["modrow"](tm)
    rows_of = lambda width: pl.BlockSpec((tm, width), lambda i: (i, 0))
    y_spec = pl.BlockSpec((n_hp, tm, LANES), lambda i: (0, i, 0))
    vec_spec = pl.BlockSpec((None, 1, bw), lambda i: (l, 0, 0))
    mod = lambda k: pl.BlockSpec((None, None, None, 1, d), lambda i: (l, modrow(i), k, 0, 0))
    return pl.pallas_call(
        _outproj_kernel,
        grid=(n_rows // tm,),
        in_specs=[rows_of(o_a.shape[1]), y_spec, y_spec, rows_of(bw), rows_of(bw), vec_spec, vec_spec,
                  pl.BlockSpec(e_mat.shape, lambda i: (0, 0)),
                  rows_of(o_c.shape[1]),
                  pl.BlockSpec((None, d, d), lambda i: (l, 0, 0)),
                  rows_of(d),
                  mod(2),
                  pl.BlockSpec((None, 1, d), lambda i: (l, 0, 0)),
                  mod(4), mod(3)],
        out_specs=[rows_of(d), rows_of(d)],
        out_shape=[jax.ShapeDtypeStruct((n_rows, d), F32), jax.ShapeDtypeStruct((n_rows, d), BF16)],
        compiler_params=_cparams(("parallel",)),
        name="out_proj",
    )(o_a, y_f, y_b, bonus, gate, gn_g, gn_b, e_mat, o_c, w, h, mods, norm2_g, mods, mods)


def _mlp_kernel(u_ref, hcol_ref, gate_ref, w1_ref, w2_ref, o_ref, mid_scr, *, n_up):
    s = pl.program_id(1)

    @pl.when(s < n_up)
    def _():
        a = jnp.maximum(_dot(u_ref[...], w1_ref[...]), 0.0)
        mid_scr[s] = (a * a).astype(BF16)

    @pl.when(s >= n_up)
    def _():
        mid = jnp.concatenate([mid_scr[k] for k in range(n_up)], axis=1)
        o_ref[...] = hcol_ref[...] + gate_ref[...] * _dot(mid, w2_ref[...])


def _mlp(h, u, mods, w1, w2, l, cfg):
    rows, d = h.shape
    dff = w1.shape[2]
    tm, tf, tn = cfg["tm"], 2048, 512
    n_up = dff // tf
    modrow = cfg["modrow"](tm)
    col = lambda s: jnp.maximum(s - n_up, 0)
    return pl.pallas_call(
        functools.partial(_mlp_kernel, n_up=n_up),
        grid=(rows // tm, n_up + d // tn),
        in_specs=[
            pl.BlockSpec((tm, d), lambda i, s: (i, 0)),
            pl.BlockSpec((tm, tn), lambda i, s: (i, col(s))),
            pl.BlockSpec((None, None, None, 1, tn), lambda i, s: (l, modrow(i), 5, 0, col(s))),
            pl.BlockSpec((None, d, tf), lambda i, s: (l, 0, jnp.minimum(s, n_up - 1))),
            pl.BlockSpec((None, dff, tn), lambda i, s: (l, 0, col(s))),
        ],
        out_specs=pl.BlockSpec((tm, tn), lambda i, s: (i, col(s))),
        out_shape=jax.ShapeDtypeStruct((rows, d), F32),
        scratch_shapes=[pltpu.VMEM((n_up, tm, tf), BF16)],
        compiler_params=_cparams(("parallel", "arbitrary")),
        name="mlp",
    )(u, h, mods, w1, w2)


def _rope_tables(seq, tm):
    rows = seq // GRID_W
    row = jnp.broadcast_to(jnp.arange(rows)[:, None], (rows, GRID_W)).reshape(-1)
    col = jnp.broadcast_to(jnp.arange(GRID_W)[None, :], (rows, GRID_W)).reshape(-1)
    n_freq = HEAD_DIM // 4
    inv = ROPE_THETA ** (-jnp.arange(n_freq, dtype=F32) / n_freq)
    ang = jnp.concatenate([row[:, None].astype(F32) * inv, col[:, None].astype(F32) * inv], -1)
    cos, sin = jnp.cos(ang), jnp.sin(ang)
    reps = LANES // HEAD_DIM
    cos_t = jnp.tile(jnp.concatenate([cos, cos], -1), (1, reps))
    sin_t = jnp.tile(jnp.concatenate([-sin, sin], -1), (1, reps))
    cos_t = jnp.concatenate([cos_t, jnp.ones((tm, LANES), F32)], 0)
    sin_t = jnp.concatenate([sin_t, jnp.zeros((tm, LANES), F32)], 0)
    return cos_t, sin_t


def _block_ones(width):
    g = jnp.arange(width) // HEAD_DIM
    return (g[:, None] == g[None, :]).astype(BF16)


def kernel(x, c, ctx, c_ctx, ada_w, ada_b, norm1_g, norm2_g, w_in, a_q_norm, a_k_norm, a_sink, c_q_norm,
           c_k_norm, shift_mu, decay_w0, decay_up, iclr_a0, iclr_up, gate_up, k_k, k_a, r_k, gn_g, gn_b,
           w_out, mlp_w1, mlp_w2):
    batch, seq, d = x.shape
    n_ctx = ctx.shape[1]
    depth = ada_w.shape[0]
    bw = k_k.shape[1]
    lora_d, lora_i, lora_g = decay_up.shape[2], iclr_up.shape[2], gate_up.shape[1]
    a_heads = a_sink.shape[1]
    a_kv = a_heads // 4
    q_w, kv_w = a_heads * HEAD_DIM, a_kv * HEAD_DIM
    attn_w = q_w + 2 * kv_w
    lora_w = 4 * LORA_PAD + lora_g
    b_in = 3 * bw + 2 * lora_d + 2 * lora_i + lora_g
    assert lora_d <= LORA_PAD and lora_i <= LORA_PAD
    assert w_in.shape[2] == 2 * attn_w + b_in

    tm = batch * n_ctx
    n_lat_tiles = batch * seq // tm
    tiles_per_batch = seq // tm
    cfg = dict(
        batch=batch, seq=seq, ctx=n_ctx, tm=tm, tr=n_ctx, tq=n_ctx, tk=min(seq, 2048),
        b_width=bw, lora_w=lora_w, attn_w=attn_w, attn_col0=3 * bw + lora_w, qk_w=q_w + kv_w,
        attn_heads=a_heads, attn_kv=a_kv,
        modrow=lambda t: (lambda i: jnp.where(i < batch * seq // t, i // (seq // t), batch)),
        rope_blk=lambda i: jnp.where(i < n_lat_tiles, i % tiles_per_batch, tiles_per_batch),
    )
    assert seq % tm == 0 and seq % GRID_W == 0 and n_ctx % CHUNK == 0 and (3 * bw) % lora_w == 0
    assert cfg["attn_col0"] % attn_w == 0 and batch + 1 <= SUBLANES

    def relayout_cols(m):
        a_part, b_part, c_part = m[..., :attn_w], m[..., attn_w:attn_w + b_in], m[..., attn_w + b_in:]
        pad = lambda z, n: jnp.pad(z, [(0, 0)] * (z.ndim - 1) + [(0, n - z.shape[-1])])
        o = 3 * bw
        pieces = [b_part[..., :o]]
        for width in (lora_d, lora_d, lora_i, lora_i):
            pieces.append(pad(b_part[..., o:o + width], LORA_PAD))
            o += width
        pieces.append(b_part[..., o:])
        return jnp.concatenate(pieces + [a_part, c_part], -1)

    w_in_r = relayout_cols(w_in).astype(BF16)
    mu_r = relayout_cols(jnp.pad(shift_mu, ((0, 0), (0, 0), (attn_w, attn_w))))[..., :3 * bw + lora_w]
    pad_rows = lambda z: jnp.pad(z, ((0, 0), (0, 0), (0, LORA_PAD - z.shape[2]), (0, 0)))
    w_up_r, a_up_r = pad_rows(decay_up), pad_rows(iclr_up)
    w_out_b = w_out.astype(BF16)
    w1_b, w2_b = mlp_w1.astype(BF16), mlp_w2.astype(BF16)

    scale = HEAD_DIM ** -0.5 * LOG2E
    tile = lambda g, n: jnp.tile(g, (1, n))

    def gains(qg, kg):
        return jnp.concatenate([tile(qg, a_heads) * scale, tile(kg, a_kv), jnp.ones((depth, kv_w), F32)], -1)

    qk_gains = jnp.stack([gains(a_q_norm, a_k_norm), gains(c_q_norm, c_k_norm)], 1)[:, :, None, :]

    cos_t, sin_t = _rope_tables(seq, tm)
    e_attn = _block_ones(attn_w)
    e_b = _block_ones(2 * LANES)

    cvec = jnp.zeros((SUBLANES, d), F32).at[:batch].set(c).at[batch].set(c_ctx)
    mods = _mods(cvec, ada_w, ada_b).reshape(depth, SUBLANES, 6, 1, d)

    h = jnp.concatenate([x.reshape(batch * seq, d), ctx.reshape(batch * n_ctx, d)], 0)
    n1g, n2g = norm1_g[:, None, :], norm2_g[:, None, :]
    gn_g3, gn_b3 = gn_g[:, None, :], gn_b[:, None, :]
    for l in range(depth):
        p = _inproj(h, n1g, mods, w_in_r, l, cfg)
        qkv = _qkprep(p, qk_gains[l], cos_t, sin_t, e_attn, cfg)
        o_a = _attention(qkv, 0, a_sink[l], cfg, window=True)
        o_c = _attention(qkv, 1, None, cfg, window=False)
        lw = dict(mu=mu_r[l], w0=decay_w0[l], w_up=w_up_r[l], a0=iclr_a0[l], a_up=a_up_r[l], g_up=gate_up[l],
                  k_k=k_k[l][None], k_a=k_a[l][None], r_k=r_k[l][None])
        s_v, s_f, s_b, e_tot, bonus, gate = _rwkv_prep(p, lw, e_b, cfg)
        y_f, y_b = _rwkv_scan(s_v, s_f, s_b, e_tot, cfg)
        n_rows = batch * seq if l == depth - 1 else h.shape[0]
        h, u2 = _outproj(o_a, y_f, y_b, bonus, gate, gn_g3, gn_b3, e_b, o_c, w_out_b, h, mods, n2g, l, n_rows, cfg)
        h = _mlp(h, u2, mods, w1_b, w2_b, l, cfg)
    return h.reshape(batch, seq, d)
```

```python
import functools

import jax
import jax.numpy as jnp
from jax import lax
from jax.experimental import pallas as pl
from jax.experimental.pallas import tpu as pltpu

F32 = jnp.float32
BF16 = jnp.bfloat16

HEAD_DIM = 64
GRID_W = 64
WINDOW = 128
ROPE_THETA = 10000.0
NORM_EPS = 1e-6
GN_EPS = 64e-5
NEG_INF = -1e30
LOG2E = 1.4426950408889634
DECAY_SCALE = 0.6065306597126334
LANES = 128
SUBLANES = 8
LORA_PAD = LANES
SUB = 16
CHUNK = 64
VMEM_LIMIT = 52 * 1024 * 1024


def _dot(a, b, trans_a=False, trans_b=False):
    dn = (((0 if trans_a else 1,), (1 if trans_b else 0,)), ((), ()))
    return lax.dot_general(a, b, dn, preferred_element_type=F32)


def _split(x):
    hi = x.astype(BF16)
    lo = (x - hi.astype(F32)).astype(BF16)
    return hi, lo


def _dot3(a, b, trans_a=False, trans_b=False):
    ah, al = _split(a)
    bh, bl = _split(b)
    kw = dict(trans_a=trans_a, trans_b=trans_b)
    return _dot(ah, bh, **kw) + (_dot(al, bh, **kw) + _dot(ah, bl, **kw))


def _group_sum(x, e_ref):
    gw = e_ref.shape[0]
    e = e_ref[...]
    outs = []
    for g in range(x.shape[1] // gw):
        hi, lo = _split(x[:, g * gw:(g + 1) * gw])
        outs.append(_dot(hi, e) + _dot(lo, e))
    return outs[0] if len(outs) == 1 else jnp.concatenate(outs, axis=1)


def _cparams(sem):
    return pltpu.CompilerParams(dimension_semantics=sem, vmem_limit_bytes=VMEM_LIMIT)


def _mods_kernel(c_ref, w_ref, b_ref, o_ref):
    c = c_ref[...]
    s = c * jax.nn.sigmoid(c)
    o_ref[0] = _dot3(s, w_ref[0]) + b_ref[0]


def _mods(cvec, ada_w, ada_b):
    depth, d, n = ada_w.shape
    tn = 512
    return pl.pallas_call(
        _mods_kernel,
        grid=(depth, n // tn),
        in_specs=[
            pl.BlockSpec((SUBLANES, d), lambda l, j: (0, 0)),
            pl.BlockSpec((1, d, tn), lambda l, j: (l, 0, j)),
            pl.BlockSpec((1, 1, tn), lambda l, j: (l, 0, j)),
        ],
        out_specs=pl.BlockSpec((1, SUBLANES, tn), lambda l, j: (l, 0, j)),
        out_shape=jax.ShapeDtypeStruct((depth, SUBLANES, n), F32),
        compiler_params=_cparams(("parallel", "parallel")),
        name="adaln_mods",
    )(cvec, ada_w, ada_b.reshape(depth, 1, n))


def _modulated_norm(x, g, sc, sh):
    ms = jnp.mean(x * x, axis=-1, keepdims=True)
    return (x * lax.rsqrt(ms + NORM_EPS) * g) * (1.0 + sc) + sh


def _inproj_kernel(h_ref, g_ref, sc_ref, sh_ref, w_ref, o_ref):
    u = _modulated_norm(h_ref[...], g_ref[...], sc_ref[...], sh_ref[...]).astype(BF16)
    o_ref[...] = _dot(u, w_ref[...])


def _inproj(h, gain, mods, w, l, cfg):
    rows, d = h.shape
    n = w.shape[2]
    tm = cfg["tm"] // 2
    modrow = cfg["modrow"](tm)
    return pl.pallas_call(
        _inproj_kernel,
        grid=(rows // tm,),
        in_specs=[
            pl.BlockSpec((tm, d), lambda i: (i, 0)),
            pl.BlockSpec((None, 1, d), lambda i: (l, 0, 0)),
            pl.BlockSpec((None, None, None, 1, d), lambda i: (l, modrow(i), 1, 0, 0)),
            pl.BlockSpec((None, None, None, 1, d), lambda i: (l, modrow(i), 0, 0, 0)),
            pl.BlockSpec((None, d, n), lambda i: (l, 0, 0), pipeline_mode=pl.Buffered(1)),
        ],
        out_specs=pl.BlockSpec((tm, n), lambda i: (i, 0)),
        out_shape=jax.ShapeDtypeStruct((rows, n), F32),
        compiler_params=_cparams(("parallel",)),
        name="in_proj",
    )(h, gain, mods, mods, w)


def _qkprep_kernel(p_ref, gain_ref, cos_ref, sin_ref, e_ref, o_ref, *, qk_w):
    x = p_ref[...]
    width = x.shape[1]
    ss = _group_sum(x * x, e_ref)
    y = x * lax.rsqrt(ss * (1.0 / HEAD_DIM) + NORM_EPS) * gain_ref[0]
    cos = cos_ref[...]
    sin = sin_ref[...]
    lane = lax.broadcasted_iota(jnp.int32, (x.shape[0], LANES), 1)
    first_half = (lane & (HEAD_DIM // 2)) == 0
    for g in range(width // LANES):
        sl = slice(g * LANES, (g + 1) * LANES)
        if g * LANES < qk_w:
            yg = y[:, sl]
            partner = jnp.where(first_half, pltpu.roll(yg, LANES - HEAD_DIM // 2, 1),
                                pltpu.roll(yg, HEAD_DIM // 2, 1))
            o_ref[0, :, sl] = (yg * cos + partner * sin).astype(BF16)
        else:
            o_ref[0, :, sl] = x[:, sl].astype(BF16)


def _qkprep(p, gains, cos_t, sin_t, e_mat, cfg):
    rows = p.shape[0]
    tm, aw = cfg["tm"], cfg["attn_w"]
    first_blk = cfg["attn_col0"] // aw
    rope_blk = cfg["rope_blk"]
    return pl.pallas_call(
        functools.partial(_qkprep_kernel, qk_w=cfg["qk_w"]),
        grid=(rows // tm, 2),
        in_specs=[
            pl.BlockSpec((tm, aw), lambda i, s: (i, first_blk + s)),
            pl.BlockSpec((1, 1, aw), lambda i, s: (s, 0, 0)),
            pl.BlockSpec((tm, LANES), lambda i, s: (rope_blk(i), 0)),
            pl.BlockSpec((tm, LANES), lambda i, s: (rope_blk(i), 0)),
            pl.BlockSpec(e_mat.shape, lambda i, s: (0, 0)),
        ],
        out_specs=pl.BlockSpec((1, tm, aw), lambda i, s: (s, i, 0)),
        out_shape=jax.ShapeDtypeStruct((2, rows, aw), BF16),
        compiler_params=_cparams(("parallel", "parallel")),
        name="qk_prep",
    )(p, gains, cos_t, sin_t, e_mat)


def _attn_kernel(sink_ref, q_ref, kc_ref, vc_ref, kl_ref, vl_ref, o_ref, *,
                 window, n_heads, group, tq, tk, seq):
    assert sink_ref is None and not window
    i = pl.program_id(1)
    is_lat = i < seq // tq
    hd = HEAD_DIM
    n_kv = n_heads // group
    den_lane = [((j + 1) % n_kv) * hd for j in range(n_kv)]

    def with_ones(vblk, j):
        lane = lax.broadcasted_iota(jnp.int32, vblk.shape, 1)
        ones_col = jnp.where(lane == den_lane[j], 1.0, 0.0).astype(BF16)
        return jnp.where((lane >= j * hd) & (lane < (j + 1) * hd), vblk, ones_col)

    def padded_q(j):
        zeros = jnp.zeros((tq, hd), BF16)
        qs = []
        for g in range(group):
            h = j * group + g
            parts = [zeros] * n_kv
            parts[j] = q_ref[0, :, h * hd:(h + 1) * hd]
            qs.append(jnp.concatenate(parts, axis=1))
        return jnp.concatenate(qs, axis=0)

    def finish(acc, j):
        o = acc[:, j * hd:(j + 1) * hd] / acc[:, den_lane[j]:den_lane[j] + 1]
        for g in range(group):
            c0 = (j * group + g) * hd
            o_ref[:, c0:c0 + hd] = o[g * tq:(g + 1) * tq].astype(BF16)

    @pl.when(is_lat)
    def _():
        for j in range(n_kv):
            q = padded_q(j)
            s_ctx = _dot(q, kc_ref[0], trans_b=True)
            s_lat = _dot(q, kl_ref[0, 0:tk, :], trans_b=True)
            m = jnp.maximum(jnp.max(s_ctx, axis=-1, keepdims=True), jnp.max(s_lat, axis=-1, keepdims=True))
            acc = (_dot(jnp.exp2(s_ctx - m).astype(BF16), with_ones(vc_ref[0], j))
                   + _dot(jnp.exp2(s_lat - m).astype(BF16), with_ones(vl_ref[0, 0:tk, :], j)))

            def body(kb, carry, q=q, j=j):
                m, acc = carry
                off = pl.multiple_of(kb * tk, tk)
                s = _dot(q, kl_ref[0, pl.ds(off, tk), :], trans_b=True)
                m_new = jnp.maximum(m, jnp.max(s, axis=-1, keepdims=True))
                p = jnp.exp2(s - m_new).astype(BF16)
                acc_new = jnp.exp2(m - m_new) * acc + _dot(p, with_ones(vl_ref[0, pl.ds(off, tk), :], j))
                return m_new, acc_new

            m, acc = lax.fori_loop(1, seq // tk, body, (m, acc))
            finish(acc, j)

    @pl.when(jnp.logical_not(is_lat))
    def _():
        for j in range(n_kv):
            s_ctx = _dot(padded_q(j), kc_ref[0], trans_b=True)
            p = jnp.exp2(s_ctx - jnp.max(s_ctx, axis=-1, keepdims=True)).astype(BF16)
            finish(_dot(p, with_ones(vc_ref[0], j)), j)


def _window_attn_kernel(sink_ref, q_ref, kc_ref, vc_ref, kl_ref, vl_ref, o_ref, *,
                        window, n_heads, group, tq, tk, seq):
    del window, tk
    i = pl.program_id(1)
    is_lat = i < seq // tq
    hd = HEAD_DIM
    n_kv = n_heads // group
    rows = group * tq
    span = tq + 2 * WINDOW
    den_lane = [((j + 1) % n_kv) * hd for j in range(n_kv)]
    start = pl.multiple_of(jnp.clip(i * tq - WINDOW, 0, seq - span), WINDOW)
    delta = (lax.broadcasted_iota(jnp.int32, (tq, span), 0) - lax.broadcasted_iota(jnp.int32, (tq, span), 1)
             + (i * tq - start))
    bias = jnp.where(jnp.abs(delta) <= jnp.where(is_lat, WINDOW, -1), 0.0, NEG_INF)
    k_ctx, v_ctx = kc_ref[0], vc_ref[0]
    k_win, v_win = kl_ref[0, pl.ds(start, span), :], vl_ref[0, pl.ds(start, span), :]

    def with_ones(vblk, j):
        lane = lax.broadcasted_iota(jnp.int32, vblk.shape, 1)
        ones_col = jnp.where(lane == den_lane[j], 1.0, 0.0).astype(BF16)
        return jnp.where((lane >= j * hd) & (lane < (j + 1) * hd), vblk, ones_col)

    zeros = jnp.zeros((tq, hd), BF16)
    for j in range(n_kv):
        qs = []
        for g in range(group):
            h = j * group + g
            parts = [zeros] * n_kv
            parts[j] = q_ref[0, :, h * hd:(h + 1) * hd]
            qs.append(jnp.concatenate(parts, axis=1))
        q = jnp.concatenate(qs, axis=0)
        m0 = jnp.concatenate(
            [jnp.full((tq, 1), sink_ref[j * group + g] * LOG2E, F32) for g in range(group)], axis=0)
        s_ctx = _dot(q, k_ctx, trans_b=True)
        s_win = (_dot(q, k_win, trans_b=True).reshape(group, tq, span) + bias[None]).reshape(rows, span)
        m = jnp.maximum(m0, jnp.maximum(jnp.max(s_ctx, axis=-1, keepdims=True),
                                        jnp.max(s_win, axis=-1, keepdims=True)))
        acc = (_dot(jnp.exp2(s_ctx - m).astype(BF16), with_ones(v_ctx, j))
               + _dot(jnp.exp2(s_win - m).astype(BF16), with_ones(v_win, j)))
        den = acc[:, den_lane[j]:den_lane[j] + 1] + jnp.exp2(m0 - m)
        o = acc[:, j * hd:(j + 1) * hd] / den
        for g in range(group):
            c0 = (j * group + g) * hd
            o_ref[:, c0:c0 + hd] = o[g * tq:(g + 1) * tq].astype(BF16)


def _attention(qkv, sec, sink, cfg, window):
    rows = qkv.shape[1]
    b, seq, ctx, tq = cfg["batch"], cfg["seq"], cfg["ctx"], cfg["tq"]
    n_heads, n_kv = cfg["attn_heads"], cfg["attn_kv"]
    q_w = n_heads * HEAD_DIM
    kv_w = n_kv * HEAD_DIM
    assert kv_w == LANES and q_w % kv_w == 0 and ctx == tq
    n_lat_tiles = seq // tq
    k_blk = q_w // kv_w
    ctx_blk0 = b * seq // ctx

    def q_map(bi, i, *_):
        return (sec, jnp.where(i < n_lat_tiles, bi * n_lat_tiles + i, ctx_blk0 + bi), 0)

    def o_map(bi, i, *_):
        return (jnp.where(i < n_lat_tiles, bi * n_lat_tiles + i, ctx_blk0 + bi), 0)

    tk = cfg["tk"]
    assert seq % tk == 0 and tq % WINDOW == 0 and tq + 2 * WINDOW <= seq
    assert (sink is not None) == window
    kernel = functools.partial(_window_attn_kernel if window else _attn_kernel, window=window, n_heads=n_heads,
                               group=n_heads // n_kv, tq=tq, tk=tk, seq=seq)
    in_specs = [
        pl.BlockSpec((1, tq, q_w), q_map),
        pl.BlockSpec((1, ctx, kv_w), lambda bi, i, *_: (sec, ctx_blk0 + bi, k_blk)),
        pl.BlockSpec((1, ctx, kv_w), lambda bi, i, *_: (sec, ctx_blk0 + bi, k_blk + 1)),
        pl.BlockSpec((1, seq, kv_w), lambda bi, i, *_: (sec, bi, k_blk)),
        pl.BlockSpec((1, seq, kv_w), lambda bi, i, *_: (sec, bi, k_blk + 1)),
    ]
    args = [qkv, qkv, qkv, qkv, qkv]
    if sink is not None:
        in_specs = [pl.BlockSpec(memory_space=pltpu.SMEM)] + in_specs
        args = [sink] + args
    else:
        kernel = functools.partial(kernel, None)
    return pl.pallas_call(
        kernel,
        grid=(b, n_lat_tiles + 1),
        in_specs=in_specs,
        out_specs=pl.BlockSpec((tq, q_w), o_map),
        out_shape=jax.ShapeDtypeStruct((rows, q_w), BF16),
        compiler_params=_cparams(("parallel", "arbitrary")),
        name="window_attn" if window else "global_attn",
    )(*args)


def _tri3(tri, x):
    h1 = x.astype(BF16)
    r1 = x - h1.astype(F32)
    h2 = r1.astype(BF16)
    h3 = (r1 - h2.astype(F32)).astype(BF16)
    return _dot(tri, h1) + (_dot(tri, h2) + _dot(tri, h3))


def _rwkv_prep_kernel(cur_ref, prev_ref, next_ref, mu_ref, w0_ref, wup_ref, a0_ref, aup_ref, gup_ref,
                      kk_ref, ka_ref, rk_ref, e_ref, tril_ref, triu_ref,
                      sv_ref, sf_ref, sb_ref, etot_ref, bonus_ref, gate_ref, *,
                      bw, seg_lat, seg_ctx, n_lat_tiles):
    i = pl.program_id(0)
    p = cur_ref[...]
    tr = p.shape[0]
    lat = i < n_lat_tiles
    seg = jnp.where(lat, seg_lat, seg_ctx)
    pos = jnp.where(lat, i, i - n_lat_tiles) % seg
    row = lax.broadcasted_iota(jnp.int32, (tr, 1), 0)
    prev_row = jnp.where(pos != 0, prev_ref[SUBLANES - 1:SUBLANES, :], 0.0)
    next_row = jnp.where(pos != seg - 1, next_ref[0:1, :], 0.0)
    prv = jnp.where(row == 0, prev_row, pltpu.roll(p, 1, 0))
    nxt = jnp.where(row == tr - 1, next_row, pltpu.roll(p, tr - 1, 0))
    xs = p + mu_ref[0:1, :] * (prv - p) + mu_ref[1:2, :] * (nxt - p)

    r = xs[:, 0:bw]
    k = xs[:, bw:2 * bw]
    v = xs[:, 2 * bw:3 * bw]
    lora = xs[:, 3 * bw:]
    n_hp = bw // LANES

    kk = k * kk_ref[...]
    kk = kk * lax.rsqrt(jnp.maximum(_group_sum(kk * kk, e_ref), 1e-24))
    ksum = jnp.zeros_like(k)
    for d, (s_ref, tri_ref) in enumerate(((sf_ref, tril_ref), (sb_ref, triu_ref))):
        wd = lora[:, d * LORA_PAD:(d + 1) * LORA_PAD]
        ad = lora[:, (2 + d) * LORA_PAD:(3 + d) * LORA_PAD]
        log_decay = -DECAY_SCALE * jax.nn.sigmoid(w0_ref[d:d + 1, :] + _dot3(jnp.tanh(wd), wup_ref[d]))
        a = jax.nn.sigmoid(a0_ref[d:d + 1, :] + _dot3(ad, aup_ref[d]))
        key = k * (1.0 + (a - 1.0) * ka_ref[...])
        ksum = ksum + key
        kka = kk * a
        cum = _tri3(tri_ref[...], log_decay)
        last = CHUNK - 1 if d == 0 else 0
        tot_rows = [cum[c * CHUNK + last:c * CHUNK + last + 1] for c in range(tr // CHUNK)]
        e_inv = jnp.exp(-cum)
        streams = (kk * jnp.exp(cum - log_decay), r * jnp.exp(cum), kka * e_inv, key * e_inv)
        for n, st in enumerate(streams):
            st = st.astype(BF16)
            for hp in range(n_hp):
                s_ref[n, hp] = st[:, hp * LANES:(hp + 1) * LANES]
        for cidx in range(tr // CHUNK):
            e_tot = jnp.exp(tot_rows[cidx])
            etot_ref[d, cidx] = jnp.concatenate(
                [e_tot[:, hp * LANES:(hp + 1) * LANES] for hp in range(n_hp)], axis=0)
    vb = v.astype(BF16)
    for hp in range(n_hp):
        sv_ref[hp] = vb[:, hp * LANES:(hp + 1) * LANES]
    bonus_ref[...] = (_group_sum(r * ksum * rk_ref[...], e_ref) * v).astype(BF16)
    gate_ref[...] = _dot3(jax.nn.sigmoid(lora[:, 4 * LORA_PAD:]), gup_ref[...]).astype(BF16)


def _chunk_block_diag(tr, kind):
    t = jnp.arange(tr)
    same = (t[:, None] // CHUNK) == (t[None, :] // CHUNK)
    if kind == "lower":
        same = same & (t[None, :] <= t[:, None])
    elif kind == "upper":
        same = same & (t[None, :] >= t[:, None])
    return same.astype(BF16)


N_STREAMS = 4


def _rwkv_prep(p, lw, e_mat, cfg):
    rows = p.shape[0]
    bw, tr = cfg["b_width"], cfg["tr"]
    cw = 3 * bw + cfg["lora_w"]
    n_hp = bw // LANES
    n_lat_tiles = cfg["batch"] * cfg["seq"] // tr
    hb = tr // SUBLANES
    last8 = rows // SUBLANES - 1
    full = lambda a: pl.BlockSpec(a.shape, lambda i: (0,) * a.ndim)
    consts = [lw["mu"], lw["w0"], lw["w_up"], lw["a0"], lw["a_up"], lw["g_up"], lw["k_k"], lw["k_a"], lw["r_k"],
              e_mat, _chunk_block_diag(tr, "lower"), _chunk_block_diag(tr, "upper")]
    stream_spec = pl.BlockSpec((N_STREAMS, n_hp, tr, LANES), lambda i: (0, 0, i, 0))
    stream_shape = jax.ShapeDtypeStruct((N_STREAMS, n_hp, rows, LANES), BF16)
    cpt = tr // CHUNK
    kernel = functools.partial(_rwkv_prep_kernel, bw=bw, seg_lat=cfg["seq"] // tr, seg_ctx=cfg["ctx"] // tr,
                               n_lat_tiles=n_lat_tiles)
    return pl.pallas_call(
        kernel,
        grid=(rows // tr,),
        in_specs=[
            pl.BlockSpec((tr, cw), lambda i: (i, 0)),
            pl.BlockSpec((SUBLANES, cw), lambda i: (jnp.maximum(i * hb - 1, 0), 0)),
            pl.BlockSpec((SUBLANES, cw), lambda i: (jnp.minimum((i + 1) * hb, last8), 0)),
        ] + [full(a) for a in consts],
        out_specs=[pl.BlockSpec((n_hp, tr, LANES), lambda i: (0, i, 0)),
                   stream_spec, stream_spec,
                   pl.BlockSpec((2, cpt, n_hp, LANES), lambda i: (0, i, 0, 0)),
                   pl.BlockSpec((tr, bw), lambda i: (i, 0)),
                   pl.BlockSpec((tr, bw), lambda i: (i, 0))],
        out_shape=[jax.ShapeDtypeStruct((n_hp, rows, LANES), BF16),
                   stream_shape, stream_shape,
                   jax.ShapeDtypeStruct((2, rows // CHUNK, n_hp, LANES), F32),
                   jax.ShapeDtypeStruct((rows, bw), BF16),
                   jax.ShapeDtypeStruct((rows, bw), BF16)],
        compiler_params=_cparams(("parallel",)),
        name="rwkv_prep",
    )(p, p, p, *consts)


def _block_diag(x):
    lo = lax.broadcasted_iota(jnp.int32, x.shape, 1) < HEAD_DIM
    zero = jnp.zeros_like(x)
    return jnp.concatenate([jnp.where(lo, x, zero), jnp.where(lo, zero, x)], axis=0)


def _scan_step(dirs, ones_ref, z_ref, n_hp):
    c, hd = CHUNK, HEAD_DIM
    assert c == hd and LANES == 2 * hd
    row = lax.broadcasted_iota(jnp.int32, (c, LANES), 0)
    lane = lax.broadcasted_iota(jnp.int32, (c, LANES), 1)
    col = lane & (hd - 1)
    lo = lane < hd
    eye = col == row
    inst = [(d, hp) for d in range(len(dirs)) for hp in range(n_hp)]
    pairs = range(len(inst))
    incl = [(col >= row) if dirs[d][4] else (col <= row) for d, _ in inst]
    strict = [(col > row) if dirs[d][4] else (col < row) for d, _ in inst]

    def half(x, h):
        keep = lo if h == 0 else ~lo
        return jnp.where(keep, x, jnp.zeros_like(x))

    qk, rt, bt, kt = ([dirs[d][1][k, hp] for d, hp in inst] for k in range(N_STREAMS))
    e_row = [dirs[d][2][hp:hp + 1, :] for d, hp in inst]
    bh = [(bt[i].astype(F32) * e_row[i]).astype(BF16) for i in pairs]
    kh = [(kt[i].astype(F32) * e_row[i]).astype(BF16) for i in pairs]
    vv = [dirs[d][0][hp] for d, hp in inst]
    p1 = [_dot(jnp.concatenate([qk[i], rt[i]], axis=0),
               jnp.concatenate([half(bt[i], 0), half(bt[i], 1), half(kt[i], 0), half(kt[i], 1)], axis=0),
               trans_b=True) for i in pairs]
    same_blk = (row // SUB) == (col // SUB)
    a_ab = [jnp.where(strict[i], p[:c, :LANES], 0.0) for i, p in enumerate(p1)]
    x_pow = [jnp.where(same_blk, -a, 0.0) for a in a_ab]
    u = x_pow
    x_pow = [_dot(x.astype(BF16), _block_diag(x.astype(BF16))) for x in x_pow]
    sq = 2
    while sq < SUB:
        w = [_block_diag(x.astype(BF16)) for x in x_pow]
        if 2 * sq < SUB:
            prod = [_dot(jnp.concatenate([u[i], x_pow[i]], axis=0).astype(BF16), w[i]) for i in pairs]
            u = [u[i] + x_pow[i] + prod[i][:c] for i in pairs]
            x_pow = [pr[c:] for pr in prod]
        else:
            u = [u[i] + x_pow[i] + _dot(u[i].astype(BF16), w[i]) for i in pairs]
        sq *= 2
    u_bf = [x.astype(BF16) for x in u]
    n_off = [jnp.where(same_blk, 0.0, a) for a in a_ab]
    m1_bf = [(-(n_off[i] + _dot(u_bf[i], _block_diag(n_off[i].astype(BF16))))).astype(BF16) for i in pairs]
    m2_bf = [_dot(m1_bf[i], _block_diag(m1_bf[i])).astype(BF16) for i in pairs]
    avy = [_dot(jnp.concatenate([jnp.where(strict[i], p[:c, LANES:], 0.0), jnp.where(incl[i], p[c:, LANES:], 0.0)],
                                axis=0).astype(BF16), _block_diag(vv[i])) for i, p in enumerate(p1)]

    def pair_cols(x):
        return jnp.concatenate([_block_diag(x[:, :LANES].astype(BF16)), _block_diag(x[:, LANES:].astype(BF16))],
                               axis=1)

    r = [jnp.concatenate([qk[i].astype(F32), avy[i][:c]], axis=1) for i in pairs]
    r = [r[i] + _dot(u_bf[i], pair_cols(r[i])) for i in pairs]
    r = [r[i] + _dot(m2_bf[i], pair_cols(r[i])) for i in pairs]
    tu = [(r[i] + _dot(m1_bf[i], pair_cols(r[i]))).astype(BF16) for i in pairs]
    bu = [_dot(jnp.where(incl[i], p[c:, :LANES], 0.0).astype(BF16),
               jnp.concatenate([_block_diag(tu[i][:, :LANES]), _block_diag(tu[i][:, LANES:])], axis=1))
          for i, p in enumerate(p1)]
    rh = [(rt[i].astype(F32) - bu[i][:, :LANES]).astype(BF16) for i in pairs]
    yh = [avy[i][c:] - bu[i][:, LANES:] for i in pairs]
    rp = [_dot(bh[i], tu[i], trans_a=True) for i in pairs]
    kv = [_dot(kh[i], vv[i], trans_a=True) for i in pairs]
    g = [jnp.where(lo, r[:hd, :LANES], r[hd:, :LANES]).astype(BF16) for r in rp]
    hc = [jnp.where(lo, kv[i][:hd], kv[i][hd:]) - jnp.where(lo, rp[i][:hd, LANES:], rp[i][hd:, LANES:])
          for i in pairs]
    ones2 = jnp.concatenate([ones_ref[...], ones_ref[...]], axis=0)
    e_col = []
    for i in pairs:
        e_diag = jnp.where(eye, jnp.broadcast_to(e_row[i], (c, LANES)), 0.0)
        e_col.append(_dot(jnp.concatenate(_split(e_diag), axis=1), ones2))
    z = [z_ref[i] for i in pairs]
    yz = [_dot(jnp.concatenate([rh[i], g[i]], axis=0), _block_diag(z[i].astype(BF16))) for i in pairs]
    for i, (d, hp) in enumerate(inst):
        dirs[d][3][hp] = (yh[i] + yz[i][:c]).astype(BF16)
        z_ref[i] = e_col[i] * z[i] - yz[i][c:] + hc[i]


def _rwkv_scan_kernel(vf_ref, sf_ref, ef_ref, vb_ref, sb_ref, eb_ref, ones_ref, yf_ref, yb_ref, z_ref, *, n_hp):
    @pl.when(pl.program_id(1) == 0)
    def _():
        z_ref[...] = jnp.zeros_like(z_ref)

    _scan_step([(vf_ref, sf_ref, ef_ref, yf_ref, False), (vb_ref, sb_ref, eb_ref, yb_ref, True)],
               ones_ref, z_ref, n_hp)


def _rwkv_scan(s_v, s_f, s_b, e_tot, cfg):
    n_hp, rows, _ = s_v.shape
    b, seq, ctx = cfg["batch"], cfg["seq"], cfg["ctx"]
    c = CHUNK
    ncc, ncl = ctx // c, seq // c
    ctx0 = b * seq // c

    def fwd_blk(bi, j):
        return jnp.where(j < ncc, ctx0 + bi * ncc + j, bi * ncl + (j - ncc))

    def bwd_blk(bi, j):
        return jnp.where(j < ncc, ctx0 + bi * ncc + (ncc - 1 - j), bi * ncl + (ncl - 1 - (j - ncc)))

    def specs(blk, d):
        return [pl.BlockSpec((n_hp, c, LANES), lambda bi, j: (0, blk(bi, j), 0)),
                pl.BlockSpec((N_STREAMS, n_hp, c, LANES), lambda bi, j: (0, 0, blk(bi, j), 0)),
                pl.BlockSpec((None, None, n_hp, LANES), lambda bi, j: (d, blk(bi, j), 0, 0))]

    out_f = pl.BlockSpec((n_hp, c, LANES), lambda bi, j: (0, fwd_blk(bi, j), 0))
    out_b = pl.BlockSpec((n_hp, c, LANES), lambda bi, j: (0, bwd_blk(bi, j), 0))
    y_shape = jax.ShapeDtypeStruct((n_hp, rows, LANES), BF16)
    return pl.pallas_call(
        functools.partial(_rwkv_scan_kernel, n_hp=n_hp),
        grid=(b, ncc + ncl),
        in_specs=specs(fwd_blk, 0) + specs(bwd_blk, 1) + [pl.BlockSpec((LANES, LANES), lambda bi, j: (0, 0))],
        out_specs=[out_f, out_b],
        out_shape=[y_shape, y_shape],
        scratch_shapes=[pltpu.VMEM((2 * n_hp, HEAD_DIM, LANES), F32)],
        compiler_params=_cparams(("parallel", "arbitrary")),
        name="rwkv_scan",
    )(s_v, s_f, e_tot, s_v, s_b, e_tot, _block_ones(LANES))


def _outproj_kernel(oa_ref, yf_ref, yb_ref, bonus_ref, gate_ref, gg_ref, gb_ref, e_ref, oc_ref, w_ref, h_ref,
                    g_ref, n2_ref, sc_ref, sh_ref, o_ref, u_ref):
    n_hp = yf_ref.shape[0]
    y = jnp.concatenate([yf_ref[hp].astype(F32) + yb_ref[hp].astype(F32) for hp in range(n_hp)], axis=1)
    mu = _group_sum(y, e_ref) * (1.0 / HEAD_DIM)
    yc = y - mu
    var = _group_sum(yc * yc, e_ref) * (1.0 / HEAD_DIM)
    yn = yc * lax.rsqrt(var + GN_EPS) * gg_ref[...] + gb_ref[...]
    o_b = ((yn + bonus_ref[...].astype(F32)) * gate_ref[...].astype(F32)).astype(BF16)
    mix = jnp.concatenate([oa_ref[...], o_b, oc_ref[...]], axis=1)
    h_new = h_ref[...] + g_ref[...] * _dot(mix, w_ref[...])
    o_ref[...] = h_new
    u_ref[...] = _modulated_norm(h_new, n2_ref[...], sc_ref[...], sh_ref[...]).astype(BF16)


def _outproj(o_a, y_f, y_b, bonus, gate, gn_g, gn_b, e_mat, o_c, w, h, mods, norm2_g, l, n_rows, cfg):
    d = h.shape[1]
    n_hp = y_f.shape[0]
    bw, tm = cfg["b_width"], cfg["tm"] // 2
    modrow = cfg["modrow"](tm)
    rows_of = lambda width: pl.BlockSpec((tm, width), lambda i: (i, 0))
    y_spec = pl.BlockSpec((n_hp, tm, LANES), lambda i: (0, i, 0))
    vec_spec = pl.BlockSpec((None, 1, bw), lambda i: (l, 0, 0))
    mod = lambda k: pl.BlockSpec((None, None, None, 1, d), lambda i: (l, modrow(i), k, 0, 0))
    return pl.pallas_call(
        _outproj_kernel,
        grid=(n_rows // tm,),
        in_specs=[rows_of(o_a.shape[1]), y_spec, y_spec, rows_of(bw), rows_of(bw), vec_spec, vec_spec,
                  pl.BlockSpec(e_mat.shape, lambda i: (0, 0)),
                  rows_of(o_c.shape[1]),
                  pl.BlockSpec((None, d, d), lambda i: (l, 0, 0)),
                  rows_of(d),
                  mod(2),
                  pl.BlockSpec((None, 1, d), lambda i: (l, 0, 0)),
                  mod(4), mod(3)],
        out_specs=[rows_of(d), rows_of(d)],
        out_shape=[jax.ShapeDtypeStruct((n_rows, d), F32), jax.ShapeDtypeStruct((n_rows, d), BF16)],
        compiler_params=_cparams(("parallel",)),
        name="out_proj",
    )(o_a, y_f, y_b, bonus, gate, gn_g, gn_b, e_mat, o_c, w, h, mods, norm2_g, mods, mods)


def _mlp_kernel(u_ref, hcol_ref, gate_ref, w1_ref, w2_ref, o_ref, mid_scr, *, n_up):
    s = pl.program_id(1)

    @pl.when(s < n_up)
    def _():
        a = jnp.maximum(_dot(u_ref[...], w1_ref[...]), 0.0)
        mid_scr[s] = (a * a).astype(BF16)

    @pl.when(s >= n_up)
    def _():
        mid = jnp.concatenate([mid_scr[k] for k in range(n_up)], axis=1)
        o_ref[...] = hcol_ref[...] + gate_ref[...] * _dot(mid, w2_ref[...])


def _mlp(h, u, mods, w1, w2, l, cfg):
    rows, d = h.shape
    dff = w1.shape[2]
    tm, tf, tn = cfg["tm"], 2048, 512
    n_up = dff // tf
    modrow = cfg["modrow"](tm)
    col = lambda s: jnp.maximum(s - n_up, 0)
    return pl.pallas_call(
        functools.partial(_mlp_kernel, n_up=n_up),
        grid=(rows // tm, n_up + d // tn),
        in_specs=[
            pl.BlockSpec((tm, d), lambda i, s: (i, 0)),
            pl.BlockSpec((tm, tn), lambda i, s: (i, col(s))),
            pl.BlockSpec((None, None, None, 1, tn), lambda i, s: (l, modrow(i), 5, 0, col(s))),
            pl.BlockSpec((None, d, tf), lambda i, s: (l, 0, jnp.minimum(s, n_up - 1))),
            pl.BlockSpec((None, dff, tn), lambda i, s: (l, 0, col(s))),
        ],
        out_specs=pl.BlockSpec((tm, tn), lambda i, s: (i, col(s))),
        out_shape=jax.ShapeDtypeStruct((rows, d), F32),
        scratch_shapes=[pltpu.VMEM((n_up, tm, tf), BF16)],
        compiler_params=_cparams(("parallel", "arbitrary")),
        name="mlp",
    )(u, h, mods, w1, w2)


def _rope_tables(seq, tm):
    rows = seq // GRID_W
    row = jnp.broadcast_to(jnp.arange(rows)[:, None], (rows, GRID_W)).reshape(-1)
    col = jnp.broadcast_to(jnp.arange(GRID_W)[None, :], (rows, GRID_W)).reshape(-1)
    n_freq = HEAD_DIM // 4
    inv = ROPE_THETA ** (-jnp.arange(n_freq, dtype=F32) / n_freq)
    ang = jnp.concatenate([row[:, None].astype(F32) * inv, col[:, None].astype(F32) * inv], -1)
    cos, sin = jnp.cos(ang), jnp.sin(ang)
    reps = LANES // HEAD_DIM
    cos_t = jnp.tile(jnp.concatenate([cos, cos], -1), (1, reps))
    sin_t = jnp.tile(jnp.concatenate([-sin, sin], -1), (1, reps))
    cos_t = jnp.concatenate([cos_t, jnp.ones((tm, LANES), F32)], 0)
    sin_t = jnp.concatenate([sin_t, jnp.zeros((tm, LANES), F32)], 0)
    return cos_t, sin_t


def _block_ones(width):
    g = jnp.arange(width) // HEAD_DIM
    return (g[:, None] == g[None, :]).astype(BF16)


def kernel(x, c, ctx, c_ctx, ada_w, ada_b, norm1_g, norm2_g, w_in, a_q_norm, a_k_norm, a_sink, c_q_norm,
           c_k_norm, shift_mu, decay_w0, decay_up, iclr_a0, iclr_up, gate_up, k_k, k_a, r_k, gn_g, gn_b,
           w_out, mlp_w1, mlp_w2):
    batch, seq, d = x.shape
    n_ctx = ctx.shape[1]
    depth = ada_w.shape[0]
    bw = k_k.shape[1]
    lora_d, lora_i, lora_g = decay_up.shape[2], iclr_up.shape[2], gate_up.shape[1]
    a_heads = a_sink.shape[1]
    a_kv = a_heads // 4
    q_w, kv_w = a_heads * HEAD_DIM, a_kv * HEAD_DIM
    attn_w = q_w + 2 * kv_w
    lora_w = 4 * LORA_PAD + lora_g
    b_in = 3 * bw + 2 * lora_d + 2 * lora_i + lora_g
    assert lora_d <= LORA_PAD and lora_i <= LORA_PAD
    assert w_in.shape[2] == 2 * attn_w + b_in

    tm = batch * n_ctx
    n_lat_tiles = batch * seq // tm
    tiles_per_batch = seq // tm
    cfg = dict(
        batch=batch, seq=seq, ctx=n_ctx, tm=tm, tr=n_ctx, tq=n_ctx, tk=min(seq, 2048),
        b_width=bw, lora_w=lora_w, attn_w=attn_w, attn_col0=3 * bw + lora_w, qk_w=q_w + kv_w,
        attn_heads=a_heads, attn_kv=a_kv,
        modrow=lambda t: (lambda i: jnp.where(i < batch * seq // t, i // (seq // t), batch)),
        rope_blk=lambda i: jnp.where(i < n_lat_tiles, i % tiles_per_batch, tiles_per_batch),
    )
    assert seq % tm == 0 and seq % GRID_W == 0 and n_ctx % CHUNK == 0 and (3 * bw) % lora_w == 0
    assert cfg["attn_col0"] % attn_w == 0 and batch + 1 <= SUBLANES

    def relayout_cols(m):
        a_part, b_part, c_part = m[..., :attn_w], m[..., attn_w:attn_w + b_in], m[..., attn_w + b_in:]
        pad = lambda z, n: jnp.pad(z, [(0, 0)] * (z.ndim - 1) + [(0, n - z.shape[-1])])
        o = 3 * bw
        pieces = [b_part[..., :o]]
        for width in (lora_d, lora_d, lora_i, lora_i):
            pieces.append(pad(b_part[..., o:o + width], LORA_PAD))
            o += width
        pieces.append(b_part[..., o:])
        return jnp.concatenate(pieces + [a_part, c_part], -1)

    w_in_r = relayout_cols(w_in).astype(BF16)
    mu_r = relayout_cols(jnp.pad(shift_mu, ((0, 0), (0, 0), (attn_w, attn_w))))[..., :3 * bw + lora_w]
    pad_rows = lambda z: jnp.pad(z, ((0, 0), (0, 0), (0, LORA_PAD - z.shape[2]), (0, 0)))
    w_up_r, a_up_r = pad_rows(decay_up), pad_rows(iclr_up)
    w_out_b = w_out.astype(BF16)
    w1_b, w2_b = mlp_w1.astype(BF16), mlp_w2.astype(BF16)

    scale = HEAD_DIM ** -0.5 * LOG2E
    tile = lambda g, n: jnp.tile(g, (1, n))

    def gains(qg, kg):
        return jnp.concatenate([tile(qg, a_heads) * scale, tile(kg, a_kv), jnp.ones((depth, kv_w), F32)], -1)

    qk_gains = jnp.stack([gains(a_q_norm, a_k_norm), gains(c_q_norm, c_k_norm)], 1)[:, :, None, :]

    cos_t, sin_t = _rope_tables(seq, tm)
    e_attn = _block_ones(attn_w)
    e_b = _block_ones(2 * LANES)

    cvec = jnp.zeros((SUBLANES, d), F32).at[:batch].set(c).at[batch].set(c_ctx)
    mods = _mods(cvec, ada_w, ada_b).reshape(depth, SUBLANES, 6, 1, d)

    h = jnp.concatenate([x.reshape(batch * seq, d), ctx.reshape(batch * n_ctx, d)], 0)
    n1g, n2g = norm1_g[:, None, :], norm2_g[:, None, :]
    gn_g3, gn_b3 = gn_g[:, None, :], gn_b[:, None, :]
    for l in range(depth):
        p = _inproj(h, n1g, mods, w_in_r, l, cfg)
        qkv = _qkprep(p, qk_gains[l], cos_t, sin_t, e_attn, cfg)
        o_a = _attention(qkv, 0, a_sink[l], cfg, window=True)
        o_c = _attention(qkv, 1, None, cfg, window=False)
        lw = dict(mu=mu_r[l], w0=decay_w0[l], w_up=w_up_r[l], a0=iclr_a0[l], a_up=a_up_r[l], g_up=gate_up[l],
                  k_k=k_k[l][None], k_a=k_a[l][None], r_k=r_k[l][None])
        s_v, s_f, s_b, e_tot, bonus, gate = _rwkv_prep(p, lw, e_b, cfg)
        y_f, y_b = _rwkv_scan(s_v, s_f, s_b, e_tot, cfg)
        n_rows = batch * seq if l == depth - 1 else h.shape[0]
        h, u2 = _outproj(o_a, y_f, y_b, bonus, gate, gn_g3, gn_b3, e_b, o_c, w_out_b, h, mods, n2g, l, n_rows, cfg)
        h = _mlp(h, u2, mods, w1_b, w2_b, l, cfg)
    return h.reshape(batch, seq, d)
```
